```python
import math
import jax
import jax.numpy as jnp
from jax import lax
import numpy as np

D_MODEL = 2048
BATCH = 8
SEQ = 2048
DEPTH = 2
DEC_BATCH = 128
DEC_SEQ = 1
PAST_LEN = 2048
PAGE_SIZE = 128

N_A = DEPTH // 2
N_B = DEPTH - N_A
D_POOL = D_MODEL
POOL_WINDOWS = (2, 4, 8, 16)
N_POOL_GROUPS = len(POOL_WINDOWS)
POOL_GROUP = D_POOL // N_POOL_GROUPS
POOL_BUF = max(POOL_WINDOWS) - 1
HEAD_DIM = 128
N_HEADS = D_MODEL // HEAD_DIM
N_KV = 4
GROUP_REP = N_HEADS // N_KV
D_ATT = N_HEADS * HEAD_DIM
D_KV = N_KV * HEAD_DIM
N_BRANCH = 3
CMP_LEN = 32
CMP_STRIDE = 16
SLC_BLOCK = 64
N_SEL = 8
WINDOW = 512
N_BUCKETS = 32
MAX_DISTANCE = 128
Q_BLOCK = 32
RMS_EPS = 1e-6
SCALE = HEAD_DIM ** -0.5
NEG_INF = -1e30
FORCE_SCORE = 1e9

kernel_name = 'yoco_pool_nsa_decode_step'


def rms_norm(x, g):
    xf = x.astype(jnp.float32)
    y = xf * lax.rsqrt(jnp.mean(xf * xf, axis=-1, keepdims=True) + RMS_EPS)
    return (y * g.astype(jnp.float32)).astype(x.dtype)


def ada_modulate(c, w_ada, b_ada):
    mod = jax.nn.silu(c) @ w_ada + b_ada
    shift, scale, gate = jnp.split(mod, 3, axis=-1)
    return shift[:, None], scale[:, None], gate[:, None]


def rel_bucket(dist):
    max_exact = N_BUCKETS // 2
    d = jnp.maximum(dist, 0)
    df = jnp.maximum(d, max_exact).astype(jnp.float32)
    large = max_exact + (jnp.log(df / max_exact) / math.log(MAX_DISTANCE / max_exact) * (N_BUCKETS - max_exact)).astype(jnp.int32)
    return jnp.where(d < max_exact, d, jnp.minimum(large, N_BUCKETS - 1))


def pool_mixer(h, prefix, qpos, w_in, w_grp, pool_scale, w_out):
    B, T = h.shape[:2]
    u, z = jnp.split(h @ w_in, 2, axis=-1)
    u_ext = jnp.concatenate([prefix.astype(u.dtype), u], axis=1)
    uf = u_ext.astype(jnp.float32)
    csum = jnp.concatenate([jnp.zeros_like(uf[:, :1]), jnp.cumsum(uf, axis=1)], axis=1)
    hi = csum[:, POOL_BUF + 1:]
    u32 = uf[:, POOL_BUF:]
    outs = []
    for gi, w in enumerate(POOL_WINDOWS):
        sl = slice(gi * POOL_GROUP, (gi + 1) * POOL_GROUP)
        lo = csum[:, POOL_BUF + 1 - w:POOL_BUF + 1 - w + T, sl]
        cnt = jnp.minimum(qpos + 1, w).astype(jnp.float32)[None, :, None]
        outs.append((hi[..., sl] - lo) / cnt - u32[..., sl])
    pooled = jnp.stack(outs, axis=2).astype(h.dtype)
    mixed = jnp.einsum('btgc,gcd->btgd', pooled, w_grp).reshape(B, T, D_POOL)
    y = mixed * pool_scale * jax.nn.silu(z)
    return y @ w_out, u_ext[:, -POOL_BUF:]


def shared_kv(x, g_kv, w_kv):
    B, T = x.shape[:2]
    kv = (rms_norm(x, g_kv) @ w_kv).reshape(B, T, 2 * N_BRANCH, N_KV, HEAD_DIM)
    return [kv[:, :, i] for i in range(2 * N_BRANCH)]


def gather_pages(cache, page_table):
    rows = cache[page_table]
    return rows.reshape(page_table.shape[0], page_table.shape[1] * PAGE_SIZE, N_KV, HEAD_DIM)


def compress(rows, pe, w1, w2):
    B, T = rows.shape[:2]
    n_sub = -(-T // CMP_STRIDE)
    rows = jnp.pad(rows, ((0, 0), (0, n_sub * CMP_STRIDE - T), (0, 0), (0, 0)))
    sub = rows.reshape(B, n_sub, CMP_STRIDE, N_KV, HEAD_DIM)
    blocks = jnp.concatenate([sub[:, :-1], sub[:, 1:]], axis=2) + pe[None, None, :, None, :]
    flat = blocks.transpose(0, 1, 3, 2, 4).reshape(B, n_sub - 1, N_KV, CMP_LEN * HEAD_DIM)
    return jax.nn.silu(flat @ w1) @ w2


def to_blocks(rows):
    B, T = rows.shape[:2]
    n = -(-T // SLC_BLOCK)
    rows = jnp.pad(rows, ((0, 0), (0, n * SLC_BLOCK - T), (0, 0), (0, 0)))
    return rows.reshape(B, n, SLC_BLOCK, N_KV, HEAD_DIM).transpose(0, 3, 1, 2, 4)


def compressed_branch(qg, qpos, kc, vc, rel_bias):
    Tq, n_cmp = qg.shape[1], kc.shape[1]
    end = jnp.arange(n_cmp, dtype=jnp.int32) * CMP_STRIDE + CMP_LEN - 1
    dist = qpos[:, None] - end[None, :]
    vis = dist >= 0
    bias = rel_bias[rel_bucket(dist)].transpose(2, 0, 1).reshape(N_KV, GROUP_REP, Tq, n_cmp)
    s = jnp.einsum('bqgrd,bngd->bgrqn', qg, kc).astype(jnp.float32) * SCALE + bias
    p = jax.nn.softmax(jnp.where(vis, s, NEG_INF), axis=-1)
    p = jnp.where(jnp.any(vis, axis=-1)[:, None], p, 0.0)
    o = jnp.einsum('bgrqn,bngd->bqgrd', p.astype(vc.dtype), vc)
    return o, p


def selection_indices(p_cmp, qpos, n_slc):
    n_cmp = p_cmp.shape[-1]
    cs = jnp.arange(n_cmp, dtype=jnp.int32) * CMP_STRIDE
    bs = jnp.arange(n_slc, dtype=jnp.int32) * SLC_BLOCK
    overlap = jnp.clip(jnp.minimum(cs[:, None] + CMP_LEN, bs[None, :] + SLC_BLOCK) - jnp.maximum(cs[:, None], bs[None, :]), 0, None).astype(jnp.float32) / CMP_LEN
    imp = jnp.einsum('bgrqn,nj->bgqj', p_cmp, overlap)
    j = jnp.arange(n_slc, dtype=jnp.int32)
    cur = qpos // SLC_BLOCK
    forced = (j[None, :] == cur[:, None]) | (j[None, :] == 0)
    valid = j[None, :] <= cur[:, None]
    imp = jnp.where(forced, FORCE_SCORE, jnp.where(valid, imp, -FORCE_SCORE))
    _, idx = lax.top_k(imp, min(N_SEL, n_slc))
    return idx


def selected_branch(qg, qpos, idx, kb, vb, rel_bias):
    B, Tq = qg.shape[:2]
    gather = jax.vmap(jax.vmap(lambda blk, ix: blk[ix]))
    ks = gather(kb, idx)
    vs = gather(vb, idx)
    n_sel = idx.shape[-1]
    kpos = idx[..., None] * SLC_BLOCK + jnp.arange(SLC_BLOCK, dtype=jnp.int32)
    dist = qpos[None, None, :, None, None] - kpos
    tbl = rel_bias.T.reshape(N_KV, GROUP_REP, N_BUCKETS)
    g_ix = jnp.arange(N_KV)[None, :, None, None, None, None]
    r_ix = jnp.arange(GROUP_REP)[None, None, :, None, None, None]
    bias = tbl[g_ix, r_ix, rel_bucket(dist)[:, :, None]]
    s = jnp.einsum('bqgrd,bgqnsd->bgrqns', qg, ks).astype(jnp.float32) * SCALE + bias
    s = jnp.where((dist >= 0)[:, :, None], s, NEG_INF).reshape(B, N_KV, GROUP_REP, Tq, n_sel * SLC_BLOCK)
    p = jax.nn.softmax(s, axis=-1).reshape(B, N_KV, GROUP_REP, Tq, n_sel, SLC_BLOCK)
    return jnp.einsum('bgrqns,bgqnsd->bqgrd', p.astype(vs.dtype), vs)


def window_branch(qg, qpos, kw, vw, kpos, rel_bias):
    Tq, Tk = qg.shape[1], kw.shape[1]
    dist = qpos[:, None] - kpos[None, :]
    vis = (dist >= 0) & (dist < WINDOW) & (kpos[None, :] >= 0)
    bias = rel_bias[rel_bucket(dist)].transpose(2, 0, 1).reshape(N_KV, GROUP_REP, Tq, Tk)
    s = jnp.einsum('bqgrd,bkgd->bgrqk', qg, kw).astype(jnp.float32) * SCALE + bias
    p = jax.nn.softmax(jnp.where(vis, s, NEG_INF), axis=-1)
    return jnp.einsum('bgrqk,bkgd->bqgrd', p.astype(vw.dtype), vw)


def prompt_sparse_branches(qg, idx, kb, vb, kw, vw, rel_bias):
    B, T = qg.shape[:2]
    kw_pad = jnp.pad(kw, ((0, 0), (WINDOW, 0), (0, 0), (0, 0)))
    vw_pad = jnp.pad(vw, ((0, 0), (WINDOW, 0), (0, 0), (0, 0)))

    def one_block(jb):
        start = jb * Q_BLOCK
        qpos = start + jnp.arange(Q_BLOCK, dtype=jnp.int32)
        q_j = lax.dynamic_slice_in_dim(qg, start, Q_BLOCK, axis=1)
        idx_j = lax.dynamic_slice_in_dim(idx, start, Q_BLOCK, axis=2)
        o_slc = selected_branch(q_j, qpos, idx_j, kb, vb, rel_bias)
        kw_j = lax.dynamic_slice_in_dim(kw_pad, start, WINDOW + Q_BLOCK, axis=1)
        vw_j = lax.dynamic_slice_in_dim(vw_pad, start, WINDOW + Q_BLOCK, axis=1)
        kpos = start - WINDOW + jnp.arange(WINDOW + Q_BLOCK, dtype=jnp.int32)
        o_win = window_branch(q_j, qpos, kw_j, vw_j, kpos, rel_bias)
        return o_slc, o_win

    o_slc, o_win = lax.map(one_block, jnp.arange(T // Q_BLOCK, dtype=jnp.int32))
    o_slc = jnp.moveaxis(o_slc, 0, 1).reshape(B, T, N_KV, GROUP_REP, HEAD_DIM)
    o_win = jnp.moveaxis(o_win, 0, 1).reshape(B, T, N_KV, GROUP_REP, HEAD_DIM)
    return o_slc, o_win


def nsa_mixer(h, qpos, ctx, w_in, w_out, rel_bias, prompt_mode):
    B, T = h.shape[:2]
    proj = h @ w_in
    q = proj[..., :D_ATT].reshape(B, T, N_KV, GROUP_REP, HEAD_DIM)
    z = proj[..., D_ATT:(1 + N_BRANCH) * D_ATT].reshape(B, T, N_BRANCH, N_HEADS, HEAD_DIM)
    gates = jax.nn.sigmoid(proj[..., (1 + N_BRANCH) * D_ATT:].reshape(B, T, N_BRANCH, N_HEADS))
    o_cmp, p_cmp = compressed_branch(q, qpos, ctx['kc'], ctx['vc'], rel_bias)
    idx = selection_indices(p_cmp, qpos, ctx['kb'].shape[2])
    if prompt_mode:
        o_slc, o_win = prompt_sparse_branches(q, idx, ctx['kb'], ctx['vb'], ctx['kw'], ctx['vw'], rel_bias)
    else:
        o_slc = selected_branch(q, qpos, idx, ctx['kb'], ctx['vb'], rel_bias)
        o_win = window_branch(q, qpos, ctx['kw'], ctx['vw'], ctx['kw_pos'], rel_bias)
    o = jnp.stack([o_cmp, o_slc, o_win], axis=2).reshape(B, T, N_BRANCH, N_HEADS, HEAD_DIM)
    y = jnp.sum(gates[..., None] * o * jax.nn.silu(z), axis=2).reshape(B, T, D_ATT)
    return y @ w_out


def run_trunk(x, c, qpos, pool_prefix, build_ctx, prompt_mode, g_norm, w_ada, b_ada, w_in_a, w_grp, pool_scale, w_out_a, w_in_b, w_out_b, rel_bias, g_final):
    new_pool = []
    for l in range(DEPTH):
        shift, scale, gate = ada_modulate(c, w_ada[l], b_ada[l])
        if l == N_A:
            ctx, kv_state = build_ctx(x)
        h = rms_norm(x, g_norm[l]) * (1 + scale) + shift
        if l < N_A:
            y, pref = pool_mixer(h, pool_prefix[l], qpos, w_in_a[l], w_grp[l], pool_scale[l], w_out_a[l])
            new_pool.append(pref)
        else:
            y = nsa_mixer(h, qpos, ctx, w_in_b[l - N_A], w_out_b[l - N_A], rel_bias, prompt_mode)
        x = x + gate * y
    return rms_norm(x, g_final), jnp.stack(new_pool), kv_state


def setup_inputs(seed: int = 0) -> dict:
    key = jax.random.key(seed)
    ks = jax.random.split(key, 32)
    n_pages = PAST_LEN // PAGE_SIZE
    n_phys = (DEC_BATCH * n_pages * 5) // 4
    win_buf = min(WINDOW, PAST_LEN)

    def nrm(k, shape, scale=1.0):
        return jax.random.normal(k, shape, jnp.float32) * scale

    perm = jax.random.permutation(ks[0], n_phys)
    page_table = perm[:DEC_BATCH * n_pages].reshape(DEC_BATCH, n_pages).astype(jnp.int32)
    cache_shape = (n_phys, PAGE_SIZE, N_KV, HEAD_DIM)
    win_shape = (DEC_BATCH, win_buf, N_KV, HEAD_DIM)
    return {
        'x_prompt': nrm(ks[1], (BATCH, SEQ, D_MODEL)),
        'x_sample': nrm(ks[2], (DEC_BATCH, DEC_SEQ, D_MODEL)),
        'state_pool': nrm(ks[3], (N_A, DEC_BATCH, POOL_BUF, D_POOL)),
        'cache_k_cmp': nrm(ks[4], cache_shape),
        'cache_v_cmp': nrm(ks[5], cache_shape),
        'cache_k_slc': nrm(ks[6], cache_shape),
        'cache_v_slc': nrm(ks[7], cache_shape),
        'state_k_win': nrm(ks[8], win_shape),
        'state_v_win': nrm(ks[9], win_shape),
        'page_table': page_table,
        'c_prompt': nrm(ks[10], (BATCH, D_MODEL)),
        'c_sample': nrm(ks[11], (DEC_BATCH, D_MODEL)),
        'g_norm': 1.0 + nrm(ks[12], (DEPTH, D_MODEL), 0.02),
        'w_ada': nrm(ks[13], (DEPTH, D_MODEL, 3 * D_MODEL), 0.5 * D_MODEL ** -0.5),
        'b_ada': nrm(ks[14], (DEPTH, 3 * D_MODEL), 0.01),
        'w_in_a': nrm(ks[15], (N_A, D_MODEL, 2 * D_POOL), D_MODEL ** -0.5),
        'w_grp': nrm(ks[16], (N_A, N_POOL_GROUPS, POOL_GROUP, POOL_GROUP), POOL_GROUP ** -0.5),
        'pool_scale': 1.0 + nrm(ks[17], (N_A, D_POOL), 0.02),
        'w_out_a': nrm(ks[18], (N_A, D_POOL, D_MODEL), D_POOL ** -0.5),
        'g_kv': 1.0 + nrm(ks[19], (D_MODEL,), 0.02),
        'w_kv': nrm(ks[20], (D_MODEL, 2 * N_BRANCH * D_KV), D_MODEL ** -0.5),
        'pe_k': nrm(ks[21], (CMP_LEN, HEAD_DIM), 0.1),
        'w_ck1': nrm(ks[22], (CMP_LEN * HEAD_DIM, HEAD_DIM), (CMP_LEN * HEAD_DIM) ** -0.5),
        'w_ck2': nrm(ks[23], (HEAD_DIM, HEAD_DIM), 1.5 * HEAD_DIM ** -0.5),
        'pe_v': nrm(ks[24], (CMP_LEN, HEAD_DIM), 0.1),
        'w_cv1': nrm(ks[25], (CMP_LEN * HEAD_DIM, HEAD_DIM), (CMP_LEN * HEAD_DIM) ** -0.5),
        'w_cv2': nrm(ks[26], (HEAD_DIM, HEAD_DIM), 1.5 * HEAD_DIM ** -0.5),
        'rel_bias': nrm(ks[27], (N_BUCKETS, N_HEADS), 0.5),
        'w_in_b': nrm(ks[28], (N_B, D_MODEL, (1 + N_BRANCH) * D_ATT + N_BRANCH * N_HEADS), D_MODEL ** -0.5),
        'w_out_b': nrm(ks[29], (N_B, D_ATT, D_MODEL), D_ATT ** -0.5),
        'g_final': 1.0 + nrm(ks[30], (D_MODEL,), 0.02),
    }


def reference(x_prompt, x_sample, state_pool, cache_k_cmp, cache_v_cmp, cache_k_slc, cache_v_slc, state_k_win, state_v_win, page_table, c_prompt, c_sample, g_norm, w_ada, b_ada, w_in_a, w_grp, pool_scale, w_out_a, g_kv, w_kv, pe_k, w_ck1, w_ck2, pe_v, w_cv1, w_cv2, rel_bias, w_in_b, w_out_b, g_final):
    def prompt_ctx(xs):
        k_cmp, v_cmp, k_slc, v_slc, k_win, v_win = shared_kv(xs, g_kv, w_kv)
        ctx = {'kc': compress(k_cmp, pe_k, w_ck1, w_ck2), 'vc': compress(v_cmp, pe_v, w_cv1, w_cv2),
               'kb': to_blocks(k_slc), 'vb': to_blocks(v_slc), 'kw': k_win, 'vw': v_win}
        wb = min(WINDOW, xs.shape[1])
        return ctx, (k_cmp, v_cmp, k_slc, v_slc, k_win[:, -wb:], v_win[:, -wb:])

    def sample_ctx(xs):
        k_cmp, v_cmp, k_slc, v_slc, k_win, v_win = shared_kv(xs, g_kv, w_kv)
        k_cmp_all = jnp.concatenate([gather_pages(cache_k_cmp, page_table), k_cmp], axis=1)
        v_cmp_all = jnp.concatenate([gather_pages(cache_v_cmp, page_table), v_cmp], axis=1)
        k_slc_all = jnp.concatenate([gather_pages(cache_k_slc, page_table), k_slc], axis=1)
        v_slc_all = jnp.concatenate([gather_pages(cache_v_slc, page_table), v_slc], axis=1)
        kw = jnp.concatenate([state_k_win, k_win], axis=1)
        vw = jnp.concatenate([state_v_win, v_win], axis=1)
        wb = state_k_win.shape[1]
        kw_pos = PAST_LEN - wb + jnp.arange(wb + xs.shape[1], dtype=jnp.int32)
        ctx = {'kc': compress(k_cmp_all, pe_k, w_ck1, w_ck2), 'vc': compress(v_cmp_all, pe_v, w_cv1, w_cv2),
               'kb': to_blocks(k_slc_all), 'vb': to_blocks(v_slc_all), 'kw': kw, 'vw': vw, 'kw_pos': kw_pos}
        return ctx, (k_cmp, v_cmp, k_slc, v_slc, kw[:, -wb:], vw[:, -wb:])

    qpos_p = jnp.arange(x_prompt.shape[1], dtype=jnp.int32)
    pool_zero = jnp.zeros((N_A, x_prompt.shape[0], POOL_BUF, D_POOL), x_prompt.dtype)
    y_prompt, pool_p, kv_p = run_trunk(x_prompt, c_prompt, qpos_p, pool_zero, prompt_ctx, True, g_norm, w_ada, b_ada, w_in_a, w_grp, pool_scale, w_out_a, w_in_b, w_out_b, rel_bias, g_final)
    k_cmp_p, v_cmp_p, k_slc_p, v_slc_p, k_win_p, v_win_p = kv_p

    qpos_s = PAST_LEN + jnp.arange(x_sample.shape[1], dtype=jnp.int32)
    y_sample, pool_s, kv_s = run_trunk(x_sample, c_sample, qpos_s, state_pool, sample_ctx, False, g_norm, w_ada, b_ada, w_in_a, w_grp, pool_scale, w_out_a, w_in_b, w_out_b, rel_bias, g_final)
    k_cmp_s, v_cmp_s, k_slc_s, v_slc_s, k_win_s, v_win_s = kv_s

    return (y_prompt, y_sample, pool_p, k_cmp_p, v_cmp_p, k_slc_p, v_slc_p, k_win_p, v_win_p, pool_s, k_cmp_s, v_cmp_s, k_slc_s, v_slc_s, k_win_s, v_win_s)
```

```python
import functools
import math

import numpy as np
import jax
import jax.numpy as jnp
from jax import lax
from jax.experimental import pallas as pl
from jax.experimental.pallas import tpu as pltpu

F32 = jnp.float32
BF16 = jnp.bfloat16

PAGE_SIZE = 128
POOL_WINDOWS = (2, 4, 8, 16)
POOL_BUF = max(POOL_WINDOWS) - 1
HEAD_DIM = 128
N_KV = 4
GROUP_REP = 4
N_HEADS = N_KV * GROUP_REP
N_BRANCH = 3
CMP_LEN = 32
CMP_STRIDE = 16
SLC_BLOCK = 64
N_SEL = 8
WINDOW = 512
N_BUCKETS = 32
MAX_DISTANCE = 128
RMS_EPS = 1e-6
SCALE = HEAD_DIM ** -0.5
NEG_INF = -1e30
FORCE_SCORE = 1e9

LANES = 128
SUBLANES = 8
VMEM_LIMIT_BYTES = 56 * 1024 * 1024
TQ = 256
HALO = 16


def _bucket_starts():
    max_exact = N_BUCKETS // 2
    d = np.arange(0, MAX_DISTANCE + 1)
    large = max_exact + np.floor(
        np.log(np.maximum(d, max_exact) / max_exact) / math.log(MAX_DISTANCE / max_exact) * (N_BUCKETS - max_exact)
    ).astype(np.int64)
    bucket = np.where(d < max_exact, d, np.minimum(large, N_BUCKETS - 1))
    return [int(np.argmax(bucket >= k)) for k in range(N_BUCKETS)]


BUCKET_STARTS = _bucket_starts()
FAR_DIST = BUCKET_STARTS[-1]


def _params(*sem):
    return pltpu.CompilerParams(dimension_semantics=sem, vmem_limit_bytes=VMEM_LIMIT_BYTES)


def _silu(x):
    return x * jax.nn.sigmoid(x)


def _rms(x, g):
    return x * lax.rsqrt(jnp.mean(x * x, axis=-1, keepdims=True) + RMS_EPS) * g


def _dot(a, b):
    return jnp.dot(a, b, preferred_element_type=F32)


def _dot_nt(a, b):
    return lax.dot_general(a, b, (((1,), (1,)), ((), ())), preferred_element_type=F32)


def _ada_kernel(c_ref, w_ref, b_ref, o_ref):
    a = _silu(c_ref[...]).astype(BF16)
    o_ref[0] = _dot(a, w_ref[0].astype(BF16)) + b_ref[0]


def _ada(c_all, w_ada, b_ada):
    depth, d, n = w_ada.shape
    m = c_all.shape[0]
    tn = 512
    return pl.pallas_call(
        _ada_kernel,
        grid=(depth, n // tn),
        in_specs=[
            pl.BlockSpec((m, d), lambda l, j: (0, 0)),
            pl.BlockSpec((1, d, tn), lambda l, j: (l, 0, j)),
            pl.BlockSpec((1, 1, tn), lambda l, j: (l, 0, j)),
        ],
        out_specs=pl.BlockSpec((1, m, tn), lambda l, j: (l, 0, j)),
        out_shape=jax.ShapeDtypeStruct((depth, m, n), F32),
        compiler_params=_params("parallel", "parallel"),
        name="ada_modulation",
    )(c_all, w_ada, b_ada.reshape(depth, 1, n))


def _nm_matmul_kernel(x_ref, g_ref, sc_ref, sh_ref, w_ref, o_ref, h_scr, *, sigmoid):
    @pl.when(pl.program_id(2) == 0)
    def _():
        y = _rms(x_ref[0], g_ref[...])
        h_scr[...] = (y * (1.0 + sc_ref[0]) + sh_ref[0]).astype(BF16)

    r = _dot(h_scr[...], w_ref[...])
    o_ref[0] = jax.nn.sigmoid(r) if sigmoid else r


def _mod_spec(arr, tm):
    d = arr.shape[-1]
    if arr.shape[1] == 1:
        return pl.BlockSpec((1, 1, d), lambda b, i, *_: (b, 0, 0))
    return pl.BlockSpec((1, tm, d), lambda b, i, *_: (b, i, 0))


def _nm_matmul(x, g, scale, shift, w, *, n_cols, tm, tn, sigmoid=False, name):
    bsz, rows, d = x.shape
    return pl.pallas_call(
        functools.partial(_nm_matmul_kernel, sigmoid=sigmoid),
        grid=(bsz, rows // tm, n_cols // tn),
        in_specs=[
            pl.BlockSpec((1, tm, d), lambda b, i, j: (b, i, 0)),
            pl.BlockSpec((1, d), lambda b, i, j: (0, 0)),
            _mod_spec(scale, tm),
            _mod_spec(shift, tm),
            pl.BlockSpec((d, tn), lambda b, i, j: (0, j)),
        ],
        out_specs=pl.BlockSpec((1, tm, tn), lambda b, i, j: (b, i, j)),
        out_shape=jax.ShapeDtypeStruct((bsz, rows, n_cols), F32),
        scratch_shapes=[pltpu.VMEM((tm, d), BF16)],
        compiler_params=_params("parallel", "parallel", "arbitrary"),
        name=name,
    )(x, g.reshape(1, d), scale, shift, w)


def _pool_mix(pooled_fn, z, wg_ref, ps_ref, y_ref):
    grp = z.shape[-1] // len(POOL_WINDOWS)
    for gi, w in enumerate(POOL_WINDOWS):
        cs = slice(gi * grp, (gi + 1) * grp)
        mixed = _dot(pooled_fn(gi, w, cs).astype(BF16), wg_ref[gi])
        y_ref[0, :, cs] = mixed * ps_ref[:, cs] * _silu(z[:, cs])


def _pool_prompt_kernel(u_ref, z_ref, halo_ref, wg_ref, ps_ref, y_ref, ue_scr):
    i = pl.program_id(1)
    tm = u_ref.shape[1]
    u = u_ref[0]
    ue_scr[0:HALO, :] = jnp.where(i > 0, halo_ref[0], 0.0)
    ue_scr[HALO:HALO + tm, :] = u
    pos = i * tm + lax.broadcasted_iota(jnp.int32, (tm, 1), 0)

    def pooled(gi, w, cs):
        acc = u[:, cs]
        for k in range(1, w):
            acc = acc + ue_scr[HALO - k:HALO - k + tm, cs]
        cnt = jnp.minimum(pos + 1, w).astype(F32)
        return acc / cnt - u[:, cs]

    _pool_mix(pooled, z_ref[0], wg_ref, ps_ref, y_ref)


def _pool_prompt(uz, w_grp, pool_scale, *, tm):
    bsz, t, d2 = uz.shape
    d = d2 // 2
    ng, grp, _ = w_grp.shape
    return pl.pallas_call(
        _pool_prompt_kernel,
        grid=(bsz, t // tm),
        in_specs=[
            pl.BlockSpec((1, tm, d), lambda b, i: (b, i, 0)),
            pl.BlockSpec((1, tm, d), lambda b, i: (b, i, 1)),
            pl.BlockSpec((1, HALO, d), lambda b, i: (b, jnp.maximum(i * (tm // HALO) - 1, 0), 0)),
            pl.BlockSpec((ng, grp, grp), lambda b, i: (0, 0, 0)),
            pl.BlockSpec((1, d), lambda b, i: (0, 0)),
        ],
        out_specs=pl.BlockSpec((1, tm, d), lambda b, i: (b, i, 0)),
        out_shape=jax.ShapeDtypeStruct((bsz, t, d), F32),
        scratch_shapes=[pltpu.VMEM((HALO + tm, d), F32)],
        compiler_params=_params("parallel", "parallel"),
        name="pool_mixer_prompt",
    )(uz, uz, uz, w_grp, pool_scale.reshape(1, d))


def _pool_sample_kernel(u_ref, z_ref, pre_ref, wg_ref, ps_ref, y_ref, *, qpos):
    u = u_ref[0]

    def pooled(gi, w, cs):
        acc = u[:, cs]
        for k in range(1, w):
            acc = acc + pre_ref[:, POOL_BUF - k, cs]
        return acc / float(min(qpos + 1, w)) - u[:, cs]

    _pool_mix(pooled, z_ref[0], wg_ref, ps_ref, y_ref)


def _pool_sample(uz, prefix, w_grp, pool_scale, *, qpos, tb):
    _, nb, d2 = uz.shape
    d = d2 // 2
    ng, grp, _ = w_grp.shape
    return pl.pallas_call(
        functools.partial(_pool_sample_kernel, qpos=qpos),
        grid=(nb // tb,),
        in_specs=[
            pl.BlockSpec((1, tb, d), lambda i: (0, i, 0)),
            pl.BlockSpec((1, tb, d), lambda i: (0, i, 1)),
            pl.BlockSpec((tb, POOL_BUF, d), lambda i: (i, 0, 0)),
            pl.BlockSpec((ng, grp, grp), lambda i: (0, 0, 0)),
            pl.BlockSpec((1, d), lambda i: (0, 0)),
        ],
        out_specs=pl.BlockSpec((1, tb, d), lambda i: (0, i, 0)),
        out_shape=jax.ShapeDtypeStruct((1, nb, d), F32),
        compiler_params=_params("parallel"),
        name="pool_mixer_sample",
    )(uz, uz, prefix, w_grp, pool_scale.reshape(1, d))


def _proj_residual_kernel(y_ref, w_ref, x_ref, gate_ref, g_ref, o_ref, *, final_norm):
    xo = x_ref[0] + gate_ref[0] * _dot(y_ref[0].astype(BF16), w_ref[...])
    o_ref[0] = _rms(xo, g_ref[...]) if final_norm else xo


def _proj_residual(y, w, x, gate, g_final, *, tm, final_norm, name):
    bsz, rows, d = x.shape
    dy = y.shape[-1]
    return pl.pallas_call(
        functools.partial(_proj_residual_kernel, final_norm=final_norm),
        grid=(bsz, rows // tm),
        in_specs=[
            pl.BlockSpec((1, tm, dy), lambda b, i: (b, i, 0)),
            pl.BlockSpec((dy, d), lambda b, i: (0, 0)),
            pl.BlockSpec((1, tm, d), lambda b, i: (b, i, 0)),
            _mod_spec(gate, tm),
            pl.BlockSpec((1, d), lambda b, i: (0, 0)),
        ],
        out_specs=pl.BlockSpec((1, tm, d), lambda b, i: (b, i, 0)),
        out_shape=jax.ShapeDtypeStruct((bsz, rows, d), F32),
        compiler_params=_params("parallel", "parallel"),
        name=name,
    )(y, w, x, gate, g_final.reshape(1, d))


def _kv_kernel(x_ref, g_ref, w_ref, *o_refs):
    h = _rms(x_ref[0], g_ref[...]).astype(BF16)
    n = o_refs[0].shape[-1]
    for o, o_ref in enumerate(o_refs):
        o_ref[0] = _dot(h, w_ref[:, o * n:(o + 1) * n])


def _kv_proj(x, g_kv, w_kv, *, tm, name):
    bsz, rows, d = x.shape
    n_out = 2 * N_BRANCH
    n = w_kv.shape[1] // n_out
    return pl.pallas_call(
        _kv_kernel,
        grid=(bsz, rows // tm),
        in_specs=[
            pl.BlockSpec((1, tm, d), lambda b, i: (b, i, 0)),
            pl.BlockSpec((1, d), lambda b, i: (0, 0)),
            pl.BlockSpec((d, n_out * n), lambda b, i: (0, 0)),
        ],
        out_specs=[pl.BlockSpec((1, tm, n), lambda b, i: (b, i, 0))] * n_out,
        out_shape=[jax.ShapeDtypeStruct((bsz, rows, n), F32)] * n_out,
        compiler_params=_params("parallel", "parallel"),
        name=name,
    )(x, g_kv.reshape(1, d), w_kv)


def _bias_lookup(rb_ref, h, dist):
    acc = jnp.full(dist.shape, rb_ref[N_BUCKETS - 1, h], F32)
    for k in range(N_BUCKETS - 2, -1, -1):
        acc = jnp.where(dist < BUCKET_STARTS[k + 1], rb_ref[k, h], acc)
    return acc


def _bias_tables_kernel(rb_ref, t0_ref, t1_ref, tc_ref, ts_ref, tw_ref, tcs_ref, *, past):
    h = pl.program_id(0)
    key = lax.broadcasted_iota(jnp.int32, (TQ, TQ), 0)
    qry = lax.broadcasted_iota(jnp.int32, (TQ, TQ), 1)
    t0_ref[0] = _bias_lookup(rb_ref, h, qry - key)
    t1_ref[0] = _bias_lookup(rb_ref, h, TQ + qry - key)
    tc_ref[0] = _bias_lookup(rb_ref, h, qry - CMP_STRIDE * (key - TQ // 2) - (CMP_LEN - 1))
    ls = lax.broadcasted_iota(jnp.int32, ts_ref.shape[1:], 1)
    ts_ref[0] = jnp.where(ls <= past, _bias_lookup(rb_ref, h, past - ls), NEG_INF)
    lw = lax.broadcasted_iota(jnp.int32, tw_ref.shape[1:], 1)
    dw = WINDOW - lw
    tw_ref[0] = jnp.where((dw >= 0) & (dw < WINDOW) & (past - dw >= 0), _bias_lookup(rb_ref, h, dw), NEG_INF)
    lc = lax.broadcasted_iota(jnp.int32, tcs_ref.shape[1:], 1)
    dc = past - CMP_STRIDE * lc - (CMP_LEN - 1)
    tcs_ref[0] = jnp.where(dc >= 0, _bias_lookup(rb_ref, h, dc), NEG_INF)


def _bias_tables(rel_bias, *, past, n_cmp_s):
    nh = rel_bias.shape[1]
    ls = past + LANES
    lw = WINDOW + LANES
    shapes = [(nh, TQ, TQ)] * 3 + [(nh, 1, ls), (nh, 1, lw), (nh, 1, n_cmp_s)]
    return pl.pallas_call(
        functools.partial(_bias_tables_kernel, past=past),
        grid=(nh,),
        in_specs=[pl.BlockSpec(memory_space=pltpu.SMEM)],
        out_specs=[pl.BlockSpec((1,) + s[1:], lambda h: (h, 0, 0)) for s in shapes],
        out_shape=[jax.ShapeDtypeStruct(s, F32) for s in shapes],
        compiler_params=_params("parallel"),
        name="rel_bias_tables",
    )(rel_bias)


def _compress_one(page_refs, new_ref, wcat_ref, w2_ref, pe_ref, x_scr, s_scr, *, has_new):
    n_pages = len(page_refs)
    sub_per_page = PAGE_SIZE // CMP_STRIDE
    n_sub = n_pages * sub_per_page
    for p, pref in enumerate(page_refs):
        for c in range(CMP_STRIDE):
            for g in range(N_KV):
                x_scr[g * n_sub + p * sub_per_page:g * n_sub + (p + 1) * sub_per_page,
                      c * HEAD_DIM:(c + 1) * HEAD_DIM] = pref[0, pl.ds(N_KV * c + g, sub_per_page,
                                                                       stride=N_KV * CMP_STRIDE), :]
    wcat = wcat_ref[...]
    s_scr[0:N_KV * n_sub, :] = _dot(x_scr[...].astype(BF16), wcat)
    s_scr[N_KV * n_sub:, :] = jnp.zeros((SUBLANES, 2 * HEAD_DIM), F32)
    pe_r = _dot(pe_ref[...].astype(BF16), wcat)
    pe_const = pe_r[0:1, :HEAD_DIM] + pe_r[1:2, HEAD_DIM:]
    row = lax.broadcasted_iota(jnp.int32, (n_sub, 1), 0)
    outs = []
    for g in range(N_KV):
        top = s_scr[g * n_sub:(g + 1) * n_sub, :HEAD_DIM]
        bot = s_scr[g * n_sub + 1:(g + 1) * n_sub + 1, HEAD_DIM:]
        if has_new:
            new8 = jnp.broadcast_to(new_ref[0, :, g * HEAD_DIM:(g + 1) * HEAD_DIM], (SUBLANES, HEAD_DIM))
            new_term = _dot(new8.astype(BF16), wcat[0:HEAD_DIM, HEAD_DIM:])[0:1]
            bot = jnp.where(row == n_sub - 1, new_term, bot)
        pre = top + bot + pe_const
        out = _dot(_silu(pre).astype(BF16), w2_ref[...])
        if not has_new:
            out = jnp.where(row == n_sub - 1, 0.0, out)
        outs.append(out)
    return outs


def _compress_prompt_kernel(pt_ref, *refs, n_pages):
    kp, vp = refs[:n_pages], refs[n_pages:2 * n_pages]
    wck, w2k, pek, wcv, w2v, pev, kc_ref, vc_ref, x_scr, s_scr = refs[2 * n_pages:]
    n_sub = n_pages * (PAGE_SIZE // CMP_STRIDE)
    for pages, wc, w2, pe, o_ref in ((kp, wck, w2k, pek, kc_ref), (vp, wcv, w2v, pev, vc_ref)):
        outs = _compress_one(pages, None, wc, w2, pe, x_scr, s_scr, has_new=False)
        for g in range(N_KV):
            o_ref[0, g * n_sub:(g + 1) * n_sub, :] = outs[g]


def _page_specs(n_pages):
    return [pl.BlockSpec((1, PAGE_SIZE * N_KV, HEAD_DIM), lambda b, pt, p=p: (pt[b, p], 0, 0)) for p in range(n_pages)]


def _const_spec(shape):
    return pl.BlockSpec(shape, lambda b, pt: (0,) * len(shape))


def _compress_prompt(k_rows, v_rows, cw):
    bsz, t, width = k_rows.shape
    n_pages = t // PAGE_SIZE
    n_sub = t // CMP_STRIDE
    pt = jnp.arange(bsz * n_pages, dtype=jnp.int32).reshape(bsz, n_pages)
    kp = k_rows.reshape(bsz * n_pages, PAGE_SIZE * N_KV, HEAD_DIM)
    vp = v_rows.reshape(bsz * n_pages, PAGE_SIZE * N_KV, HEAD_DIM)
    wspecs = [_const_spec(a.shape) for a in cw]
    grid_spec = pltpu.PrefetchScalarGridSpec(
        num_scalar_prefetch=1,
        grid=(bsz,),
        in_specs=_page_specs(n_pages) * 2 + wspecs,
        out_specs=[pl.BlockSpec((1, N_KV * n_sub, HEAD_DIM), lambda b, pt: (b, 0, 0))] * 2,
        scratch_shapes=[pltpu.VMEM((N_KV * n_sub, CMP_STRIDE * HEAD_DIM), F32),
                        pltpu.VMEM((N_KV * n_sub + SUBLANES, 2 * HEAD_DIM), F32)],
    )
    return pl.pallas_call(
        functools.partial(_compress_prompt_kernel, n_pages=n_pages),
        grid_spec=grid_spec,
        out_shape=[jax.ShapeDtypeStruct((bsz, N_KV * n_sub, HEAD_DIM), F32)] * 2,
        compiler_params=_params("parallel"),
        name="compress_prompt",
    )(pt, *([kp] * n_pages), *([vp] * n_pages), *cw)


def _query_rows(q_ref, g):
    rows = [q_ref[0, :, (GROUP_REP * g + r) * HEAD_DIM:(GROUP_REP * g + r + 1) * HEAD_DIM] for r in range(GROUP_REP)]
    rows.append(jnp.zeros((SUBLANES - GROUP_REP, HEAD_DIM), F32))
    return (jnp.concatenate(rows, axis=0) * SCALE).astype(BF16)


def _split_dot(a, b):
    hi = a.astype(BF16)
    lo = (a - hi.astype(F32)).astype(BF16)
    return _dot(hi, b) + _dot(lo, b)


def _compress_sample_kernel(pt_ref, *refs, n_pages, cur_blk):
    kp, vp = refs[:n_pages], refs[n_pages:2 * n_pages]
    (newk, newv, wck, w2k, pek, wcv, w2v, pev, q_ref, tcs_ref, ovl_ref, pow_ref,
     oc_ref, bits_ref, x_scr, s_scr) = refs[2 * n_pages:]
    kcs = _compress_one(kp, newk, wck, w2k, pek, x_scr, s_scr, has_new=True)
    vcs = _compress_one(vp, newv, wcv, w2v, pev, x_scr, s_scr, has_new=True)
    lane = lax.broadcasted_iota(jnp.int32, (1, LANES), 1)
    kidx = lax.broadcasted_iota(jnp.int32, (LANES, LANES), 0)
    jidx = lax.broadcasted_iota(jnp.int32, (LANES, LANES), 1)
    for g in range(N_KV):
        s = _dot_nt(_query_rows(q_ref, g), kcs[g].astype(BF16)) + tcs_ref[g]
        e = jnp.exp(s - jnp.max(s, axis=-1, keepdims=True))
        p = e / jnp.sum(e, axis=-1, keepdims=True)
        oc_ref[0, g] = _dot(p.astype(BF16), vcs[g].astype(BF16))
        psum = jnp.sum(p[0:GROUP_REP], axis=0, keepdims=True)
        imp = _split_dot(jnp.broadcast_to(psum, (SUBLANES, LANES)), ovl_ref[...])[0:1]
        forced = (lane == cur_blk) | (lane == 0)
        imp = jnp.where(forced, FORCE_SCORE, jnp.where(lane <= cur_blk, imp, -FORCE_SCORE))
        a = jnp.broadcast_to(imp, (LANES, LANES))
        b = a.T
        beats = jnp.where(b > a, 1.0, jnp.where((b == a) & (kidx < jidx), 1.0, 0.0))
        rank = jnp.sum(beats, axis=0, keepdims=True)
        sel = jnp.where(rank < N_SEL, 1.0, 0.0) * pow_ref[...]
        lo = jnp.sum(jnp.where(lane < 16, sel, 0.0), axis=-1, keepdims=True).astype(jnp.int32)
        hi = jnp.sum(jnp.where((lane >= 16) & (lane < 32), sel, 0.0), axis=-1, keepdims=True).astype(jnp.int32)
        bits_ref[0, g:g + 1, :] = jnp.broadcast_to(lo | (hi << 16), (1, LANES))


def _compress_sample(cache_k, cache_v, page_table, new_k, new_v, cw, q, tcs, ovl, *, past):
    nb, n_pages = page_table.shape
    width = new_k.shape[-1]
    n_sub = n_pages * (PAGE_SIZE // CMP_STRIDE)
    pow2 = np.zeros((1, LANES), np.float32)
    pow2[0, :32] = 2.0 ** (np.arange(32) % 16)
    row_spec = pl.BlockSpec((1, 1, width), lambda b, pt: (b, 0, 0))
    grid_spec = pltpu.PrefetchScalarGridSpec(
        num_scalar_prefetch=1,
        grid=(nb,),
        in_specs=(_page_specs(n_pages) * 2 + [row_spec, row_spec] + [_const_spec(a.shape) for a in cw]
                  + [pl.BlockSpec((1, 1, q.shape[-1]), lambda b, pt: (b, 0, 0)),
                     _const_spec(tcs.shape), _const_spec(ovl.shape), _const_spec(pow2.shape)]),
        out_specs=[pl.BlockSpec((1, N_KV, SUBLANES, HEAD_DIM), lambda b, pt: (b, 0, 0, 0)),
                   pl.BlockSpec((1, SUBLANES, LANES), lambda b, pt: (b, 0, 0))],
        scratch_shapes=[pltpu.VMEM((N_KV * n_sub, CMP_STRIDE * HEAD_DIM), F32),
                        pltpu.VMEM((N_KV * n_sub + SUBLANES, 2 * HEAD_DIM), F32)],
    )
    return pl.pallas_call(
        functools.partial(_compress_sample_kernel, n_pages=n_pages, cur_blk=past // SLC_BLOCK),
        grid_spec=grid_spec,
        out_shape=[jax.ShapeDtypeStruct((nb, N_KV, SUBLANES, HEAD_DIM), F32),
                   jax.ShapeDtypeStruct((nb, SUBLANES, LANES), jnp.int32)],
        compiler_params=_params("parallel"),
        name="compress_select_sample",
    )(page_table, *([cache_k] * n_pages), *([cache_v] * n_pages), new_k, new_v, *cw, q, tcs, ovl, jnp.asarray(pow2))


def _attn_prompt_kernel(rb_ref, q_ref, z0_ref, z1_ref, z2_ref, gt_ref, kc_ref, vc_ref, ks_ref, vs_ref, kw_ref, vw_ref,
                        t0_ref, t1_ref, tc_ref, ovl_ref, y_ref, m_scr, l_scr, acc_scr, sel_scr):
    g = pl.program_id(1)
    qt = pl.program_id(2)
    q = q_ref[...]
    qf = jnp.concatenate([q[:, r * HEAD_DIM:(r + 1) * HEAD_DIM] for r in range(GROUP_REP)], axis=0)
    qf = (qf * SCALE).astype(BF16)
    lane_q = lax.broadcasted_iota(jnp.int32, (1, TQ), 1)
    qpos = qt * TQ + lane_q
    key_l = lax.broadcasted_iota(jnp.int32, (TQ, 1), 0)
    gate_t = gt_ref[...].T
    z_refs = (z0_ref, z1_ref, z2_ref)

    def emit(br, r, o_t, first):
        cs = slice(r * HEAD_DIM, (r + 1) * HEAD_DIM)
        term = o_t.T * _silu(z_refs[br][:, cs])
        if first:
            y_ref[:, cs] = term
        else:
            y_ref[:, cs] += term

    n_cmp = kc_ref.shape[1]
    s_all = _dot_nt(kc_ref[0].astype(BF16), qf)
    vc_t = vc_ref[0].T.astype(BF16)
    cmp_end = lax.broadcasted_iota(jnp.int32, (n_cmp, 1), 0) * CMP_STRIDE + (CMP_LEN - 1)
    vis = qpos >= cmp_end
    any_vis = qpos >= CMP_LEN - 1
    row0 = pl.multiple_of(TQ // 2 - (TQ // CMP_STRIDE) * qt, CMP_STRIDE)
    psum = jnp.zeros((n_cmp, TQ), F32)
    for r in range(GROUP_REP):
        s = s_all[:, r * TQ:(r + 1) * TQ] + tc_ref[r, pl.ds(row0, n_cmp), :]
        s = jnp.where(vis, s, NEG_INF)
        e = jnp.exp(s - jnp.max(s, axis=0, keepdims=True))
        p = jnp.where(any_vis, e / jnp.sum(e, axis=0, keepdims=True), 0.0)
        psum = psum + p
        emit(0, r, _dot(vc_t, p.astype(BF16)) * gate_t[r:r + 1, :], True)

    n_slc = ovl_ref.shape[0]
    psum_hi = psum.astype(BF16)
    imp = _dot(ovl_ref[...], psum_hi) + _dot(ovl_ref[...], (psum - psum_hi.astype(F32)).astype(BF16))
    blk = lax.broadcasted_iota(jnp.int32, (n_slc, 1), 0)
    cur = lax.shift_right_logical(qpos, int(math.log2(SLC_BLOCK)))
    forced = (blk == cur) | (blk == 0)
    imp = jnp.where(forced, FORCE_SCORE, jnp.where(blk <= cur, imp, -FORCE_SCORE))
    rank = jnp.zeros((n_slc, TQ), F32)
    for k in range(n_slc):
        rk = imp[k:k + 1, :]
        rank = rank + jnp.where(rk > imp, 1.0, jnp.where((rk == imp) & (blk > k), 1.0, 0.0))
    sel = jnp.where(rank < min(N_SEL, n_slc), 1.0, 0.0)
    blk_per_tile = TQ // SLC_BLOCK
    for t in range(n_slc // blk_per_tile):
        sel_scr[t, 0:blk_per_tile, :] = sel[t * blk_per_tile:(t + 1) * blk_per_tile, :]

    def attend(st, k_ref, v_ref, kt, kind, use_sel, first):
        start = pl.multiple_of(kt * TQ, TQ)
        s_all = _dot_nt(k_ref[pl.ds(start, TQ), :].astype(BF16), qf)
        v_t = v_ref[pl.ds(start, TQ), :].T.astype(BF16)
        mask = None
        if kind == "diag":
            mask = key_l <= lane_q
        elif kind == "winfar":
            mask = key_l > lane_q
        if use_sel:
            sel4 = sel_scr[kt, 0:blk_per_tile, :]
            selm = jnp.concatenate(
                [jnp.broadcast_to(sel4[j:j + 1, :], (SLC_BLOCK, TQ)) for j in range(blk_per_tile)], axis=0) > 0.5
            mask = selm if mask is None else mask & selm
        for r in range(GROUP_REP):
            sl = slice(r * TQ, (r + 1) * TQ)
            s = s_all[:, sl]
            if kind == "diag":
                s = s + t0_ref[r]
            elif kind == "near":
                s = s + t1_ref[r]
            else:
                s = s + rb_ref[N_BUCKETS - 1, GROUP_REP * g + r]
            if mask is not None:
                s = jnp.where(mask, s, NEG_INF)
            mx = jnp.max(s, axis=0, keepdims=True)
            if first:
                m_new = mx
                p = jnp.exp(s - m_new)
                l_scr[st, :, sl] = jnp.sum(p, axis=0, keepdims=True)
                acc_scr[st, :, sl] = _dot(v_t, p.astype(BF16))
            else:
                m_old = m_scr[st, :, sl]
                m_new = jnp.maximum(m_old, mx)
                alpha = jnp.exp(m_old - m_new)
                p = jnp.exp(s - m_new)
                l_scr[st, :, sl] = alpha * l_scr[st, :, sl] + jnp.sum(p, axis=0, keepdims=True)
                acc_scr[st, :, sl] = alpha * acc_scr[st, :, sl] + _dot(v_t, p.astype(BF16))
            m_scr[st, :, sl] = m_new

    attend(0, ks_ref, vs_ref, qt, "diag", True, True)
    attend(1, kw_ref, vw_ref, qt, "diag", False, True)

    @pl.when(qt >= 1)
    def _():
        attend(0, ks_ref, vs_ref, qt - 1, "near", True, False)
        attend(1, kw_ref, vw_ref, qt - 1, "near", False, False)

    @pl.when(qt >= 2)
    def _():
        attend(1, kw_ref, vw_ref, qt - 2, "winfar", False, False)

        def far(kt, carry):
            attend(0, ks_ref, vs_ref, kt, "far", True, False)
            return carry

        lax.fori_loop(0, qt - 1, far, 0)

    for st in range(2):
        br = st + 1
        for r in range(GROUP_REP):
            sl = slice(r * TQ, (r + 1) * TQ)
            w = gate_t[br * GROUP_REP + r:br * GROUP_REP + r + 1, :] / l_scr[st, :, sl]
            emit(br, r, acc_scr[st, :, sl] * w, False)


def _attn_prompt(proj, gates, kc, vc, k_slc, v_slc, k_win, v_win, t0, t1, tc, ovl_t, rel_bias, *, bsz, t):
    nq = t // TQ
    gw = GROUP_REP * HEAD_DIM
    assert TQ // 2 - (TQ // CMP_STRIDE) * (nq - 1) >= 0 and kc.shape[1] == N_KV * (t // CMP_STRIDE)
    n_cmp = t // CMP_STRIDE
    d_att = N_HEADS * HEAD_DIM
    zoff = d_att // gw
    row = lambda b, g, i: b * nq + i
    in_specs = [
        pl.BlockSpec(memory_space=pltpu.SMEM),
        pl.BlockSpec((TQ, gw), lambda b, g, i: (row(b, g, i), g)),
    ] + [
        pl.BlockSpec((TQ, gw), lambda b, g, i, br=br: (row(b, g, i), zoff * (1 + br) + g)) for br in range(N_BRANCH)
    ] + [
        pl.BlockSpec((TQ, LANES), lambda b, g, i: (row(b, g, i), g)),
        pl.BlockSpec((1, n_cmp, HEAD_DIM), lambda b, g, i: (b, g, 0)),
        pl.BlockSpec((1, n_cmp, HEAD_DIM), lambda b, g, i: (b, g, 0)),
    ] + [pl.BlockSpec((t, HEAD_DIM), lambda b, g, i: (b, g))] * 4 + [
        pl.BlockSpec((GROUP_REP, TQ, TQ), lambda b, g, i: (g, 0, 0))] * 3 + [
        pl.BlockSpec(ovl_t.shape, lambda b, g, i: (0, 0)),
    ]
    return pl.pallas_call(
        _attn_prompt_kernel,
        grid=(bsz, N_KV, nq),
        in_specs=in_specs,
        out_specs=pl.BlockSpec((TQ, gw), lambda b, g, i: (row(b, g, i), g)),
        out_shape=jax.ShapeDtypeStruct((bsz * t, d_att), F32),
        scratch_shapes=[
            pltpu.VMEM((2, 1, GROUP_REP * TQ), F32),
            pltpu.VMEM((2, 1, GROUP_REP * TQ), F32),
            pltpu.VMEM((2, HEAD_DIM, GROUP_REP * TQ), F32),
            pltpu.VMEM((t // TQ, SUBLANES, TQ), F32),
        ],
        compiler_params=_params("parallel", "parallel", "arbitrary"),
        name="nsa_attention_prompt",
    )(rel_bias, proj, proj, proj, proj, gates, kc, vc, k_slc, v_slc, k_win, v_win, t0, t1, tc, ovl_t)


def _attn_sample_kernel(bits_ref, pt_ref, *refs, n_pages):
    kp, vp = refs[:n_pages], refs[n_pages:2 * n_pages]
    (kwin_ref, vwin_ref, nks_ref, nvs_ref, nkw_ref, nvw_ref, q_ref, z_ref, gt_ref, oc_ref, ts_ref, tw_ref,
     y_ref) = refs[2 * n_pages:]
    b = pl.program_id(0)
    lane = lax.broadcasted_iota(jnp.int32, (1, LANES), 1)
    blk_per_page = PAGE_SIZE // SLC_BLOCK
    d_att = N_HEADS * HEAD_DIM
    n_win = kwin_ref.shape[1]

    def softmax_pv(s_tiles, v_tiles, s_new, v_new):
        m = s_new
        for s in s_tiles:
            m = jnp.maximum(m, jnp.max(s, axis=-1, keepdims=True))
        p_new = jnp.exp(s_new - m)
        l = p_new
        acc = p_new * v_new
        for s, v in zip(s_tiles, v_tiles):
            p = jnp.exp(s - m)
            l = l + jnp.sum(p, axis=-1, keepdims=True)
            acc = acc + _dot(p.astype(BF16), v.astype(BF16))
        return acc / l

    for g in range(N_KV):
        gs = slice(g * HEAD_DIM, (g + 1) * HEAD_DIM)
        qg = _query_rows(q_ref, g)
        qg32 = qg.astype(F32)
        bits = bits_ref[b, g]
        s_tiles, v_tiles = [], []
        for p in range(n_pages):
            s = _dot_nt(qg, kp[p][0, :, gs].astype(BF16)) + ts_ref[g, :, p * PAGE_SIZE:(p + 1) * PAGE_SIZE]
            sel = jnp.zeros((1, LANES), jnp.int32)
            for j in range(blk_per_page):
                bit = lax.shift_right_logical(bits, blk_per_page * p + j) & 1
                sel = jnp.where((lane >= j * SLC_BLOCK) & (lane < (j + 1) * SLC_BLOCK), bit, sel)
            s_tiles.append(jnp.where(sel == 1, s, NEG_INF))
            v_tiles.append(vp[p][0, :, gs])
        past = n_pages * PAGE_SIZE
        s_new = jnp.sum(qg32 * nks_ref[0, :, gs], axis=-1, keepdims=True) + ts_ref[g, :, past:past + 1]
        o_slc = softmax_pv(s_tiles, v_tiles, s_new, nvs_ref[0, :, gs])
        s_tiles, v_tiles = [], []
        for p in range(n_win // PAGE_SIZE):
            rows = slice(p * PAGE_SIZE, (p + 1) * PAGE_SIZE)
            s_tiles.append(_dot_nt(qg, kwin_ref[0, rows, gs].astype(BF16)) + tw_ref[g, :, rows])
            v_tiles.append(vwin_ref[0, rows, gs])
        s_new = jnp.sum(qg32 * nkw_ref[0, :, gs], axis=-1, keepdims=True) + tw_ref[g, :, n_win:n_win + 1]
        o_win = softmax_pv(s_tiles, v_tiles, s_new, nvw_ref[0, :, gs])
        outs = (oc_ref[0, g], o_slc, o_win)
        for r in range(GROUP_REP):
            h = GROUP_REP * g + r
            hs = slice(h * HEAD_DIM, (h + 1) * HEAD_DIM)
            y = jnp.zeros((1, HEAD_DIM), F32)
            for br in range(N_BRANCH):
                col = g * LANES + br * GROUP_REP + r
                zs = slice((1 + br) * d_att + h * HEAD_DIM, (1 + br) * d_att + (h + 1) * HEAD_DIM)
                y = y + gt_ref[0, :, col:col + 1] * outs[br][r:r + 1, :] * _silu(z_ref[0, :, zs])
            y_ref[0, :, hs] = y


def _attn_sample(cache_k, cache_v, page_table, bits, k_win, v_win, new_rows, proj, gates, o_cmp, ts, tw):
    nb, n_pages = page_table.shape
    width = cache_k.shape[-1]
    n_win = k_win.shape[1]
    d_att = N_HEADS * HEAD_DIM
    row_spec = lambda w: pl.BlockSpec((1, 1, w), lambda b, bits, pt: (b, 0, 0))
    const = lambda a: pl.BlockSpec(a.shape, lambda b, bits, pt: (0,) * a.ndim)
    page_specs = [pl.BlockSpec((1, PAGE_SIZE, width), lambda b, bits, pt, p=p: (pt[b, p], 0, 0)) for p in range(n_pages)]
    grid_spec = pltpu.PrefetchScalarGridSpec(
        num_scalar_prefetch=2,
        grid=(nb,),
        in_specs=(page_specs * 2
                  + [pl.BlockSpec((1, n_win, width), lambda b, bits, pt: (b, 0, 0))] * 2
                  + [row_spec(width)] * 4
                  + [row_spec(proj.shape[-1]), row_spec(proj.shape[-1]), row_spec(gates.shape[-1]),
                     pl.BlockSpec((1, N_KV, SUBLANES, HEAD_DIM), lambda b, bits, pt: (b, 0, 0, 0)),
                     const(ts), const(tw)]),
        out_specs=pl.BlockSpec((1, 1, d_att), lambda b, bits, pt: (b, 0, 0)),
    )
    return pl.pallas_call(
        functools.partial(_attn_sample_kernel, n_pages=n_pages),
        grid_spec=grid_spec,
        out_shape=jax.ShapeDtypeStruct((nb, 1, d_att), F32),
        compiler_params=_params("parallel"),
        name="nsa_attention_sample",
    )(bits, page_table, *([cache_k] * n_pages), *([cache_v] * n_pages), k_win, v_win, *new_rows,
      proj, proj, gates, o_cmp, ts, tw)


def _overlap(n_cmp, n_slc):
    cs = np.arange(n_cmp) * CMP_STRIDE
    bs = np.arange(n_slc) * SLC_BLOCK
    ov = np.clip(np.minimum(cs[:, None] + CMP_LEN, bs[None, :] + SLC_BLOCK) - np.maximum(cs[:, None], bs[None, :]), 0, None)
    return (ov / CMP_LEN).astype(np.float32)


def _compress_weights(w1, w2, pe):
    half = CMP_STRIDE * HEAD_DIM
    wcat = jnp.concatenate([w1[:half], w1[half:]], axis=1).astype(BF16)
    pe8 = jnp.pad(pe.reshape(2, half), ((0, SUBLANES - 2), (0, 0)))
    return wcat, w2.astype(BF16), pe8


def _group_rows(tbl):
    l = tbl.shape[-1]
    return jnp.pad(tbl.reshape(N_KV, GROUP_REP, l), ((0, 0), (0, SUBLANES - GROUP_REP), (0, 0)))


def kernel(x_prompt, x_sample, state_pool, cache_k_cmp, cache_v_cmp, cache_k_slc, cache_v_slc, state_k_win, state_v_win, page_table, c_prompt, c_sample, g_norm, w_ada, b_ada, w_in_a, w_grp, pool_scale, w_out_a, g_kv, w_kv, pe_k, w_ck1, w_ck2, pe_v, w_cv1, w_cv2, rel_bias, w_in_b, w_out_b, g_final):
    bsz, t, d = x_prompt.shape
    nb = x_sample.shape[0]
    n_pages = page_table.shape[1]
    past = n_pages * PAGE_SIZE
    d_att = N_HEADS * HEAD_DIM
    d_kv = N_KV * HEAD_DIM
    n_a = w_in_a.shape[0]
    assert n_a == 1 and w_in_b.shape[0] == 1 and x_sample.shape[1] == 1
    assert t % TQ == 0 and t >= WINDOW and past % PAGE_SIZE == 0 and state_k_win.shape[1] == WINDOW

    w_in_a16 = w_in_a[0].astype(BF16)
    w_grp16 = w_grp[0].astype(BF16)
    w_out_a16 = w_out_a[0].astype(BF16)
    w_kv16 = w_kv.astype(BF16)
    w_in_b16 = w_in_b[0].astype(BF16)
    w_out_b16 = w_out_b[0].astype(BF16)
    wg = w_in_b[0][:, (1 + N_BRANCH) * d_att:].reshape(d, N_BRANCH, N_KV, GROUP_REP).transpose(0, 2, 1, 3)
    wg = jnp.pad(wg.reshape(d, N_KV, N_BRANCH * GROUP_REP), ((0, 0), (0, 0), (0, LANES - N_BRANCH * GROUP_REP)))
    w_gate16 = wg.reshape(d, N_KV * LANES).astype(BF16)
    cw_k = _compress_weights(w_ck1, w_ck2, pe_k)
    cw_v = _compress_weights(w_cv1, w_cv2, pe_v)
    cw = cw_k + cw_v

    mod = _ada(jnp.concatenate([c_prompt, c_sample], axis=0), w_ada, b_ada)

    def modulation(l, lo, hi, per_row):
        parts = [mod[l, lo:hi, k * d:(k + 1) * d] for k in range(3)]
        return [p[None] if per_row else p[:, None] for p in parts]

    t0, t1, tc, ts, tw, tcs = _bias_tables(rel_bias, past=past, n_cmp_s=past // CMP_STRIDE)
    ts, tw, tcs = _group_rows(ts), _group_rows(tw), _group_rows(tcs)

    shift, scale, gate = modulation(0, 0, bsz, False)
    uz = _nm_matmul(x_prompt, g_norm[0], scale, shift, w_in_a16, n_cols=2 * d, tm=512, tn=512, name="in_proj_pool_prompt")
    pool_p = uz[:, t - POOL_BUF:, :d][None]
    y0 = _pool_prompt(uz, w_grp16, pool_scale[0], tm=256)
    x1 = _proj_residual(y0, w_out_a16, x_prompt, gate, g_final, tm=512, final_norm=False, name="out_proj_pool_prompt")
    kv_p = _kv_proj(x1, g_kv, w_kv16, tm=256, name="kv_proj_prompt")
    kc_p, vc_p = _compress_prompt(kv_p[0], kv_p[1], cw)
    shift, scale, gate = modulation(1, 0, bsz, False)
    proj = _nm_matmul(x1, g_norm[1], scale, shift, w_in_b16, n_cols=(1 + N_BRANCH) * d_att, tm=512, tn=512,
                      name="in_proj_nsa_prompt")
    gates = _nm_matmul(x1, g_norm[1], scale, shift, w_gate16, n_cols=N_KV * LANES, tm=512, tn=N_KV * LANES, sigmoid=True,
                       name="gate_proj_prompt")
    ovl_t = jnp.asarray(np.pad(_overlap(t // CMP_STRIDE - 1, t // SLC_BLOCK), ((0, 1), (0, 0))).T).astype(BF16)
    flat = lambda a: a.reshape(bsz * t, a.shape[-1])
    y1 = _attn_prompt(flat(proj), flat(gates), kc_p, vc_p, flat(kv_p[2]), flat(kv_p[3]), flat(kv_p[4]), flat(kv_p[5]),
                      t0, t1, tc, ovl_t, rel_bias, bsz=bsz, t=t)
    y_prompt = _proj_residual(y1.reshape(bsz, t, d_att), w_out_b16, x1, gate, g_final, tm=512, final_norm=True,
                              name="out_proj_nsa_prompt")
    heads = lambda a: a.reshape(a.shape[0], a.shape[1], N_KV, HEAD_DIM)
    kv_state_p = [heads(a) for a in kv_p[:4]] + [heads(a[:, t - WINDOW:]) for a in kv_p[4:]]

    xs = x_sample.reshape(1, nb, d)
    shift, scale, gate = modulation(0, bsz, bsz + nb, True)
    uz_s = _nm_matmul(xs, g_norm[0], scale, shift, w_in_a16, n_cols=2 * d, tm=nb, tn=512, name="in_proj_pool_sample")
    pool_s = jnp.concatenate([state_pool[:, :, 1:], uz_s[0, :, None, :d][None]], axis=2)
    y0_s = _pool_sample(uz_s, state_pool[0], w_grp16, pool_scale[0], qpos=past, tb=min(32, nb))
    x1_s = _proj_residual(y0_s, w_out_a16, xs, gate, g_final, tm=nb, final_norm=False, name="out_proj_pool_sample")
    kv_s = _kv_proj(x1_s, g_kv, w_kv16, tm=nb, name="kv_proj_sample")
    new_rows = [a.reshape(nb, 1, d_kv) for a in kv_s]
    shift, scale, gate = modulation(1, bsz, bsz + nb, True)
    proj_s = _nm_matmul(x1_s, g_norm[1], scale, shift, w_in_b16, n_cols=(1 + N_BRANCH) * d_att, tm=nb, tn=512,
                        name="in_proj_nsa_sample").reshape(nb, 1, (1 + N_BRANCH) * d_att)
    gates_s = _nm_matmul(x1_s, g_norm[1], scale, shift, w_gate16, n_cols=N_KV * LANES, tm=nb, tn=N_KV * LANES, sigmoid=True,
                         name="gate_proj_sample").reshape(nb, 1, N_KV * LANES)
    paged = lambda c: c.reshape(c.shape[0], PAGE_SIZE, d_kv)
    n_cmp_s = past // CMP_STRIDE
    ovl_s = jnp.asarray(np.pad(_overlap(n_cmp_s, past // SLC_BLOCK + 1), ((0, 0), (0, LANES - past // SLC_BLOCK - 1)))).astype(BF16)
    rows_of = lambda c: c.reshape(c.shape[0], PAGE_SIZE * N_KV, HEAD_DIM)
    o_cmp, bits = _compress_sample(rows_of(cache_k_cmp), rows_of(cache_v_cmp), page_table, new_rows[0], new_rows[1], cw,
                                   proj_s, tcs, ovl_s, past=past)
    y1_s = _attn_sample(paged(cache_k_slc), paged(cache_v_slc), page_table, bits[:, :N_KV, 0],
                        state_k_win.reshape(nb, WINDOW, d_kv), state_v_win.reshape(nb, WINDOW, d_kv),
                        new_rows[2:], proj_s, gates_s, o_cmp, ts, tw)
    y_sample = _proj_residual(y1_s.reshape(1, nb, d_att), w_out_b16, x1_s, gate, g_final, tm=nb, final_norm=True,
                              name="out_proj_nsa_sample").reshape(nb, 1, d)
    new4 = [a.reshape(nb, 1, N_KV, HEAD_DIM) for a in kv_s]
    kv_state_s = new4[:4] + [jnp.concatenate([state_k_win[:, 1:], new4[4]], axis=1),
                             jnp.concatenate([state_v_win[:, 1:], new4[5]], axis=1)]

    return (y_prompt, y_sample, pool_p, *kv_state_p, pool_s, *kv_state_s)
```

```python
import functools
import math

import numpy as np
import jax
import jax.numpy as jnp
from jax import lax
from jax.experimental import pallas as pl
from jax.experimental.pallas import tpu as pltpu

F32 = jnp.float32
BF16 = jnp.bfloat16

PAGE_SIZE = 128
POOL_WINDOWS = (2, 4, 8, 16)
POOL_BUF = max(POOL_WINDOWS) - 1
HEAD_DIM = 128
N_KV = 4
GROUP_REP = 4
N_HEADS = N_KV * GROUP_REP
N_BRANCH = 3
CMP_LEN = 32
CMP_STRIDE = 16
SLC_BLOCK = 64
N_SEL = 8
WINDOW = 512
N_BUCKETS = 32
MAX_DISTANCE = 128
RMS_EPS = 1e-6
SCALE = HEAD_DIM ** -0.5
NEG_INF = -1e30
FORCE_SCORE = 1e9

LANES = 128
SUBLANES = 8
VMEM_LIMIT_BYTES = 56 * 1024 * 1024
TQ = 256
TM_PROJ = 1024
HALO = 16


def _bucket_starts():
    max_exact = N_BUCKETS // 2
    d = np.arange(0, MAX_DISTANCE + 1)
    large = max_exact + np.floor(
        np.log(np.maximum(d, max_exact) / max_exact) / math.log(MAX_DISTANCE / max_exact) * (N_BUCKETS - max_exact)
    ).astype(np.int64)
    bucket = np.where(d < max_exact, d, np.minimum(large, N_BUCKETS - 1))
    return [int(np.argmax(bucket >= k)) for k in range(N_BUCKETS)]


BUCKET_STARTS = _bucket_starts()
FAR_DIST = BUCKET_STARTS[-1]


def _params(*sem):
    return pltpu.CompilerParams(dimension_semantics=sem, vmem_limit_bytes=VMEM_LIMIT_BYTES)


def _silu(x):
    return x * jax.nn.sigmoid(x)


def _rms(x, g):
    return x * lax.rsqrt(jnp.mean(x * x, axis=-1, keepdims=True) + RMS_EPS) * g


def _dot(a, b):
    return jnp.dot(a, b, preferred_element_type=F32)


def _dot_nt(a, b):
    return lax.dot_general(a, b, (((1,), (1,)), ((), ())), preferred_element_type=F32)


def _ada_kernel(c_ref, w_ref, b_ref, o_ref):
    a = _silu(c_ref[...]).astype(BF16)
    o_ref[0] = _dot(a, w_ref[0].astype(BF16)) + b_ref[0]


def _ada(c_all, w_ada, b_ada):
    depth, d, n = w_ada.shape
    m = c_all.shape[0]
    tn = 512
    return pl.pallas_call(
        _ada_kernel,
        grid=(depth, n // tn),
        in_specs=[
            pl.BlockSpec((m, d), lambda l, j: (0, 0)),
            pl.BlockSpec((1, d, tn), lambda l, j: (l, 0, j)),
            pl.BlockSpec((1, 1, tn), lambda l, j: (l, 0, j)),
        ],
        out_specs=pl.BlockSpec((1, m, tn), lambda l, j: (l, 0, j)),
        out_shape=jax.ShapeDtypeStruct((depth, m, n), F32),
        compiler_params=_params("parallel", "parallel"),
        name="ada_modulation",
    )(c_all, w_ada, b_ada.reshape(depth, 1, n))


def _nm_matmul_kernel(x_ref, g_ref, sc_ref, sh_ref, w_ref, o_ref, h_scr, *, sigmoid):
    @pl.when(pl.program_id(2) == 0)
    def _():
        y = _rms(x_ref[0], g_ref[...])
        h_scr[...] = (y * (1.0 + sc_ref[0]) + sh_ref[0]).astype(BF16)

    r = _dot(h_scr[...], w_ref[...])
    o_ref[0] = jax.nn.sigmoid(r) if sigmoid else r


def _mod_spec(arr, tm):
    d = arr.shape[-1]
    if arr.shape[1] == 1:
        return pl.BlockSpec((1, 1, d), lambda b, i, *_: (b, 0, 0))
    return pl.BlockSpec((1, tm, d), lambda b, i, *_: (b, i, 0))


def _nm_matmul(x, g, scale, shift, w, *, n_cols, tm, tn, sigmoid=False, name):
    bsz, rows, d = x.shape
    return pl.pallas_call(
        functools.partial(_nm_matmul_kernel, sigmoid=sigmoid),
        grid=(bsz, rows // tm, n_cols // tn),
        in_specs=[
            pl.BlockSpec((1, tm, d), lambda b, i, j: (b, i, 0)),
            pl.BlockSpec((1, d), lambda b, i, j: (0, 0)),
            _mod_spec(scale, tm),
            _mod_spec(shift, tm),
            pl.BlockSpec((d, tn), lambda b, i, j: (0, j)),
        ],
        out_specs=pl.BlockSpec((1, tm, tn), lambda b, i, j: (b, i, j)),
        out_shape=jax.ShapeDtypeStruct((bsz, rows, n_cols), F32),
        scratch_shapes=[pltpu.VMEM((tm, d), BF16)],
        compiler_params=_params("parallel", "parallel", "arbitrary"),
        name=name,
    )(x, g.reshape(1, d), scale, shift, w)


def _pool_mix(pooled_fn, z, wg_ref, ps_ref, y_ref):
    grp = z.shape[-1] // len(POOL_WINDOWS)
    for gi, w in enumerate(POOL_WINDOWS):
        cs = slice(gi * grp, (gi + 1) * grp)
        mixed = _dot(pooled_fn(gi, w, cs).astype(BF16), wg_ref[gi])
        y_ref[0, :, cs] = mixed * ps_ref[:, cs] * _silu(z[:, cs])


def _pool_prompt_kernel(u_ref, z_ref, halo_ref, wg_ref, ps_ref, y_ref, ue_scr):
    i = pl.program_id(1)
    tm = u_ref.shape[1]
    u = u_ref[0]
    ue_scr[0:HALO, :] = jnp.where(i > 0, halo_ref[0], 0.0)
    ue_scr[HALO:HALO + tm, :] = u
    pos = i * tm + lax.broadcasted_iota(jnp.int32, (tm, 1), 0)

    def pooled(gi, w, cs):
        acc = u[:, cs]
        for k in range(1, w):
            acc = acc + ue_scr[HALO - k:HALO - k + tm, cs]
        cnt = jnp.minimum(pos + 1, w).astype(F32)
        return acc / cnt - u[:, cs]

    _pool_mix(pooled, z_ref[0], wg_ref, ps_ref, y_ref)


def _pool_prompt(uz, w_grp, pool_scale, *, tm):
    bsz, t, d2 = uz.shape
    d = d2 // 2
    ng, grp, _ = w_grp.shape
    return pl.pallas_call(
        _pool_prompt_kernel,
        grid=(bsz, t // tm),
        in_specs=[
            pl.BlockSpec((1, tm, d), lambda b, i: (b, i, 0)),
            pl.BlockSpec((1, tm, d), lambda b, i: (b, i, 1)),
            pl.BlockSpec((1, HALO, d), lambda b, i: (b, jnp.maximum(i * (tm // HALO) - 1, 0), 0)),
            pl.BlockSpec((ng, grp, grp), lambda b, i: (0, 0, 0)),
            pl.BlockSpec((1, d), lambda b, i: (0, 0)),
        ],
        out_specs=pl.BlockSpec((1, tm, d), lambda b, i: (b, i, 0)),
        out_shape=jax.ShapeDtypeStruct((bsz, t, d), F32),
        scratch_shapes=[pltpu.VMEM((HALO + tm, d), F32)],
        compiler_params=_params("parallel", "parallel"),
        name="pool_mixer_prompt",
    )(uz, uz, uz, w_grp, pool_scale.reshape(1, d))


def _pool_sample_kernel(u_ref, z_ref, pre_ref, wg_ref, ps_ref, y_ref, *, qpos):
    u = u_ref[0]

    def pooled(gi, w, cs):
        acc = u[:, cs]
        for k in range(1, w):
            acc = acc + pre_ref[:, POOL_BUF - k, cs]
        return acc / float(min(qpos + 1, w)) - u[:, cs]

    _pool_mix(pooled, z_ref[0], wg_ref, ps_ref, y_ref)


def _pool_sample(uz, prefix, w_grp, pool_scale, *, qpos, tb):
    _, nb, d2 = uz.shape
    d = d2 // 2
    ng, grp, _ = w_grp.shape
    return pl.pallas_call(
        functools.partial(_pool_sample_kernel, qpos=qpos),
        grid=(nb // tb,),
        in_specs=[
            pl.BlockSpec((1, tb, d), lambda i: (0, i, 0)),
            pl.BlockSpec((1, tb, d), lambda i: (0, i, 1)),
            pl.BlockSpec((tb, POOL_BUF, d), lambda i: (i, 0, 0)),
            pl.BlockSpec((ng, grp, grp), lambda i: (0, 0, 0)),
            pl.BlockSpec((1, d), lambda i: (0, 0)),
        ],
        out_specs=pl.BlockSpec((1, tb, d), lambda i: (0, i, 0)),
        out_shape=jax.ShapeDtypeStruct((1, nb, d), F32),
        compiler_params=_params("parallel"),
        name="pool_mixer_sample",
    )(uz, uz, prefix, w_grp, pool_scale.reshape(1, d))


def _proj_residual_kernel(y_ref, w_ref, x_ref, gate_ref, g_ref, o_ref, *, final_norm):
    xo = x_ref[0] + gate_ref[0] * _dot(y_ref[0].astype(BF16), w_ref[...])
    o_ref[0] = _rms(xo, g_ref[...]) if final_norm else xo


def _proj_residual(y, w, x, gate, g_final, *, tm, final_norm, name):
    bsz, rows, d = x.shape
    dy = y.shape[-1]
    return pl.pallas_call(
        functools.partial(_proj_residual_kernel, final_norm=final_norm),
        grid=(bsz, rows // tm),
        in_specs=[
            pl.BlockSpec((1, tm, dy), lambda b, i: (b, i, 0)),
            pl.BlockSpec((dy, d), lambda b, i: (0, 0)),
            pl.BlockSpec((1, tm, d), lambda b, i: (b, i, 0)),
            _mod_spec(gate, tm),
            pl.BlockSpec((1, d), lambda b, i: (0, 0)),
        ],
        out_specs=pl.BlockSpec((1, tm, d), lambda b, i: (b, i, 0)),
        out_shape=jax.ShapeDtypeStruct((bsz, rows, d), F32),
        compiler_params=_params("parallel", "parallel"),
        name=name,
    )(y, w, x, gate, g_final.reshape(1, d))


KV_BF16_COPIES = (2, 4)


def _kv_kernel(x_ref, g_ref, w_ref, *o_refs):
    h = _rms(x_ref[0], g_ref[...]).astype(BF16)
    n_out = 2 * N_BRANCH
    n = o_refs[0].shape[-1]
    for o in range(n_out):
        r = _dot(h, w_ref[:, o * n:(o + 1) * n])
        o_refs[o][0] = r
        if o in KV_BF16_COPIES:
            c_ref = o_refs[n_out + KV_BF16_COPIES.index(o)]
            for g in range(N_KV):
                c_ref[0, g] = r[:, g * HEAD_DIM:(g + 1) * HEAD_DIM].astype(BF16)


def _kv_proj(x, g_kv, w_kv, *, tm, name):
    bsz, rows, d = x.shape
    n_out = 2 * N_BRANCH
    n = w_kv.shape[1] // n_out
    n_cp = len(KV_BF16_COPIES)
    return pl.pallas_call(
        _kv_kernel,
        grid=(bsz, rows // tm),
        in_specs=[
            pl.BlockSpec((1, tm, d), lambda b, i: (b, i, 0)),
            pl.BlockSpec((1, d), lambda b, i: (0, 0)),
            pl.BlockSpec((d, n_out * n), lambda b, i: (0, 0)),
        ],
        out_specs=([pl.BlockSpec((1, tm, n), lambda b, i: (b, i, 0))] * n_out
                   + [pl.BlockSpec((1, N_KV, tm, HEAD_DIM), lambda b, i: (b, 0, i, 0))] * n_cp),
        out_shape=([jax.ShapeDtypeStruct((bsz, rows, n), F32)] * n_out
                   + [jax.ShapeDtypeStruct((bsz, N_KV, rows, HEAD_DIM), BF16)] * n_cp),
        compiler_params=_params("parallel", "parallel"),
        name=name,
    )(x, g_kv.reshape(1, d), w_kv)


def _bias_lookup(rb_ref, h, dist):
    acc = jnp.full(dist.shape, rb_ref[N_BUCKETS - 1, h], F32)
    for k in range(N_BUCKETS - 2, -1, -1):
        acc = jnp.where(dist < BUCKET_STARTS[k + 1], rb_ref[k, h], acc)
    return acc


def _bias_tables_kernel(rb_ref, t0_ref, t1_ref, tc_ref, ts_ref, tw_ref, tcs_ref, *, past):
    h = pl.program_id(0)
    key = lax.broadcasted_iota(jnp.int32, (TQ, TQ), 0)
    qry = lax.broadcasted_iota(jnp.int32, (TQ, TQ), 1)
    t0_ref[0] = _bias_lookup(rb_ref, h, qry - key)
    t1_ref[0] = _bias_lookup(rb_ref, h, TQ + qry - key)
    tc_ref[0] = _bias_lookup(rb_ref, h, qry - CMP_STRIDE * (key - TQ // 2) - (CMP_LEN - 1))
    ls = lax.broadcasted_iota(jnp.int32, ts_ref.shape[1:], 1)
    own = (ls % N_KV) == h // GROUP_REP
    tok = ls // N_KV
    ts_ref[0] = jnp.where(ls == N_KV * past, rb_ref[0, h],
                          jnp.where(own & (tok < past), _bias_lookup(rb_ref, h, past - tok), NEG_INF))
    lw = lax.broadcasted_iota(jnp.int32, tw_ref.shape[1:], 1)
    own = (lw % N_KV) == h // GROUP_REP
    dw = WINDOW - lw // N_KV
    tw_ref[0] = jnp.where(lw == N_KV * WINDOW, rb_ref[0, h],
                          jnp.where(own & (dw > 0) & (dw < WINDOW) & (past - dw >= 0), _bias_lookup(rb_ref, h, dw), NEG_INF))
    lc = lax.broadcasted_iota(jnp.int32, tcs_ref.shape[1:], 1)
    dc = past - CMP_STRIDE * lc - (CMP_LEN - 1)
    tcs_ref[0] = jnp.where(dc >= 0, _bias_lookup(rb_ref, h, dc), NEG_INF)


def _bias_tables(rel_bias, *, past, n_cmp_s):
    nh = rel_bias.shape[1]
    ls = N_KV * past + LANES
    lw = N_KV * WINDOW + LANES
    shapes = [(nh, TQ, TQ)] * 3 + [(nh, 1, ls), (nh, 1, lw), (nh, 1, n_cmp_s)]
    return pl.pallas_call(
        functools.partial(_bias_tables_kernel, past=past),
        grid=(nh,),
        in_specs=[pl.BlockSpec(memory_space=pltpu.SMEM)],
        out_specs=[pl.BlockSpec((1,) + s[1:], lambda h: (h, 0, 0)) for s in shapes],
        out_shape=[jax.ShapeDtypeStruct(s, F32) for s in shapes],
        compiler_params=_params("parallel"),
        name="rel_bias_tables",
    )(rel_bias)


def _compress_fill(page_refs, lhs_scr):
    sub_per_page = PAGE_SIZE // CMP_STRIDE
    pairs = CMP_STRIDE // 2
    rows_pp = sub_per_page * SUBLANES
    even = (lax.broadcasted_iota(jnp.int32, (rows_pp, 1), 0) & (SUBLANES - 1)) < N_KV
    for p, pref in enumerate(page_refs):
        x4 = pref[0].reshape(sub_per_page, pairs, SUBLANES, HEAD_DIM)
        rs = slice(p * rows_pp, (p + 1) * rows_pp)
        for j in range(pairs):
            t = x4[:, j].reshape(rows_pp, HEAD_DIM)
            lhs_scr[rs, 2 * j * HEAD_DIM:(2 * j + 1) * HEAD_DIM] = jnp.where(even, t, 0.0).astype(BF16)
            lhs_scr[rs, (2 * j + 1) * HEAD_DIM:(2 * j + 2) * HEAD_DIM] = jnp.where(even, 0.0, t).astype(BF16)


def _compress_finish(acc, new_ref, wcat_ref, w2_ref, pe_ref, t_scr, p_scr, *, has_new):
    n_rows = acc.shape[0]
    n_sub = n_rows // SUBLANES
    wcat = wcat_ref[...]
    t_scr[0:n_rows, :] = acc
    t_scr[n_rows:, :] = jnp.zeros((2 * SUBLANES, 2 * HEAD_DIM), F32)
    t_scr[0:n_rows, :] = t_scr[0:n_rows, :] + t_scr[N_KV:n_rows + N_KV, :]
    if has_new:
        new8 = jnp.concatenate([new_ref[0, :, g * HEAD_DIM:(g + 1) * HEAD_DIM] for g in range(N_KV)]
                               + [jnp.zeros((SUBLANES - N_KV, HEAD_DIM), F32)], axis=0)
        t_scr[n_rows:n_rows + SUBLANES, HEAD_DIM:] = _dot(new8.astype(BF16), wcat[0:HEAD_DIM, HEAD_DIM:])
    pe_r = _dot(pe_ref[...].astype(BF16), wcat)
    pe_const = pe_r[0:1, :HEAD_DIM] + pe_r[1:2, HEAD_DIM:]
    p_scr[...] = t_scr[0:n_rows, :HEAD_DIM] + t_scr[SUBLANES:n_rows + SUBLANES, HEAD_DIM:] + pe_const
    pre = jnp.concatenate([p_scr[pl.ds(g, n_sub, stride=SUBLANES), :] for g in range(N_KV)], axis=0)
    out = _dot(_silu(pre).astype(BF16), w2_ref[...])
    if not has_new:
        blk = lax.broadcasted_iota(jnp.int32, (N_KV * n_sub, 1), 0) & (n_sub - 1)
        out = jnp.where(blk == n_sub - 1, 0.0, out)
    return [out[g * n_sub:(g + 1) * n_sub] for g in range(N_KV)]


def _compress_kv(k_pages, v_pages, k_new, v_new, k_w, v_w, scr, *, has_new):
    _compress_fill(k_pages, scr[0])
    _compress_fill(v_pages, scr[3])
    acc_k = _dot(scr[0][...], k_w[0][...])
    acc_v = _dot(scr[3][...], v_w[0][...])
    kcs = _compress_finish(acc_k, k_new, *k_w, scr[1], scr[2], has_new=has_new)
    vcs = _compress_finish(acc_v, v_new, *v_w, scr[4], scr[5], has_new=has_new)
    return kcs, vcs


def _compress_scratch(n_pages):
    n_rows = n_pages * (PAGE_SIZE // CMP_STRIDE) * SUBLANES
    return [pltpu.VMEM((n_rows, CMP_STRIDE * HEAD_DIM), BF16),
            pltpu.VMEM((n_rows + 2 * SUBLANES, 2 * HEAD_DIM), F32),
            pltpu.VMEM((n_rows, HEAD_DIM), F32)]


def _compress_prompt_kernel(pt_ref, *refs, n_pages):
    kp, vp = refs[:n_pages], refs[n_pages:2 * n_pages]
    wck, w2k, pek, wcv, w2v, pev, kc_ref, vc_ref, *scr = refs[2 * n_pages:]
    n_sub = n_pages * (PAGE_SIZE // CMP_STRIDE)
    assert n_sub & (n_sub - 1) == 0
    kcs, vcs = _compress_kv(kp, vp, None, None, (wck, w2k, pek), (wcv, w2v, pev), scr, has_new=False)
    for g in range(N_KV):
        kc_ref[0, g * n_sub:(g + 1) * n_sub, :] = kcs[g]
        vc_ref[0, g * n_sub:(g + 1) * n_sub, :] = vcs[g]


def _page_specs(n_pages):
    return [pl.BlockSpec((1, PAGE_SIZE * N_KV, HEAD_DIM), lambda b, pt, p=p: (pt[b, p], 0, 0)) for p in range(n_pages)]


def _const_spec(shape):
    return pl.BlockSpec(shape, lambda b, pt: (0,) * len(shape))


def _compress_prompt(k_rows, v_rows, cw):
    bsz = k_rows.shape[0]
    t = k_rows.shape[1] // N_KV
    n_pages = t // PAGE_SIZE
    n_sub = t // CMP_STRIDE
    pt = jnp.arange(bsz * n_pages, dtype=jnp.int32).reshape(bsz, n_pages)
    kp = k_rows.reshape(bsz * n_pages, PAGE_SIZE * N_KV, HEAD_DIM)
    vp = v_rows.reshape(bsz * n_pages, PAGE_SIZE * N_KV, HEAD_DIM)
    wspecs = [_const_spec(a.shape) for a in cw]
    grid_spec = pltpu.PrefetchScalarGridSpec(
        num_scalar_prefetch=1,
        grid=(bsz,),
        in_specs=_page_specs(n_pages) * 2 + wspecs,
        out_specs=[pl.BlockSpec((1, N_KV * n_sub, HEAD_DIM), lambda b, pt: (b, 0, 0))] * 2,
        scratch_shapes=_compress_scratch(n_pages) * 2,
    )
    return pl.pallas_call(
        functools.partial(_compress_prompt_kernel, n_pages=n_pages),
        grid_spec=grid_spec,
        out_shape=[jax.ShapeDtypeStruct((bsz, N_KV * n_sub, HEAD_DIM), F32)] * 2,
        compiler_params=_params("parallel"),
        name="compress_prompt",
    )(pt, *([kp] * n_pages), *([vp] * n_pages), *cw)


def _query_rows(q_ref, g):
    rows = [q_ref[0, :, (GROUP_REP * g + r) * HEAD_DIM:(GROUP_REP * g + r + 1) * HEAD_DIM] for r in range(GROUP_REP)]
    rows.append(jnp.zeros((SUBLANES - GROUP_REP, HEAD_DIM), F32))
    return (jnp.concatenate(rows, axis=0) * SCALE).astype(BF16)


def _split_dot(a, b):
    hi = a.astype(BF16)
    lo = (a - hi.astype(F32)).astype(BF16)
    return _dot(hi, b) + _dot(lo, b)


def _compress_sample_kernel(pt_ref, *refs, n_pages, cur_blk):
    kp, vp = refs[:n_pages], refs[n_pages:2 * n_pages]
    (newk, newv, wck, w2k, pek, wcv, w2v, pev, q_ref, tcs_ref, ovl_ref, pow_ref,
     oc_ref, bits_ref, *scr) = refs[2 * n_pages:]
    kcs, vcs = _compress_kv(kp, vp, newk, newv, (wck, w2k, pek), (wcv, w2v, pev), scr, has_new=True)
    lane = lax.broadcasted_iota(jnp.int32, (1, LANES), 1)
    kidx = lax.broadcasted_iota(jnp.int32, (LANES, LANES), 0)
    jidx = lax.broadcasted_iota(jnp.int32, (LANES, LANES), 1)
    for g in range(N_KV):
        s = _dot_nt(_query_rows(q_ref, g), kcs[g].astype(BF16)) + tcs_ref[g]
        e = jnp.exp(s - jnp.max(s, axis=-1, keepdims=True))
        p = e / jnp.sum(e, axis=-1, keepdims=True)
        oc_ref[0, GROUP_REP * g:GROUP_REP * (g + 1), :] = _dot(p.astype(BF16), vcs[g].astype(BF16))[0:GROUP_REP]
        psum = jnp.sum(p[0:GROUP_REP], axis=0, keepdims=True)
        imp = _split_dot(jnp.broadcast_to(psum, (SUBLANES, LANES)), ovl_ref[...])[0:1]
        forced = (lane == cur_blk) | (lane == 0)
        imp = jnp.where(forced, FORCE_SCORE, jnp.where(lane <= cur_blk, imp, -FORCE_SCORE))
        a = jnp.broadcast_to(imp, (LANES, LANES))
        b = a.T
        beats = jnp.where(b > a, 1.0, jnp.where((b == a) & (kidx < jidx), 1.0, 0.0))
        rank = jnp.sum(beats, axis=0, keepdims=True)
        sel = jnp.where(rank < N_SEL, 1.0, 0.0) * pow_ref[...]
        lo = jnp.sum(jnp.where(lane < 16, sel, 0.0), axis=-1, keepdims=True).astype(jnp.int32)
        hi = jnp.sum(jnp.where((lane >= 16) & (lane < 32), sel, 0.0), axis=-1, keepdims=True).astype(jnp.int32)
        bits_ref[0, g:g + 1, :] = jnp.broadcast_to(lo | (hi << 16), (1, LANES))


def _compress_sample(cache_k, cache_v, page_table, new_k, new_v, cw, q, tcs, ovl, *, past):
    nb, n_pages = page_table.shape
    width = new_k.shape[-1]
    n_sub = n_pages * (PAGE_SIZE // CMP_STRIDE)
    pow2 = np.zeros((1, LANES), np.float32)
    pow2[0, :32] = 2.0 ** (np.arange(32) % 16)
    row_spec = pl.BlockSpec((1, 1, width), lambda b, pt: (b, 0, 0))
    grid_spec = pltpu.PrefetchScalarGridSpec(
        num_scalar_prefetch=1,
        grid=(nb,),
        in_specs=(_page_specs(n_pages) * 2 + [row_spec, row_spec] + [_const_spec(a.shape) for a in cw]
                  + [pl.BlockSpec((1, 1, q.shape[-1]), lambda b, pt: (b, 0, 0)),
                     _const_spec(tcs.shape), _const_spec(ovl.shape), _const_spec(pow2.shape)]),
        out_specs=[pl.BlockSpec((1, N_HEADS, HEAD_DIM), lambda b, pt: (b, 0, 0)),
                   pl.BlockSpec((1, SUBLANES, LANES), lambda b, pt: (b, 0, 0))],
        scratch_shapes=_compress_scratch(n_pages) * 2,
    )
    return pl.pallas_call(
        functools.partial(_compress_sample_kernel, n_pages=n_pages, cur_blk=past // SLC_BLOCK),
        grid_spec=grid_spec,
        out_shape=[jax.ShapeDtypeStruct((nb, N_HEADS, HEAD_DIM), F32),
                   jax.ShapeDtypeStruct((nb, SUBLANES, LANES), jnp.int32)],
        compiler_params=_params("parallel"),
        name="compress_select_sample",
    )(page_table, *([cache_k] * n_pages), *([cache_v] * n_pages), new_k, new_v, *cw, q, tcs, ovl, jnp.asarray(pow2))


def _attn_prompt_kernel(rb_ref, q_ref, z0_ref, z1_ref, z2_ref, gt_ref, kc_ref, vc_ref, ks_ref, vs_ref, kw_ref, vw_ref,
                        t0_ref, t1_ref, tc_ref, ovl_ref, y_ref, m_scr, l_scr, acc_scr, sel_scr):
    g = pl.program_id(1)
    qt = pl.program_id(2)
    q = q_ref[...]
    qf = jnp.concatenate([q[:, r * HEAD_DIM:(r + 1) * HEAD_DIM] for r in range(GROUP_REP)], axis=0)
    qf = (qf * SCALE).astype(BF16)
    lane_q = lax.broadcasted_iota(jnp.int32, (1, TQ), 1)
    qpos = qt * TQ + lane_q
    key_l = lax.broadcasted_iota(jnp.int32, (TQ, 1), 0)
    gate_t = gt_ref[...].T
    z_refs = (z0_ref, z1_ref, z2_ref)

    def emit(br, r, o_t, first):
        cs = slice(r * HEAD_DIM, (r + 1) * HEAD_DIM)
        term = o_t.T * _silu(z_refs[br][:, cs])
        if first:
            y_ref[:, cs] = term
        else:
            y_ref[:, cs] += term

    n_cmp = kc_ref.shape[1]
    s_all = _dot_nt(kc_ref[0].astype(BF16), qf)
    vc_t = vc_ref[0].T.astype(BF16)
    cmp_end = lax.broadcasted_iota(jnp.int32, (n_cmp, 1), 0) * CMP_STRIDE + (CMP_LEN - 1)
    vis = qpos >= cmp_end
    any_vis = qpos >= CMP_LEN - 1
    row0 = pl.multiple_of(TQ // 2 - (TQ // CMP_STRIDE) * qt, CMP_STRIDE)
    psum = jnp.zeros((n_cmp, TQ), F32)
    for r in range(GROUP_REP):
        s = s_all[:, r * TQ:(r + 1) * TQ] + tc_ref[r, pl.ds(row0, n_cmp), :]
        s = jnp.where(vis, s, NEG_INF)
        e = jnp.exp(s - jnp.max(s, axis=0, keepdims=True))
        p = jnp.where(any_vis, e / jnp.sum(e, axis=0, keepdims=True), 0.0)
        psum = psum + p
        emit(0, r, _dot(vc_t, p.astype(BF16)) * gate_t[r:r + 1, :], True)

    n_slc = ovl_ref.shape[0]
    psum_hi = psum.astype(BF16)
    imp = _dot(ovl_ref[...], psum_hi) + _dot(ovl_ref[...], (psum - psum_hi.astype(F32)).astype(BF16))
    blk = lax.broadcasted_iota(jnp.int32, (n_slc, 1), 0)
    cur = lax.shift_right_logical(qpos, int(math.log2(SLC_BLOCK)))
    forced = (blk == cur) | (blk == 0)
    imp = jnp.where(forced, FORCE_SCORE, jnp.where(blk <= cur, imp, -FORCE_SCORE))
    rank = jnp.zeros((n_slc, TQ), F32)
    for k in range(n_slc):
        rk = imp[k:k + 1, :]
        rank = rank + jnp.where(rk > imp, 1.0, jnp.where((rk == imp) & (blk > k), 1.0, 0.0))
    sel = jnp.where(rank < min(N_SEL, n_slc), 1.0, 0.0)
    blk_per_tile = TQ // SLC_BLOCK
    for t in range(n_slc // blk_per_tile):
        sel_scr[t, 0:blk_per_tile, :] = sel[t * blk_per_tile:(t + 1) * blk_per_tile, :]

    def attend(st, k_ref, v_ref, kt, kind, use_sel, first):
        start = pl.multiple_of(kt * TQ, TQ)
        s_all = _dot_nt(k_ref[pl.ds(start, TQ), :].astype(BF16), qf)
        v_t = v_ref[pl.ds(start, TQ), :].T.astype(BF16)
        mask = None
        if kind == "diag":
            mask = key_l <= lane_q
        elif kind == "winfar":
            mask = key_l > lane_q
        if use_sel:
            sel4 = sel_scr[kt, 0:blk_per_tile, :]
            selm = jnp.concatenate(
                [jnp.broadcast_to(sel4[j:j + 1, :], (SLC_BLOCK, TQ)) for j in range(blk_per_tile)], axis=0) > 0.5
            mask = selm if mask is None else mask & selm
        for r in range(GROUP_REP):
            sl = slice(r * TQ, (r + 1) * TQ)
            s = s_all[:, sl]
            if kind == "diag":
                s = s + t0_ref[r]
            elif kind == "near":
                s = s + t1_ref[r]
            else:
                s = s + rb_ref[N_BUCKETS - 1, GROUP_REP * g + r]
            if mask is not None:
                s = jnp.where(mask, s, NEG_INF)
            mx = jnp.max(s, axis=0, keepdims=True)
            if first:
                m_new = mx
                p = jnp.exp(s - m_new)
                l_scr[st, :, sl] = jnp.sum(p, axis=0, keepdims=True)
                acc_scr[st, :, sl] = _dot(v_t, p.astype(BF16))
            else:
                m_old = m_scr[st, :, sl]
                m_new = jnp.maximum(m_old, mx)
                alpha = jnp.exp(m_old - m_new)
                p = jnp.exp(s - m_new)
                l_scr[st, :, sl] = alpha * l_scr[st, :, sl] + jnp.sum(p, axis=0, keepdims=True)
                acc_scr[st, :, sl] = alpha * acc_scr[st, :, sl] + _dot(v_t, p.astype(BF16))
            m_scr[st, :, sl] = m_new

    attend(0, ks_ref, vs_ref, qt, "diag", True, True)
    attend(1, kw_ref, vw_ref, qt, "diag", False, True)

    @pl.when(qt >= 1)
    def _():
        attend(0, ks_ref, vs_ref, qt - 1, "near", True, False)
        attend(1, kw_ref, vw_ref, qt - 1, "near", False, False)

    @pl.when(qt >= 2)
    def _():
        attend(1, kw_ref, vw_ref, qt - 2, "winfar", False, False)

        def far(kt, carry):
            attend(0, ks_ref, vs_ref, kt, "far", True, False)
            return carry

        lax.fori_loop(0, qt - 1, far, 0)

    for st in range(2):
        br = st + 1
        for r in range(GROUP_REP):
            sl = slice(r * TQ, (r + 1) * TQ)
            w = gate_t[br * GROUP_REP + r:br * GROUP_REP + r + 1, :] / l_scr[st, :, sl]
            emit(br, r, acc_scr[st, :, sl] * w, False)


def _attn_prompt(proj, gates, kc, vc, k_slc, v_slc, k_win, v_win, t0, t1, tc, ovl_t, rel_bias, *, bsz, t):
    nq = t // TQ
    gw = GROUP_REP * HEAD_DIM
    assert TQ // 2 - (TQ // CMP_STRIDE) * (nq - 1) >= 0 and kc.shape[1] == N_KV * (t // CMP_STRIDE)
    n_cmp = t // CMP_STRIDE
    d_att = N_HEADS * HEAD_DIM
    zoff = d_att // gw
    row = lambda b, g, i: b * nq + i
    in_specs = [
        pl.BlockSpec(memory_space=pltpu.SMEM),
        pl.BlockSpec((TQ, gw), lambda b, g, i: (row(b, g, i), g)),
    ] + [
        pl.BlockSpec((TQ, gw), lambda b, g, i, br=br: (row(b, g, i), zoff * (1 + br) + g)) for br in range(N_BRANCH)
    ] + [
        pl.BlockSpec((TQ, LANES), lambda b, g, i: (row(b, g, i), g)),
        pl.BlockSpec((1, n_cmp, HEAD_DIM), lambda b, g, i: (b, g, 0)),
        pl.BlockSpec((1, n_cmp, HEAD_DIM), lambda b, g, i: (b, g, 0)),
    ] + [pl.BlockSpec((None, None, t, HEAD_DIM), lambda b, g, i: (b, g, 0, 0)),
         pl.BlockSpec((t, HEAD_DIM), lambda b, g, i: (b, g))] * 2 + [
        pl.BlockSpec((GROUP_REP, TQ, TQ), lambda b, g, i: (g, 0, 0))] * 3 + [
        pl.BlockSpec(ovl_t.shape, lambda b, g, i: (0, 0)),
    ]
    return pl.pallas_call(
        _attn_prompt_kernel,
        grid=(bsz, N_KV, nq),
        in_specs=in_specs,
        out_specs=pl.BlockSpec((TQ, gw), lambda b, g, i: (row(b, g, i), g)),
        out_shape=jax.ShapeDtypeStruct((bsz * t, d_att), F32),
        scratch_shapes=[
            pltpu.VMEM((2, 1, GROUP_REP * TQ), F32),
            pltpu.VMEM((2, 1, GROUP_REP * TQ), F32),
            pltpu.VMEM((2, HEAD_DIM, GROUP_REP * TQ), F32),
            pltpu.VMEM((t // TQ, SUBLANES, TQ), F32),
        ],
        compiler_params=_params("parallel", "parallel", "arbitrary"),
        name="nsa_attention_prompt",
    )(rel_bias, proj, proj, proj, proj, gates, kc, vc, k_slc, v_slc, k_win, v_win, t0, t1, tc, ovl_t)


def _attn_sample_kernel(bits_ref, pt_ref, *refs, n_pages):
    kp, vp = refs[:n_pages], refs[n_pages:2 * n_pages]
    (kwin_ref, vwin_ref, nks_ref, nvs_ref, nkw_ref, nvw_ref, q_ref, gt_ref, oc_ref, ts_ref, tw_ref,
     y_ref) = refs[2 * n_pages:]
    b = pl.program_id(0)
    rows_pp = PAGE_SIZE * N_KV
    blk_rows = SLC_BLOCK * N_KV
    d_att = N_HEADS * HEAD_DIM
    n_win_rows = kwin_ref.shape[1]

    def head_rows(base):
        return jnp.concatenate([q_ref[0, :, base + h * HEAD_DIM:base + (h + 1) * HEAD_DIM] for h in range(N_HEADS)], axis=0)

    def group_rows(ref):
        return jnp.concatenate([ref[0, :, (h // GROUP_REP) * HEAD_DIM:(h // GROUP_REP + 1) * HEAD_DIM]
                                for h in range(N_HEADS)], axis=0)

    def softmax_pv(s_tiles, v_tiles, s_new, v_new):
        m = s_new
        for s in s_tiles:
            m = jnp.maximum(m, jnp.max(s, axis=-1, keepdims=True))
        p_new = jnp.exp(s_new - m)
        l = p_new
        acc = p_new * v_new
        for s, v in zip(s_tiles, v_tiles):
            p = jnp.exp(s - m)
            l = l + jnp.sum(p, axis=-1, keepdims=True)
            acc = acc + _dot(p.astype(BF16), v.astype(BF16))
        return acc / l

    q32 = head_rows(0) * SCALE
    q16 = q32.astype(BF16)
    head = lax.broadcasted_iota(jnp.int32, (N_HEADS, 1), 0)
    bits = jnp.zeros((N_HEADS, 1), jnp.int32)
    for g in range(N_KV):
        bits = jnp.where((head >= g * GROUP_REP) & (head < (g + 1) * GROUP_REP), bits_ref[b, g], bits)
    lane_blk = lax.broadcasted_iota(jnp.int32, (1, rows_pp), 1) // blk_rows
    s_tiles, v_tiles = [], []
    for p in range(n_pages):
        s = _dot_nt(q16, kp[p][0].astype(BF16)) + ts_ref[:, p * rows_pp:(p + 1) * rows_pp]
        shift = jnp.broadcast_to(lane_blk + p * (rows_pp // blk_rows), s.shape)
        sel = lax.shift_right_logical(jnp.broadcast_to(bits, s.shape), shift) & 1
        s_tiles.append(jnp.where(sel == 1, s, NEG_INF))
        v_tiles.append(vp[p][0])
    tail = n_pages * rows_pp
    s_new = jnp.sum(q32 * group_rows(nks_ref), axis=-1, keepdims=True) + ts_ref[:, tail:tail + 1]
    o_slc = softmax_pv(s_tiles, v_tiles, s_new, group_rows(nvs_ref))
    s_tiles, v_tiles = [], []
    for p in range(n_win_rows // rows_pp):
        rows = slice(p * rows_pp, (p + 1) * rows_pp)
        s_tiles.append(_dot_nt(q16, kwin_ref[0, rows, :].astype(BF16)) + tw_ref[:, rows])
        v_tiles.append(vwin_ref[0, rows, :])
    s_new = jnp.sum(q32 * group_rows(nkw_ref), axis=-1, keepdims=True) + tw_ref[:, n_win_rows:n_win_rows + 1]
    o_win = softmax_pv(s_tiles, v_tiles, s_new, group_rows(nvw_ref))
    y = jnp.zeros((N_HEADS, HEAD_DIM), F32)
    for br, o in enumerate((oc_ref[0], o_slc, o_win)):
        cols = [(h // GROUP_REP) * LANES + br * GROUP_REP + h % GROUP_REP for h in range(N_HEADS)]
        gate = jnp.concatenate([gt_ref[0, :, c:c + 1] for c in cols], axis=0)
        y = y + gate * o * _silu(head_rows((1 + br) * d_att))
    y_ref[0] = y


def _attn_sample(cache_k, cache_v, page_table, bits, k_win, v_win, new_rows, proj, gates, o_cmp, ts, tw):
    nb, n_pages = page_table.shape
    rows_pp = PAGE_SIZE * N_KV
    row_spec = lambda w: pl.BlockSpec((1, 1, w), lambda b, bits, pt: (b, 0, 0))
    const = lambda a: pl.BlockSpec(a.shape, lambda b, bits, pt: (0,) * a.ndim)
    page_specs = [pl.BlockSpec((1, rows_pp, HEAD_DIM), lambda b, bits, pt, p=p: (pt[b, p], 0, 0)) for p in range(n_pages)]
    grid_spec = pltpu.PrefetchScalarGridSpec(
        num_scalar_prefetch=2,
        grid=(nb,),
        in_specs=(page_specs * 2
                  + [pl.BlockSpec((1, k_win.shape[1], HEAD_DIM), lambda b, bits, pt: (b, 0, 0))] * 2
                  + [row_spec(N_KV * HEAD_DIM)] * 4
                  + [row_spec(proj.shape[-1]), row_spec(gates.shape[-1]),
                     pl.BlockSpec((1, N_HEADS, HEAD_DIM), lambda b, bits, pt: (b, 0, 0)),
                     const(ts), const(tw)]),
        out_specs=pl.BlockSpec((1, N_HEADS, HEAD_DIM), lambda b, bits, pt: (b, 0, 0)),
    )
    return pl.pallas_call(
        functools.partial(_attn_sample_kernel, n_pages=n_pages),
        grid_spec=grid_spec,
        out_shape=jax.ShapeDtypeStruct((nb, N_HEADS, HEAD_DIM), F32),
        compiler_params=_params("parallel"),
        name="nsa_attention_sample",
    )(bits, page_table, *([cache_k] * n_pages), *([cache_v] * n_pages), k_win, v_win, *new_rows,
      proj, gates, o_cmp, ts, tw)


def _overlap(n_cmp, n_slc):
    cs = np.arange(n_cmp) * CMP_STRIDE
    bs = np.arange(n_slc) * SLC_BLOCK
    ov = np.clip(np.minimum(cs[:, None] + CMP_LEN, bs[None, :] + SLC_BLOCK) - np.maximum(cs[:, None], bs[None, :]), 0, None)
    return (ov / CMP_LEN).astype(np.float32)


def _compress_weights(w1, w2, pe):
    half = CMP_STRIDE * HEAD_DIM
    wcat = jnp.concatenate([w1[:half], w1[half:]], axis=1).astype(BF16)
    pe8 = jnp.pad(pe.reshape(2, half), ((0, SUBLANES - 2), (0, 0)))
    return wcat, w2.astype(BF16), pe8


def _group_rows(tbl):
    l = tbl.shape[-1]
    return jnp.pad(tbl.reshape(N_KV, GROUP_REP, l), ((0, 0), (0, SUBLANES - GROUP_REP), (0, 0)))


def kernel(x_prompt, x_sample, state_pool, cache_k_cmp, cache_v_cmp, cache_k_slc, cache_v_slc, state_k_win, state_v_win, page_table, c_prompt, c_sample, g_norm, w_ada, b_ada, w_in_a, w_grp, pool_scale, w_out_a, g_kv, w_kv, pe_k, w_ck1, w_ck2, pe_v, w_cv1, w_cv2, rel_bias, w_in_b, w_out_b, g_final):
    bsz, t, d = x_prompt.shape
    nb = x_sample.shape[0]
    n_pages = page_table.shape[1]
    past = n_pages * PAGE_SIZE
    d_att = N_HEADS * HEAD_DIM
    d_kv = N_KV * HEAD_DIM
    n_a = w_in_a.shape[0]
    assert n_a == 1 and w_in_b.shape[0] == 1 and x_sample.shape[1] == 1
    assert t % TQ == 0 and t >= WINDOW and past % PAGE_SIZE == 0 and state_k_win.shape[1] == WINDOW

    w_in_a16 = w_in_a[0].astype(BF16)
    w_grp16 = w_grp[0].astype(BF16)
    w_out_a16 = w_out_a[0].astype(BF16)
    w_kv16 = w_kv.astype(BF16)
    w_in_b16 = w_in_b[0].astype(BF16)
    w_out_b16 = w_out_b[0].astype(BF16)
    wg = w_in_b[0][:, (1 + N_BRANCH) * d_att:].reshape(d, N_BRANCH, N_KV, GROUP_REP).transpose(0, 2, 1, 3)
    wg = jnp.pad(wg.reshape(d, N_KV, N_BRANCH * GROUP_REP), ((0, 0), (0, 0), (0, LANES - N_BRANCH * GROUP_REP)))
    w_gate16 = wg.reshape(d, N_KV * LANES).astype(BF16)
    cw_k = _compress_weights(w_ck1, w_ck2, pe_k)
    cw_v = _compress_weights(w_cv1, w_cv2, pe_v)
    cw = cw_k + cw_v

    mod = _ada(jnp.concatenate([c_prompt, c_sample], axis=0), w_ada, b_ada)

    def modulation(l, lo, hi, per_row):
        parts = [mod[l, lo:hi, k * d:(k + 1) * d] for k in range(3)]
        return [p[None] if per_row else p[:, None] for p in parts]

    t0, t1, tc, ts, tw, tcs = _bias_tables(rel_bias, past=past, n_cmp_s=past // CMP_STRIDE)
    ts, tw, tcs = ts.reshape(N_HEADS, -1), tw.reshape(N_HEADS, -1), _group_rows(tcs)

    shift, scale, gate = modulation(0, 0, bsz, False)
    uz = _nm_matmul(x_prompt, g_norm[0], scale, shift, w_in_a16, n_cols=2 * d, tm=TM_PROJ, tn=512, name="in_proj_pool_prompt")
    pool_p = uz[:, t - POOL_BUF:, :d][None]
    y0 = _pool_prompt(uz, w_grp16, pool_scale[0], tm=256)
    x1 = _proj_residual(y0, w_out_a16, x_prompt, gate, g_final, tm=512, final_norm=False, name="out_proj_pool_prompt")
    kv_p = _kv_proj(x1, g_kv, w_kv16, tm=256, name="kv_proj_prompt")
    heads = lambda a: a.reshape(a.shape[0], a.shape[1], N_KV, HEAD_DIM)
    kv_state_p = [heads(a) for a in kv_p[:4]] + [heads(a[:, t - WINDOW:]) for a in kv_p[4:6]]
    kc_p, vc_p = _compress_prompt(kv_state_p[0].reshape(bsz, t * N_KV, HEAD_DIM),
                                  kv_state_p[1].reshape(bsz, t * N_KV, HEAD_DIM), cw)
    shift, scale, gate = modulation(1, 0, bsz, False)
    proj = _nm_matmul(x1, g_norm[1], scale, shift, w_in_b16, n_cols=(1 + N_BRANCH) * d_att, tm=TM_PROJ, tn=512,
                      name="in_proj_nsa_prompt")
    gates = _nm_matmul(x1, g_norm[1], scale, shift, w_gate16, n_cols=N_KV * LANES, tm=512, tn=N_KV * LANES, sigmoid=True,
                       name="gate_proj_prompt")
    ovl_t = jnp.asarray(np.pad(_overlap(t // CMP_STRIDE - 1, t // SLC_BLOCK), ((0, 1), (0, 0))).T).astype(BF16)
    flat = lambda a: a.reshape(bsz * t, a.shape[-1])
    y1 = _attn_prompt(flat(proj), flat(gates), kc_p, vc_p, kv_p[6], flat(kv_p[3]), kv_p[7], flat(kv_p[5]),
                      t0, t1, tc, ovl_t, rel_bias, bsz=bsz, t=t)
    y_prompt = _proj_residual(y1.reshape(bsz, t, d_att), w_out_b16, x1, gate, g_final, tm=512, final_norm=True,
                              name="out_proj_nsa_prompt")

    xs = x_sample.reshape(1, nb, d)
    shift, scale, gate = modulation(0, bsz, bsz + nb, True)
    uz_s = _nm_matmul(xs, g_norm[0], scale, shift, w_in_a16, n_cols=2 * d, tm=nb, tn=512, name="in_proj_pool_sample")
    pool_s = jnp.concatenate([state_pool[:, :, 1:], uz_s[0, :, None, :d][None]], axis=2)
    y0_s = _pool_sample(uz_s, state_pool[0], w_grp16, pool_scale[0], qpos=past, tb=min(32, nb))
    x1_s = _proj_residual(y0_s, w_out_a16, xs, gate, g_final, tm=nb, final_norm=False, name="out_proj_pool_sample")
    kv_s = _kv_proj(x1_s, g_kv, w_kv16, tm=nb, name="kv_proj_sample")
    new_rows = [a.reshape(nb, 1, d_kv) for a in kv_s[:2 * N_BRANCH]]
    shift, scale, gate = modulation(1, bsz, bsz + nb, True)
    proj_s = _nm_matmul(x1_s, g_norm[1], scale, shift, w_in_b16, n_cols=(1 + N_BRANCH) * d_att, tm=nb, tn=512,
                        name="in_proj_nsa_sample").reshape(nb, 1, (1 + N_BRANCH) * d_att)
    gates_s = _nm_matmul(x1_s, g_norm[1], scale, shift, w_gate16, n_cols=N_KV * LANES, tm=nb, tn=N_KV * LANES, sigmoid=True,
                         name="gate_proj_sample").reshape(nb, 1, N_KV * LANES)
    n_cmp_s = past // CMP_STRIDE
    ovl_s = jnp.asarray(np.pad(_overlap(n_cmp_s, past // SLC_BLOCK + 1), ((0, 0), (0, LANES - past // SLC_BLOCK - 1)))).astype(BF16)
    rows_of = lambda c: c.reshape(c.shape[0], PAGE_SIZE * N_KV, HEAD_DIM)
    o_cmp, bits = _compress_sample(rows_of(cache_k_cmp), rows_of(cache_v_cmp), page_table, new_rows[0], new_rows[1], cw,
                                   proj_s, tcs, ovl_s, past=past)
    y1_s = _attn_sample(rows_of(cache_k_slc), rows_of(cache_v_slc), page_table, bits[:, :N_KV, 0],
                        state_k_win.reshape(nb, WINDOW * N_KV, HEAD_DIM), state_v_win.reshape(nb, WINDOW * N_KV, HEAD_DIM),
                        new_rows[2:], proj_s, gates_s, o_cmp, ts, tw)
    y_sample = _proj_residual(y1_s.reshape(1, nb, d_att), w_out_b16, x1_s, gate, g_final, tm=nb, final_norm=True,
                              name="out_proj_nsa_sample").reshape(nb, 1, d)
    new4 = [a.reshape(nb, 1, N_KV, HEAD_DIM) for a in kv_s[:2 * N_BRANCH]]
    kv_state_s = new4[:4] + [jnp.concatenate([state_k_win[:, 1:], new4[4]], axis=1),
                             jnp.concatenate([state_v_win[:, 1:], new4[5]], axis=1)]

    return (y_prompt, y_sample, pool_p, *kv_state_p, pool_s, *kv_state_s)
```

```python
import functools
import math

import numpy as np
import jax
import jax.numpy as jnp
from jax import lax
from jax.experimental import pallas as pl
from jax.experimental.pallas import tpu as pltpu

F32 = jnp.float32
BF16 = jnp.bfloat16

PAGE_SIZE = 128
POOL_WINDOWS = (2, 4, 8, 16)
POOL_BUF = max(POOL_WINDOWS) - 1
HEAD_DIM = 128
N_KV = 4
GROUP_REP = 4
N_HEADS = N_KV * GROUP_REP
N_BRANCH = 3
CMP_LEN = 32
CMP_STRIDE = 16
SLC_BLOCK = 64
N_SEL = 8
WINDOW = 512
N_BUCKETS = 32
MAX_DISTANCE = 128
RMS_EPS = 1e-6
SCALE = HEAD_DIM ** -0.5
LOG2E = math.log2(math.e)
NEG_INF = -1e30
FORCE_SCORE = 1e9

LANES = 128
SUBLANES = 8
VMEM_LIMIT_BYTES = 56 * 1024 * 1024
TQ = 256
ACC_ROWS = HEAD_DIM + 16
TN_PROJ = 1024
TM_PROJ = 1024
HALO = 16


def _bucket_starts():
    max_exact = N_BUCKETS // 2
    d = np.arange(0, MAX_DISTANCE + 1)
    large = max_exact + np.floor(
        np.log(np.maximum(d, max_exact) / max_exact) / math.log(MAX_DISTANCE / max_exact) * (N_BUCKETS - max_exact)
    ).astype(np.int64)
    bucket = np.where(d < max_exact, d, np.minimum(large, N_BUCKETS - 1))
    return [int(np.argmax(bucket >= k)) for k in range(N_BUCKETS)]


BUCKET_STARTS = _bucket_starts()
FAR_DIST = BUCKET_STARTS[-1]


def _params(*sem):
    return pltpu.CompilerParams(dimension_semantics=sem, vmem_limit_bytes=VMEM_LIMIT_BYTES)


def _silu(x):
    return x * jax.nn.sigmoid(x)


def _rms(x, g):
    return x * lax.rsqrt(jnp.mean(x * x, axis=-1, keepdims=True) + RMS_EPS) * g


def _dot(a, b):
    return jnp.dot(a, b, preferred_element_type=F32)


def _dot_nt(a, b):
    return lax.dot_general(a, b, (((1,), (1,)), ((), ())), preferred_element_type=F32)


def _ada_kernel(c_ref, w_ref, b_ref, o_ref):
    a = _silu(c_ref[...]).astype(BF16)
    o_ref[0] = _dot(a, w_ref[0].astype(BF16)) + b_ref[0]


def _ada(c_all, w_ada, b_ada):
    depth, d, n = w_ada.shape
    m = c_all.shape[0]
    tn = 512
    return pl.pallas_call(
        _ada_kernel,
        grid=(depth, n // tn),
        in_specs=[
            pl.BlockSpec((m, d), lambda l, j: (0, 0)),
            pl.BlockSpec((1, d, tn), lambda l, j: (l, 0, j)),
            pl.BlockSpec((1, 1, tn), lambda l, j: (l, 0, j)),
        ],
        out_specs=pl.BlockSpec((1, m, tn), lambda l, j: (l, 0, j)),
        out_shape=jax.ShapeDtypeStruct((depth, m, n), F32),
        compiler_params=_params("parallel", "parallel"),
        name="ada_modulation",
    )(c_all, w_ada, b_ada.reshape(depth, 1, n))


def _nm_matmul_kernel(x_ref, g_ref, sc_ref, sh_ref, w_ref, o_ref, h_scr, *, silu_from, sigmoid_from):
    j = pl.program_id(2)

    @pl.when(j == 0)
    def _():
        y = _rms(x_ref[0], g_ref[...])
        h_scr[...] = (y * (1.0 + sc_ref[0]) + sh_ref[0]).astype(BF16)

    r = _dot(h_scr[...], w_ref[...])
    if silu_from is None:
        o_ref[0] = r
    else:
        sig = jax.nn.sigmoid(r)
        o_ref[0] = jnp.where(j >= sigmoid_from, sig, jnp.where(j >= silu_from, r * sig, r))


def _mod_spec(arr, tm):
    d = arr.shape[-1]
    if arr.shape[1] == 1:
        return pl.BlockSpec((1, 1, d), lambda b, i, *_: (b, 0, 0))
    return pl.BlockSpec((1, tm, d), lambda b, i, *_: (b, i, 0))


def _nm_matmul(x, g, scale, shift, w, *, n_cols, tm, tn, silu_cols=None, sigmoid_cols=None, name):
    bsz, rows, d = x.shape
    if silu_cols is None and sigmoid_cols is None:
        silu_from = sigmoid_from = None
    else:
        sigmoid_from = n_cols // tn if sigmoid_cols is None else sigmoid_cols // tn
        silu_from = sigmoid_from if silu_cols is None else silu_cols // tn
    return pl.pallas_call(
        functools.partial(_nm_matmul_kernel, silu_from=silu_from, sigmoid_from=sigmoid_from),
        grid=(bsz, rows // tm, n_cols // tn),
        in_specs=[
            pl.BlockSpec((1, tm, d), lambda b, i, j: (b, i, 0)),
            pl.BlockSpec((1, d), lambda b, i, j: (0, 0)),
            _mod_spec(scale, tm),
            _mod_spec(shift, tm),
            pl.BlockSpec((d, tn), lambda b, i, j: (0, j)),
        ],
        out_specs=pl.BlockSpec((1, tm, tn), lambda b, i, j: (b, i, j)),
        out_shape=jax.ShapeDtypeStruct((bsz, rows, n_cols), F32),
        scratch_shapes=[pltpu.VMEM((tm, d), BF16)],
        compiler_params=_params("parallel", "parallel", "arbitrary"),
        name=name,
    )(x, g.reshape(1, d), scale, shift, w)


def _pool_mix(pooled_fn, z, wg_ref, ps_ref, y_ref):
    grp = z.shape[-1] // len(POOL_WINDOWS)
    for gi, w in enumerate(POOL_WINDOWS):
        cs = slice(gi * grp, (gi + 1) * grp)
        mixed = _dot(pooled_fn(gi, w, cs).astype(BF16), wg_ref[gi])
        y_ref[0, :, cs] = mixed * ps_ref[:, cs] * _silu(z[:, cs])


def _pool_prompt_kernel(u_ref, z_ref, halo_ref, wg_ref, ps_ref, y_ref, ue_scr):
    i = pl.program_id(1)
    tm = u_ref.shape[1]
    u = u_ref[0]
    ue_scr[0:HALO, :] = jnp.where(i > 0, halo_ref[0], 0.0)
    ue_scr[HALO:HALO + tm, :] = u
    pos = i * tm + lax.broadcasted_iota(jnp.int32, (tm, 1), 0)

    def pooled(gi, w, cs):
        acc = u[:, cs]
        for k in range(1, w):
            acc = acc + ue_scr[HALO - k:HALO - k + tm, cs]
        cnt = jnp.minimum(pos + 1, w).astype(F32)
        return acc / cnt - u[:, cs]

    _pool_mix(pooled, z_ref[0], wg_ref, ps_ref, y_ref)


def _pool_prompt(uz, w_grp, pool_scale, *, tm):
    bsz, t, d2 = uz.shape
    d = d2 // 2
    ng, grp, _ = w_grp.shape
    return pl.pallas_call(
        _pool_prompt_kernel,
        grid=(bsz, t // tm),
        in_specs=[
            pl.BlockSpec((1, tm, d), lambda b, i: (b, i, 0)),
            pl.BlockSpec((1, tm, d), lambda b, i: (b, i, 1)),
            pl.BlockSpec((1, HALO, d), lambda b, i: (b, jnp.maximum(i * (tm // HALO) - 1, 0), 0)),
            pl.BlockSpec((ng, grp, grp), lambda b, i: (0, 0, 0)),
            pl.BlockSpec((1, d), lambda b, i: (0, 0)),
        ],
        out_specs=pl.BlockSpec((1, tm, d), lambda b, i: (b, i, 0)),
        out_shape=jax.ShapeDtypeStruct((bsz, t, d), F32),
        scratch_shapes=[pltpu.VMEM((HALO + tm, d), F32)],
        compiler_params=_params("parallel", "parallel"),
        name="pool_mixer_prompt",
    )(uz, uz, uz, w_grp, pool_scale.reshape(1, d))


def _pool_sample_kernel(u_ref, z_ref, pre_ref, wg_ref, ps_ref, y_ref, *, qpos):
    u = u_ref[0]

    def pooled(gi, w, cs):
        acc = u[:, cs]
        for k in range(1, w):
            acc = acc + pre_ref[:, POOL_BUF - k, cs]
        return acc / float(min(qpos + 1, w)) - u[:, cs]

    _pool_mix(pooled, z_ref[0], wg_ref, ps_ref, y_ref)


def _pool_sample(uz, prefix, w_grp, pool_scale, *, qpos, tb):
    _, nb, d2 = uz.shape
    d = d2 // 2
    ng, grp, _ = w_grp.shape
    return pl.pallas_call(
        functools.partial(_pool_sample_kernel, qpos=qpos),
        grid=(nb // tb,),
        in_specs=[
            pl.BlockSpec((1, tb, d), lambda i: (0, i, 0)),
            pl.BlockSpec((1, tb, d), lambda i: (0, i, 1)),
            pl.BlockSpec((tb, POOL_BUF, d), lambda i: (i, 0, 0)),
            pl.BlockSpec((ng, grp, grp), lambda i: (0, 0, 0)),
            pl.BlockSpec((1, d), lambda i: (0, 0)),
        ],
        out_specs=pl.BlockSpec((1, tb, d), lambda i: (0, i, 0)),
        out_shape=jax.ShapeDtypeStruct((1, nb, d), F32),
        compiler_params=_params("parallel"),
        name="pool_mixer_sample",
    )(uz, uz, prefix, w_grp, pool_scale.reshape(1, d))


def _proj_residual_kernel(y_ref, w_ref, x_ref, gate_ref, g_ref, o_ref, *, final_norm):
    xo = x_ref[0] + gate_ref[0] * _dot(y_ref[0].astype(BF16), w_ref[...])
    o_ref[0] = _rms(xo, g_ref[...]) if final_norm else xo


def _proj_residual(y, w, x, gate, g_final, *, tm, final_norm, name):
    bsz, rows, d = x.shape
    dy = y.shape[-1]
    return pl.pallas_call(
        functools.partial(_proj_residual_kernel, final_norm=final_norm),
        grid=(bsz, rows // tm),
        in_specs=[
            pl.BlockSpec((1, tm, dy), lambda b, i: (b, i, 0)),
            pl.BlockSpec((dy, d), lambda b, i: (0, 0)),
            pl.BlockSpec((1, tm, d), lambda b, i: (b, i, 0)),
            _mod_spec(gate, tm),
            pl.BlockSpec((1, d), lambda b, i: (0, 0)),
        ],
        out_specs=pl.BlockSpec((1, tm, d), lambda b, i: (b, i, 0)),
        out_shape=jax.ShapeDtypeStruct((bsz, rows, d), F32),
        compiler_params=_params("parallel", "parallel"),
        name=name,
    )(y, w, x, gate, g_final.reshape(1, d))


KV_BF16_COPIES = (2, 4)


def _kv_kernel(x_ref, g_ref, w_ref, *o_refs):
    h = _rms(x_ref[0], g_ref[...]).astype(BF16)
    n_out = 2 * N_BRANCH
    n = o_refs[0].shape[-1]
    for o in range(n_out):
        r = _dot(h, w_ref[:, o * n:(o + 1) * n])
        o_refs[o][0] = r
        if o in KV_BF16_COPIES:
            c_ref = o_refs[n_out + KV_BF16_COPIES.index(o)]
            for g in range(N_KV):
                c_ref[0, g] = r[:, g * HEAD_DIM:(g + 1) * HEAD_DIM].astype(BF16)


def _kv_proj(x, g_kv, w_kv, *, tm, name):
    bsz, rows, d = x.shape
    n_out = 2 * N_BRANCH
    n = w_kv.shape[1] // n_out
    n_cp = len(KV_BF16_COPIES)
    return pl.pallas_call(
        _kv_kernel,
        grid=(bsz, rows // tm),
        in_specs=[
            pl.BlockSpec((1, tm, d), lambda b, i: (b, i, 0)),
            pl.BlockSpec((1, d), lambda b, i: (0, 0)),
            pl.BlockSpec((d, n_out * n), lambda b, i: (0, 0)),
        ],
        out_specs=([pl.BlockSpec((1, tm, n), lambda b, i: (b, i, 0))] * n_out
                   + [pl.BlockSpec((1, N_KV, tm, HEAD_DIM), lambda b, i: (b, 0, i, 0))] * n_cp),
        out_shape=([jax.ShapeDtypeStruct((bsz, rows, n), F32)] * n_out
                   + [jax.ShapeDtypeStruct((bsz, N_KV, rows, HEAD_DIM), BF16)] * n_cp),
        compiler_params=_params("parallel", "parallel"),
        name=name,
    )(x, g_kv.reshape(1, d), w_kv)


def _bias_lookup(rb_ref, h, dist):
    acc = jnp.full(dist.shape, rb_ref[N_BUCKETS - 1, h], F32)
    for k in range(N_BUCKETS - 2, -1, -1):
        acc = jnp.where(dist < BUCKET_STARTS[k + 1], rb_ref[k, h], acc)
    return acc


def _bias_tables_kernel(rb_ref, t0_ref, t1_ref, tc_ref, ts_ref, tw_ref, tcs_ref, *, past):
    h = pl.program_id(0)
    key = lax.broadcasted_iota(jnp.int32, (TQ, TQ), 0)
    qry = lax.broadcasted_iota(jnp.int32, (TQ, TQ), 1)
    far = rb_ref[N_BUCKETS - 1, h]
    rel2 = lambda dist: (_bias_lookup(rb_ref, h, dist) - far) * LOG2E
    t0_ref[0] = rel2(qry - key)
    t1_ref[0] = rel2(TQ + qry - key)
    tc_ref[0] = rel2(qry - CMP_STRIDE * (key - TQ // 2) - (CMP_LEN - 1))
    ls = lax.broadcasted_iota(jnp.int32, ts_ref.shape[1:], 1)
    own = (ls % N_KV) == h // GROUP_REP
    tok = ls // N_KV
    ts_ref[0] = jnp.where(ls == N_KV * past, rb_ref[0, h],
                          jnp.where(own & (tok < past), _bias_lookup(rb_ref, h, past - tok), NEG_INF))
    lw = lax.broadcasted_iota(jnp.int32, tw_ref.shape[1:], 1)
    own = (lw % N_KV) == h // GROUP_REP
    dw = WINDOW - lw // N_KV
    tw_ref[0] = jnp.where(lw == N_KV * WINDOW, rb_ref[0, h],
                          jnp.where(own & (dw > 0) & (dw < WINDOW) & (past - dw >= 0), _bias_lookup(rb_ref, h, dw), NEG_INF))
    lc = lax.broadcasted_iota(jnp.int32, tcs_ref.shape[1:], 1)
    n_cmp = tcs_ref.shape[2] // N_KV
    dc = past - CMP_STRIDE * (lc % n_cmp) - (CMP_LEN - 1)
    tcs_ref[0] = jnp.where((lc // n_cmp == h // GROUP_REP) & (dc >= 0), _bias_lookup(rb_ref, h, dc), NEG_INF)


def _bias_tables(rel_bias, *, past, n_cmp_s):
    nh = rel_bias.shape[1]
    ls = N_KV * past + LANES
    lw = N_KV * WINDOW + LANES
    shapes = [(nh, TQ, TQ)] * 3 + [(nh, 1, ls), (nh, 1, lw), (nh, 1, N_KV * n_cmp_s)]
    return pl.pallas_call(
        functools.partial(_bias_tables_kernel, past=past),
        grid=(nh,),
        in_specs=[pl.BlockSpec(memory_space=pltpu.SMEM)],
        out_specs=[pl.BlockSpec((1,) + s[1:], lambda h: (h, 0, 0)) for s in shapes],
        out_shape=[jax.ShapeDtypeStruct(s, F32) for s in shapes],
        compiler_params=_params("parallel"),
        name="rel_bias_tables",
    )(rel_bias)


def _compress_fill(page_refs, lhs_scr):
    sub_per_page = PAGE_SIZE // CMP_STRIDE
    pairs = CMP_STRIDE // 2
    rows_pp = sub_per_page * SUBLANES
    even = (lax.broadcasted_iota(jnp.int32, (rows_pp, 1), 0) & (SUBLANES - 1)) < N_KV
    for p, pref in enumerate(page_refs):
        x4 = pref[0].reshape(sub_per_page, pairs, SUBLANES, HEAD_DIM)
        rs = slice(p * rows_pp, (p + 1) * rows_pp)
        for j in range(pairs):
            t = x4[:, j].reshape(rows_pp, HEAD_DIM)
            lhs_scr[rs, 2 * j * HEAD_DIM:(2 * j + 1) * HEAD_DIM] = jnp.where(even, t, 0.0).astype(BF16)
            lhs_scr[rs, (2 * j + 1) * HEAD_DIM:(2 * j + 2) * HEAD_DIM] = jnp.where(even, 0.0, t).astype(BF16)


def _compress_finish(acc, new_ref, wcat_ref, w2_ref, pe_ref, t_scr, p_scr, *, has_new):
    n_rows = acc.shape[0]
    n_sub = n_rows // SUBLANES
    wcat = wcat_ref[...]
    t_scr[0:n_rows, :] = acc
    t_scr[n_rows:, :] = jnp.zeros((2 * SUBLANES, 2 * HEAD_DIM), F32)
    t_scr[0:n_rows, :] = t_scr[0:n_rows, :] + t_scr[N_KV:n_rows + N_KV, :]
    if has_new:
        new8 = jnp.concatenate([new_ref[0, :, g * HEAD_DIM:(g + 1) * HEAD_DIM] for g in range(N_KV)]
                               + [jnp.zeros((SUBLANES - N_KV, HEAD_DIM), F32)], axis=0)
        t_scr[n_rows:n_rows + SUBLANES, HEAD_DIM:] = _dot(new8.astype(BF16), wcat[0:HEAD_DIM, HEAD_DIM:])
    pe_r = _dot(pe_ref[...].astype(BF16), wcat)
    pe_const = pe_r[0:1, :HEAD_DIM] + pe_r[1:2, HEAD_DIM:]
    p_scr[...] = t_scr[0:n_rows, :HEAD_DIM] + t_scr[SUBLANES:n_rows + SUBLANES, HEAD_DIM:] + pe_const
    pre = jnp.concatenate([p_scr[pl.ds(g, n_sub, stride=SUBLANES), :] for g in range(N_KV)], axis=0)
    out = _dot(_silu(pre).astype(BF16), w2_ref[...])
    if not has_new:
        blk = lax.broadcasted_iota(jnp.int32, (N_KV * n_sub, 1), 0) & (n_sub - 1)
        out = jnp.where(blk == n_sub - 1, 0.0, out)
    return out


def _compress_kv(k_pages, v_pages, k_new, v_new, k_w, v_w, scr, *, has_new):
    _compress_fill(k_pages, scr[0])
    _compress_fill(v_pages, scr[3])
    acc_k = _dot(scr[0][...], k_w[0][...])
    acc_v = _dot(scr[3][...], v_w[0][...])
    kcs = _compress_finish(acc_k, k_new, *k_w, scr[1], scr[2], has_new=has_new)
    vcs = _compress_finish(acc_v, v_new, *v_w, scr[4], scr[5], has_new=has_new)
    return kcs, vcs


def _compress_scratch(n_pages):
    n_rows = n_pages * (PAGE_SIZE // CMP_STRIDE) * SUBLANES
    return [pltpu.VMEM((n_rows, CMP_STRIDE * HEAD_DIM), BF16),
            pltpu.VMEM((n_rows + 2 * SUBLANES, 2 * HEAD_DIM), F32),
            pltpu.VMEM((n_rows, HEAD_DIM), F32)]


def _compress_prompt_kernel(pt_ref, *refs, n_pages):
    kp, vp = refs[:n_pages], refs[n_pages:2 * n_pages]
    wck, w2k, pek, wcv, w2v, pev, kc_ref, vc_ref, *scr = refs[2 * n_pages:]
    n_sub = n_pages * (PAGE_SIZE // CMP_STRIDE)
    assert n_sub & (n_sub - 1) == 0
    kc_ref[0], vc_ref[0] = _compress_kv(kp, vp, None, None, (wck, w2k, pek), (wcv, w2v, pev), scr, has_new=False)


def _page_specs(n_pages):
    return [pl.BlockSpec((1, PAGE_SIZE * N_KV, HEAD_DIM), lambda b, pt, p=p: (pt[b, p], 0, 0)) for p in range(n_pages)]


def _const_spec(shape):
    return pl.BlockSpec(shape, lambda b, pt: (0,) * len(shape))


def _compress_prompt(k_rows, v_rows, cw):
    bsz = k_rows.shape[0]
    t = k_rows.shape[1] // N_KV
    n_pages = t // PAGE_SIZE
    n_sub = t // CMP_STRIDE
    pt = jnp.arange(bsz * n_pages, dtype=jnp.int32).reshape(bsz, n_pages)
    kp = k_rows.reshape(bsz * n_pages, PAGE_SIZE * N_KV, HEAD_DIM)
    vp = v_rows.reshape(bsz * n_pages, PAGE_SIZE * N_KV, HEAD_DIM)
    wspecs = [_const_spec(a.shape) for a in cw]
    grid_spec = pltpu.PrefetchScalarGridSpec(
        num_scalar_prefetch=1,
        grid=(bsz,),
        in_specs=_page_specs(n_pages) * 2 + wspecs,
        out_specs=[pl.BlockSpec((1, N_KV * n_sub, HEAD_DIM), lambda b, pt: (b, 0, 0))] * 2,
        scratch_shapes=_compress_scratch(n_pages) * 2,
    )
    return pl.pallas_call(
        functools.partial(_compress_prompt_kernel, n_pages=n_pages),
        grid_spec=grid_spec,
        out_shape=[jax.ShapeDtypeStruct((bsz, N_KV * n_sub, HEAD_DIM), F32)] * 2,
        compiler_params=_params("parallel"),
        name="compress_prompt",
    )(pt, *([kp] * n_pages), *([vp] * n_pages), *cw)


def _head_rows(ref, base):
    return jnp.concatenate([ref[0, :, base + h * HEAD_DIM:base + (h + 1) * HEAD_DIM] for h in range(N_HEADS)], axis=0)


def _split_dot(a, b):
    hi = a.astype(BF16)
    lo = (a - hi.astype(F32)).astype(BF16)
    return _dot(hi, b) + _dot(lo, b)


def _compress_sample_kernel(pt_ref, *refs, n_pages, cur_blk):
    kp, vp = refs[:n_pages], refs[n_pages:2 * n_pages]
    (newk, newv, wck, w2k, pek, wcv, w2v, pev, q_ref, tcs_ref, ovl_ref, pow_ref,
     oc_ref, bits_ref, *scr) = refs[2 * n_pages:]
    kc, vc = _compress_kv(kp, vp, newk, newv, (wck, w2k, pek), (wcv, w2v, pev), scr, has_new=True)
    q16 = (_head_rows(q_ref, 0) * SCALE).astype(BF16)
    s = _dot_nt(q16, kc.astype(BF16)) + tcs_ref[...]
    e = jnp.exp(s - jnp.max(s, axis=-1, keepdims=True))
    p = e / jnp.sum(e, axis=-1, keepdims=True)
    oc_ref[0] = _dot(p.astype(BF16), vc.astype(BF16))
    imp_h = _split_dot(p, ovl_ref[...])
    imp = jnp.concatenate([jnp.sum(imp_h[GROUP_REP * g:GROUP_REP * (g + 1)], axis=0, keepdims=True)
                           for g in range(N_KV)], axis=0)
    lane = lax.broadcasted_iota(jnp.int32, (1, LANES), 1)
    forced = (lane == cur_blk) | (lane == 0)
    imp = jnp.where(forced, FORCE_SCORE, jnp.where(lane <= cur_blk, imp, -FORCE_SCORE))
    rank = jnp.zeros((N_KV, LANES), F32)
    for k in range(cur_blk + 1):
        ck = imp[:, k:k + 1]
        rank = rank + jnp.where(ck > imp, 1.0, jnp.where((ck == imp) & (lane > k), 1.0, 0.0))
    sel = jnp.where(rank < N_SEL, 1.0, 0.0) * pow_ref[...]
    lo = jnp.sum(jnp.where(lane < 16, sel, 0.0), axis=-1, keepdims=True).astype(jnp.int32)
    hi = jnp.sum(jnp.where((lane >= 16) & (lane < 32), sel, 0.0), axis=-1, keepdims=True).astype(jnp.int32)
    bits_ref[0] = jnp.broadcast_to(lo | (hi << 16), (N_KV, LANES))


def _compress_sample(cache_k, cache_v, page_table, new_k, new_v, cw, q, tcs, ovl, *, past):
    nb, n_pages = page_table.shape
    width = new_k.shape[-1]
    n_sub = n_pages * (PAGE_SIZE // CMP_STRIDE)
    pow2 = np.zeros((1, LANES), np.float32)
    pow2[0, :32] = 2.0 ** (np.arange(32) % 16)
    row_spec = pl.BlockSpec((1, 1, width), lambda b, pt: (b, 0, 0))
    grid_spec = pltpu.PrefetchScalarGridSpec(
        num_scalar_prefetch=1,
        grid=(nb,),
        in_specs=(_page_specs(n_pages) * 2 + [row_spec, row_spec] + [_const_spec(a.shape) for a in cw]
                  + [pl.BlockSpec((1, 1, q.shape[-1]), lambda b, pt: (b, 0, 0)),
                     _const_spec(tcs.shape), _const_spec(ovl.shape), _const_spec(pow2.shape)]),
        out_specs=[pl.BlockSpec((1, N_HEADS, HEAD_DIM), lambda b, pt: (b, 0, 0)),
                   pl.BlockSpec((1, N_KV, LANES), lambda b, pt: (b, 0, 0))],
        scratch_shapes=_compress_scratch(n_pages) * 2,
    )
    return pl.pallas_call(
        functools.partial(_compress_sample_kernel, n_pages=n_pages, cur_blk=past // SLC_BLOCK),
        grid_spec=grid_spec,
        out_shape=[jax.ShapeDtypeStruct((nb, N_HEADS, HEAD_DIM), F32),
                   jax.ShapeDtypeStruct((nb, N_KV, LANES), jnp.int32)],
        compiler_params=_params("parallel"),
        name="compress_select_sample",
    )(page_table, *([cache_k] * n_pages), *([cache_v] * n_pages), new_k, new_v, *cw, q, tcs, ovl, jnp.asarray(pow2))


def _attn_prompt_kernel(q_ref, z0_ref, z1_ref, z2_ref, gt_ref, kc_ref, vc_ref, ks_ref, vs_ref, kw_ref, vw_ref,
                        t0_ref, t1_ref, tc_ref, ovl_ref, y_ref, m_scr, acc_scr, sel_scr):
    qt = pl.program_id(2)
    q = q_ref[...]
    qf = jnp.concatenate([q[:, r * HEAD_DIM:(r + 1) * HEAD_DIM] for r in range(GROUP_REP)], axis=0)
    qf = (qf * (SCALE * LOG2E)).astype(BF16)
    lane_q = lax.broadcasted_iota(jnp.int32, (1, TQ), 1)
    qpos = qt * TQ + lane_q
    key_l = lax.broadcasted_iota(jnp.int32, (TQ, 1), 0)
    gate_t = gt_ref[...].T
    z_refs = (z0_ref, z1_ref, z2_ref)

    def emit(br, r, o_t, first):
        cs = slice(r * HEAD_DIM, (r + 1) * HEAD_DIM)
        term = o_t.T * z_refs[br][:, cs]
        if first:
            y_ref[:, cs] = term
        else:
            y_ref[:, cs] += term

    n_cmp = kc_ref.shape[1]
    s_all = _dot_nt(kc_ref[0].astype(BF16), qf)
    vc_t = vc_ref[0].T.astype(BF16)
    cmp_end = lax.broadcasted_iota(jnp.int32, (n_cmp, 1), 0) * CMP_STRIDE + (CMP_LEN - 1)
    vis = qpos >= cmp_end
    any_vis = qpos >= CMP_LEN - 1
    row0 = pl.multiple_of(TQ // 2 - (TQ // CMP_STRIDE) * qt, CMP_STRIDE)
    psum = jnp.zeros((n_cmp, TQ), F32)
    for r in range(GROUP_REP):
        s = s_all[:, r * TQ:(r + 1) * TQ] + tc_ref[r, pl.ds(row0, n_cmp), :]
        s = jnp.where(vis, s, NEG_INF)
        e = jnp.exp2(s - jnp.max(s, axis=0, keepdims=True))
        p = jnp.where(any_vis, e / jnp.sum(e, axis=0, keepdims=True), 0.0)
        psum = psum + p
        emit(0, r, _dot(vc_t, p.astype(BF16)) * gate_t[r:r + 1, :], True)

    n_slc = ovl_ref.shape[0]
    psum_hi = psum.astype(BF16)
    imp = _dot(ovl_ref[...], psum_hi) + _dot(ovl_ref[...], (psum - psum_hi.astype(F32)).astype(BF16))
    blk = lax.broadcasted_iota(jnp.int32, (n_slc, 1), 0)
    cur = lax.shift_right_logical(qpos, int(math.log2(SLC_BLOCK)))
    forced = (blk == cur) | (blk == 0)
    imp = jnp.where(forced, FORCE_SCORE, jnp.where(blk <= cur, imp, -FORCE_SCORE))
    rank = jnp.zeros((n_slc, TQ), F32)
    for k in range(n_slc):
        rk = imp[k:k + 1, :]
        rank = rank + jnp.where(rk > imp, 1.0, jnp.where((rk == imp) & (blk > k), 1.0, 0.0))
    sel = jnp.where(rank < min(N_SEL, n_slc), 1.0, 0.0)
    blk_per_tile = TQ // SLC_BLOCK
    for t in range(n_slc // blk_per_tile):
        sel_scr[t, 0:blk_per_tile, :] = sel[t * blk_per_tile:(t + 1) * blk_per_tile, :]

    ones_rows = jnp.ones((ACC_ROWS - HEAD_DIM, TQ), BF16)

    def attend(st, k_ref, v_ref, tiles, use_sel, first):
        s_alls, v_ts, masks = [], [], []
        for kt, kind in tiles:
            start = pl.multiple_of(kt * TQ, TQ)
            s_alls.append(_dot_nt(k_ref[pl.ds(start, TQ), :], qf))
            v_t = v_ref[pl.ds(start, TQ), :].T.astype(BF16)
            v_ts.append(jnp.concatenate([v_t, ones_rows], axis=0))
            mask = None
            if kind == "diag":
                mask = key_l <= lane_q
            elif kind == "winfar":
                mask = key_l > lane_q
            if use_sel:
                sel4 = sel_scr[kt, 0:blk_per_tile, :]
                selm = jnp.concatenate(
                    [jnp.broadcast_to(sel4[j:j + 1, :], (SLC_BLOCK, TQ)) for j in range(blk_per_tile)], axis=0) > 0.5
                mask = selm if mask is None else mask & selm
            masks.append(mask)
        for r in range(GROUP_REP):
            sl = slice(r * TQ, (r + 1) * TQ)
            ss = []
            for (kt, kind), s_all, mask in zip(tiles, s_alls, masks):
                s = s_all[:, sl]
                if kind == "diag":
                    s = s + t0_ref[r]
                elif kind == "near":
                    s = s + t1_ref[r]
                if mask is not None:
                    s = jnp.where(mask, s, NEG_INF)
                ss.append(s)
            mx = jnp.max(ss[0], axis=0, keepdims=True)
            for s in ss[1:]:
                mx = jnp.maximum(mx, jnp.max(s, axis=0, keepdims=True))
            if first:
                m_new = mx
                upd = None
            else:
                m_old = m_scr[st, :, sl]
                m_new = jnp.maximum(m_old, mx)
                upd = jnp.exp2(m_old - m_new) * acc_scr[st, :, sl]
            for s, v_t in zip(ss, v_ts):
                pv = _dot(v_t, jnp.exp2(s - m_new).astype(BF16))
                upd = pv if upd is None else upd + pv
            acc_scr[st, :, sl] = upd
            m_scr[st, :, sl] = m_new

    attend(0, ks_ref, vs_ref, [(qt, "diag")], True, True)
    attend(1, kw_ref, vw_ref, [(qt, "diag")], False, True)

    @pl.when(qt >= 1)
    def _():
        attend(0, ks_ref, vs_ref, [(qt - 1, "near")], True, False)
        attend(1, kw_ref, vw_ref, [(qt - 1, "near")], False, False)

    @pl.when(qt >= 2)
    def _():
        attend(1, kw_ref, vw_ref, [(qt - 2, "winfar")], False, False)
        n_far = qt - 1

        def far_pair(i, carry):
            attend(0, ks_ref, vs_ref, [(2 * i, "far"), (2 * i + 1, "far")], True, False)
            return carry

        lax.fori_loop(0, lax.shift_right_logical(n_far, 1), far_pair, 0)

        @pl.when((n_far & 1) == 1)
        def _():
            attend(0, ks_ref, vs_ref, [(n_far - 1, "far")], True, False)

    for st in range(2):
        br = st + 1
        for r in range(GROUP_REP):
            sl = slice(r * TQ, (r + 1) * TQ)
            w = gate_t[br * GROUP_REP + r:br * GROUP_REP + r + 1, :] / acc_scr[st, HEAD_DIM:HEAD_DIM + 1, sl]
            emit(br, r, acc_scr[st, 0:HEAD_DIM, sl] * w, False)


def _attn_prompt(proj, kc, vc, k_slc, v_slc, k_win, v_win, t0, t1, tc, ovl_t, *, bsz, t):
    nq = t // TQ
    gw = GROUP_REP * HEAD_DIM
    assert TQ // 2 - (TQ // CMP_STRIDE) * (nq - 1) >= 0 and kc.shape[1] == N_KV * (t // CMP_STRIDE)
    n_cmp = t // CMP_STRIDE
    d_att = N_HEADS * HEAD_DIM
    zoff = d_att // gw
    row = lambda b, g, i: b * nq + i
    in_specs = [
        pl.BlockSpec((TQ, gw), lambda b, g, i: (row(b, g, i), g)),
    ] + [
        pl.BlockSpec((TQ, gw), lambda b, g, i, br=br: (row(b, g, i), zoff * (1 + br) + g)) for br in range(N_BRANCH)
    ] + [
        pl.BlockSpec((TQ, LANES), lambda b, g, i: (row(b, g, i), (1 + N_BRANCH) * d_att // LANES + g)),
        pl.BlockSpec((1, n_cmp, HEAD_DIM), lambda b, g, i: (b, g, 0)),
        pl.BlockSpec((1, n_cmp, HEAD_DIM), lambda b, g, i: (b, g, 0)),
    ] + [pl.BlockSpec((None, None, t, HEAD_DIM), lambda b, g, i: (b, g, 0, 0)),
         pl.BlockSpec((t, HEAD_DIM), lambda b, g, i: (b, g))] * 2 + [
        pl.BlockSpec((GROUP_REP, TQ, TQ), lambda b, g, i: (g, 0, 0))] * 3 + [
        pl.BlockSpec(ovl_t.shape, lambda b, g, i: (0, 0)),
    ]
    return pl.pallas_call(
        _attn_prompt_kernel,
        grid=(bsz, N_KV, nq),
        in_specs=in_specs,
        out_specs=pl.BlockSpec((TQ, gw), lambda b, g, i: (row(b, g, i), g)),
        out_shape=jax.ShapeDtypeStruct((bsz * t, d_att), F32),
        scratch_shapes=[
            pltpu.VMEM((2, 1, GROUP_REP * TQ), F32),
            pltpu.VMEM((2, ACC_ROWS, GROUP_REP * TQ), F32),
            pltpu.VMEM((t // TQ, SUBLANES, TQ), F32),
        ],
        compiler_params=_params("parallel", "parallel", "arbitrary"),
        name="nsa_attention_prompt",
    )(proj, proj, proj, proj, proj, kc, vc, k_slc, v_slc, k_win, v_win, t0, t1, tc, ovl_t)


def _attn_sample_kernel(bits_ref, pt_ref, *refs, n_pages):
    kp, vp = refs[:n_pages], refs[n_pages:2 * n_pages]
    (kwin_ref, vwin_ref, nks_ref, nvs_ref, nkw_ref, nvw_ref, q_ref, oc_ref, ts_ref, tw_ref,
     y_ref, kwo_ref, vwo_ref) = refs[2 * n_pages:]
    b = pl.program_id(0)
    rows_pp = PAGE_SIZE * N_KV
    blk_rows = SLC_BLOCK * N_KV
    d_att = N_HEADS * HEAD_DIM
    n_win_rows = kwin_ref.shape[1]
    head_rows = functools.partial(_head_rows, q_ref)

    for src, new, dst in ((kwin_ref, nkw_ref, kwo_ref), (vwin_ref, nvw_ref, vwo_ref)):
        dst[0, 0:n_win_rows - N_KV, :] = src[0, N_KV:n_win_rows, :]
        dst[0, n_win_rows - N_KV:n_win_rows, :] = jnp.concatenate(
            [new[0, :, g * HEAD_DIM:(g + 1) * HEAD_DIM] for g in range(N_KV)], axis=0)

    def group_rows(ref):
        return jnp.concatenate([ref[0, :, (h // GROUP_REP) * HEAD_DIM:(h // GROUP_REP + 1) * HEAD_DIM]
                                for h in range(N_HEADS)], axis=0)

    def softmax_pv(s_tiles, v_tiles, s_new, v_new):
        m = s_new
        for s in s_tiles:
            m = jnp.maximum(m, jnp.max(s, axis=-1, keepdims=True))
        p_new = jnp.exp(s_new - m)
        l = p_new
        acc = p_new * v_new
        for s, v in zip(s_tiles, v_tiles):
            p = jnp.exp(s - m)
            l = l + jnp.sum(p, axis=-1, keepdims=True)
            acc = acc + _dot(p.astype(BF16), v.astype(BF16))
        return acc / l

    q32 = head_rows(0) * SCALE
    q16 = q32.astype(BF16)
    head = lax.broadcasted_iota(jnp.int32, (N_HEADS, 1), 0)
    bits = jnp.zeros((N_HEADS, 1), jnp.int32)
    for g in range(N_KV):
        bits = jnp.where((head >= g * GROUP_REP) & (head < (g + 1) * GROUP_REP), bits_ref[b, g], bits)
    lane_blk = lax.broadcasted_iota(jnp.int32, (1, rows_pp), 1) // blk_rows
    s_tiles, v_tiles = [], []
    for p in range(n_pages):
        s = _dot_nt(q16, kp[p][0].astype(BF16)) + ts_ref[:, p * rows_pp:(p + 1) * rows_pp]
        shift = jnp.broadcast_to(lane_blk + p * (rows_pp // blk_rows), s.shape)
        sel = lax.shift_right_logical(jnp.broadcast_to(bits, s.shape), shift) & 1
        s_tiles.append(jnp.where(sel == 1, s, NEG_INF))
        v_tiles.append(vp[p][0])
    tail = n_pages * rows_pp
    s_new = jnp.sum(q32 * group_rows(nks_ref), axis=-1, keepdims=True) + ts_ref[:, tail:tail + 1]
    o_slc = softmax_pv(s_tiles, v_tiles, s_new, group_rows(nvs_ref))
    s_tiles, v_tiles = [], []
    for p in range(n_win_rows // rows_pp):
        rows = slice(p * rows_pp, (p + 1) * rows_pp)
        s_tiles.append(_dot_nt(q16, kwin_ref[0, rows, :].astype(BF16)) + tw_ref[:, rows])
        v_tiles.append(vwin_ref[0, rows, :])
    s_new = jnp.sum(q32 * group_rows(nkw_ref), axis=-1, keepdims=True) + tw_ref[:, n_win_rows:n_win_rows + 1]
    o_win = softmax_pv(s_tiles, v_tiles, s_new, group_rows(nvw_ref))
    y = jnp.zeros((N_HEADS, HEAD_DIM), F32)
    gate0 = (1 + N_BRANCH) * d_att
    for br, o in enumerate((oc_ref[0], o_slc, o_win)):
        cols = [gate0 + (h // GROUP_REP) * LANES + br * GROUP_REP + h % GROUP_REP for h in range(N_HEADS)]
        gate = jnp.concatenate([q_ref[0, :, c:c + 1] for c in cols], axis=0)
        y = y + gate * o * head_rows((1 + br) * d_att)
    y_ref[0] = y


def _attn_sample(cache_k, cache_v, page_table, bits, k_win, v_win, new_rows, proj, o_cmp, ts, tw):
    nb, n_pages = page_table.shape
    rows_pp = PAGE_SIZE * N_KV
    win_spec = pl.BlockSpec((1, k_win.shape[1], HEAD_DIM), lambda b, bits, pt: (b, 0, 0))
    row_spec = lambda w: pl.BlockSpec((1, 1, w), lambda b, bits, pt: (b, 0, 0))
    const = lambda a: pl.BlockSpec(a.shape, lambda b, bits, pt: (0,) * a.ndim)
    page_specs = [pl.BlockSpec((1, rows_pp, HEAD_DIM), lambda b, bits, pt, p=p: (pt[b, p], 0, 0)) for p in range(n_pages)]
    grid_spec = pltpu.PrefetchScalarGridSpec(
        num_scalar_prefetch=2,
        grid=(nb,),
        in_specs=(page_specs * 2 + [win_spec] * 2
                  + [row_spec(N_KV * HEAD_DIM)] * 4
                  + [row_spec(proj.shape[-1]),
                     pl.BlockSpec((1, N_HEADS, HEAD_DIM), lambda b, bits, pt: (b, 0, 0)),
                     const(ts), const(tw)]),
        out_specs=[pl.BlockSpec((1, N_HEADS, HEAD_DIM), lambda b, bits, pt: (b, 0, 0)), win_spec, win_spec],
    )
    return pl.pallas_call(
        functools.partial(_attn_sample_kernel, n_pages=n_pages),
        grid_spec=grid_spec,
        out_shape=[jax.ShapeDtypeStruct((nb, N_HEADS, HEAD_DIM), F32),
                   jax.ShapeDtypeStruct(k_win.shape, F32), jax.ShapeDtypeStruct(v_win.shape, F32)],
        compiler_params=_params("parallel"),
        name="nsa_attention_sample",
    )(bits, page_table, *([cache_k] * n_pages), *([cache_v] * n_pages), k_win, v_win, *new_rows,
      proj, o_cmp, ts, tw)


def _overlap(n_cmp, n_slc):
    cs = np.arange(n_cmp) * CMP_STRIDE
    bs = np.arange(n_slc) * SLC_BLOCK
    ov = np.clip(np.minimum(cs[:, None] + CMP_LEN, bs[None, :] + SLC_BLOCK) - np.maximum(cs[:, None], bs[None, :]), 0, None)
    return (ov / CMP_LEN).astype(np.float32)


def _compress_weights(w1, w2, pe):
    half = CMP_STRIDE * HEAD_DIM
    wcat = jnp.concatenate([w1[:half], w1[half:]], axis=1).astype(BF16)
    pe8 = jnp.pad(pe.reshape(2, half), ((0, SUBLANES - 2), (0, 0)))
    return wcat, w2.astype(BF16), pe8


def kernel(x_prompt, x_sample, state_pool, cache_k_cmp, cache_v_cmp, cache_k_slc, cache_v_slc, state_k_win, state_v_win, page_table, c_prompt, c_sample, g_norm, w_ada, b_ada, w_in_a, w_grp, pool_scale, w_out_a, g_kv, w_kv, pe_k, w_ck1, w_ck2, pe_v, w_cv1, w_cv2, rel_bias, w_in_b, w_out_b, g_final):
    bsz, t, d = x_prompt.shape
    nb = x_sample.shape[0]
    n_pages = page_table.shape[1]
    past = n_pages * PAGE_SIZE
    d_att = N_HEADS * HEAD_DIM
    d_kv = N_KV * HEAD_DIM
    n_a = w_in_a.shape[0]
    assert n_a == 1 and w_in_b.shape[0] == 1 and x_sample.shape[1] == 1
    assert t % TQ == 0 and t >= WINDOW and past % PAGE_SIZE == 0 and state_k_win.shape[1] == WINDOW

    w_in_a16 = w_in_a[0].astype(BF16)
    w_grp16 = w_grp[0].astype(BF16)
    w_out_a16 = w_out_a[0].astype(BF16)
    w_kv16 = w_kv.astype(BF16)
    w_out_b16 = w_out_b[0].astype(BF16)
    n_qz = (1 + N_BRANCH) * d_att
    wg = w_in_b[0][:, n_qz:].reshape(d, N_BRANCH, N_KV, GROUP_REP).transpose(0, 2, 1, 3)
    wg = jnp.pad(wg.reshape(d, N_KV, N_BRANCH * GROUP_REP), ((0, 0), (0, 0), (0, LANES - N_BRANCH * GROUP_REP)))
    n_proj = n_qz + TN_PROJ
    wg = jnp.pad(wg.reshape(d, N_KV * LANES), ((0, 0), (0, TN_PROJ - N_KV * LANES)))
    w_in_b16 = jnp.concatenate([w_in_b[0][:, :n_qz], wg], axis=1).astype(BF16)
    nsa_proj = functools.partial(_nm_matmul, w=w_in_b16, n_cols=n_proj, tn=TN_PROJ, silu_cols=d_att, sigmoid_cols=n_qz)
    cw_k = _compress_weights(w_ck1, w_ck2, pe_k)
    cw_v = _compress_weights(w_cv1, w_cv2, pe_v)
    cw = cw_k + cw_v

    mod = _ada(jnp.concatenate([c_prompt, c_sample], axis=0), w_ada, b_ada)

    def modulation(l, lo, hi, per_row):
        parts = [mod[l, lo:hi, k * d:(k + 1) * d] for k in range(3)]
        return [p[None] if per_row else p[:, None] for p in parts]

    t0, t1, tc, ts, tw, tcs = _bias_tables(rel_bias, past=past, n_cmp_s=past // CMP_STRIDE)
    ts, tw, tcs = ts.reshape(N_HEADS, -1), tw.reshape(N_HEADS, -1), tcs.reshape(N_HEADS, -1)

    shift, scale, gate = modulation(0, 0, bsz, False)
    uz = _nm_matmul(x_prompt, g_norm[0], scale, shift, w_in_a16, n_cols=2 * d, tm=TM_PROJ, tn=TN_PROJ, name="in_proj_pool_prompt")
    pool_p = uz[:, t - POOL_BUF:, :d][None]
    y0 = _pool_prompt(uz, w_grp16, pool_scale[0], tm=256)
    x1 = _proj_residual(y0, w_out_a16, x_prompt, gate, g_final, tm=512, final_norm=False, name="out_proj_pool_prompt")
    kv_p = _kv_proj(x1, g_kv, w_kv16, tm=256, name="kv_proj_prompt")
    heads = lambda a: a.reshape(a.shape[0], a.shape[1], N_KV, HEAD_DIM)
    kv_state_p = [heads(a) for a in kv_p[:4]] + [heads(a[:, t - WINDOW:]) for a in kv_p[4:6]]
    kc_p, vc_p = _compress_prompt(kv_state_p[0].reshape(bsz, t * N_KV, HEAD_DIM),
                                  kv_state_p[1].reshape(bsz, t * N_KV, HEAD_DIM), cw)
    shift, scale, gate = modulation(1, 0, bsz, False)
    proj = nsa_proj(x1, g_norm[1], scale, shift, tm=TM_PROJ, name="in_proj_nsa_prompt")
    ovl_t = jnp.asarray(np.pad(_overlap(t // CMP_STRIDE - 1, t // SLC_BLOCK), ((0, 1), (0, 0))).T).astype(BF16)
    flat = lambda a: a.reshape(bsz * t, a.shape[-1])
    y1 = _attn_prompt(flat(proj), kc_p, vc_p, kv_p[6], flat(kv_p[3]), kv_p[7], flat(kv_p[5]),
                      t0, t1, tc, ovl_t, bsz=bsz, t=t)
    y_prompt = _proj_residual(y1.reshape(bsz, t, d_att), w_out_b16, x1, gate, g_final, tm=512, final_norm=True,
                              name="out_proj_nsa_prompt")

    xs = x_sample.reshape(1, nb, d)
    shift, scale, gate = modulation(0, bsz, bsz + nb, True)
    uz_s = _nm_matmul(xs, g_norm[0], scale, shift, w_in_a16, n_cols=2 * d, tm=nb, tn=512, name="in_proj_pool_sample")
    pool_s = jnp.concatenate([state_pool[:, :, 1:], uz_s[0, :, None, :d][None]], axis=2)
    y0_s = _pool_sample(uz_s, state_pool[0], w_grp16, pool_scale[0], qpos=past, tb=min(32, nb))
    x1_s = _proj_residual(y0_s, w_out_a16, xs, gate, g_final, tm=nb, final_norm=False, name="out_proj_pool_sample")
    kv_s = _kv_proj(x1_s, g_kv, w_kv16, tm=nb, name="kv_proj_sample")
    new_rows = [a.reshape(nb, 1, d_kv) for a in kv_s[:2 * N_BRANCH]]
    shift, scale, gate = modulation(1, bsz, bsz + nb, True)
    proj_s = nsa_proj(x1_s, g_norm[1], scale, shift, tm=nb, name="in_proj_nsa_sample").reshape(nb, 1, n_proj)
    n_cmp_s = past // CMP_STRIDE
    n_slc_s = past // SLC_BLOCK + 1
    ovl_s = jnp.asarray(np.tile(np.pad(_overlap(n_cmp_s, n_slc_s), ((0, 0), (0, LANES - n_slc_s))), (N_KV, 1))).astype(BF16)
    rows_of = lambda c: c.reshape(c.shape[0], PAGE_SIZE * N_KV, HEAD_DIM)
    o_cmp, bits = _compress_sample(rows_of(cache_k_cmp), rows_of(cache_v_cmp), page_table, new_rows[0], new_rows[1], cw,
                                   proj_s, tcs, ovl_s, past=past)
    y1_s, k_win_s, v_win_s = _attn_sample(
        rows_of(cache_k_slc), rows_of(cache_v_slc), page_table, bits[:, :, 0],
        state_k_win.reshape(nb, WINDOW * N_KV, HEAD_DIM), state_v_win.reshape(nb, WINDOW * N_KV, HEAD_DIM),
        new_rows[2:], proj_s, o_cmp, ts, tw)
    y_sample = _proj_residual(y1_s.reshape(1, nb, d_att), w_out_b16, x1_s, gate, g_final, tm=nb, final_norm=True,
                              name="out_proj_nsa_sample").reshape(nb, 1, d)
    new4 = [a.reshape(nb, 1, N_KV, HEAD_DIM) for a in kv_s[:2 * N_BRANCH]]
    kv_state_s = new4[:4] + [k_win_s.reshape(state_k_win.shape), v_win_s.reshape(state_v_win.shape)]

    return (y_prompt, y_sample, pool_p, *kv_state_p, pool_s, *kv_state_s)
```

```python
import functools
import math

import numpy as np
import jax
import jax.numpy as jnp
from jax import lax
from jax.experimental import pallas as pl
from jax.experimental.pallas import tpu as pltpu

F32 = jnp.float32
BF16 = jnp.bfloat16

PAGE_SIZE = 128
POOL_WINDOWS = (2, 4, 8, 16)
POOL_BUF = max(POOL_WINDOWS) - 1
HEAD_DIM = 128
N_KV = 4
GROUP_REP = 4
N_HEADS = N_KV * GROUP_REP
N_BRANCH = 3
CMP_LEN = 32
CMP_STRIDE = 16
SLC_BLOCK = 64
N_SEL = 8
WINDOW = 512
N_BUCKETS = 32
MAX_DISTANCE = 128
RMS_EPS = 1e-6
SCALE = HEAD_DIM ** -0.5
LOG2E = math.log2(math.e)
NEG_INF = -1e30
FORCE_SCORE = 1e9

LANES = 128
SUBLANES = 8
VMEM_LIMIT_BYTES = 56 * 1024 * 1024
TQ = 256
HEADS_PER_DOT = 4
ACC_ROWS = HEAD_DIM + 16
TN_PROJ = 1024
TM_PROJ = 1024
HALO = 16


def _bucket_starts():
    max_exact = N_BUCKETS // 2
    d = np.arange(0, MAX_DISTANCE + 1)
    large = max_exact + np.floor(
        np.log(np.maximum(d, max_exact) / max_exact) / math.log(MAX_DISTANCE / max_exact) * (N_BUCKETS - max_exact)
    ).astype(np.int64)
    bucket = np.where(d < max_exact, d, np.minimum(large, N_BUCKETS - 1))
    return [int(np.argmax(bucket >= k)) for k in range(N_BUCKETS)]


BUCKET_STARTS = _bucket_starts()
FAR_DIST = BUCKET_STARTS[-1]


def _params(*sem):
    return pltpu.CompilerParams(dimension_semantics=sem, vmem_limit_bytes=VMEM_LIMIT_BYTES)


def _silu(x):
    return x * jax.nn.sigmoid(x)


def _rms(x, g):
    return x * lax.rsqrt(jnp.mean(x * x, axis=-1, keepdims=True) + RMS_EPS) * g


def _dot(a, b):
    return jnp.dot(a, b, preferred_element_type=F32)


def _dot_nt(a, b):
    return lax.dot_general(a, b, (((1,), (1,)), ((), ())), preferred_element_type=F32)


def _ada_kernel(c_ref, w_ref, b_ref, o_ref):
    a = _silu(c_ref[...]).astype(BF16)
    o_ref[0] = _dot(a, w_ref[0].astype(BF16)) + b_ref[0]


def _ada(c_all, w_ada, b_ada):
    depth, d, n = w_ada.shape
    m = c_all.shape[0]
    tn = 512
    return pl.pallas_call(
        _ada_kernel,
        grid=(depth, n // tn),
        in_specs=[
            pl.BlockSpec((m, d), lambda l, j: (0, 0)),
            pl.BlockSpec((1, d, tn), lambda l, j: (l, 0, j)),
            pl.BlockSpec((1, 1, tn), lambda l, j: (l, 0, j)),
        ],
        out_specs=pl.BlockSpec((1, m, tn), lambda l, j: (l, 0, j)),
        out_shape=jax.ShapeDtypeStruct((depth, m, n), F32),
        compiler_params=_params("parallel", "parallel"),
        name="ada_modulation",
    )(c_all, w_ada, b_ada.reshape(depth, 1, n))


def _nm_matmul_kernel(x_ref, g_ref, sc_ref, sh_ref, w_ref, o_ref, h_scr, *, silu_from, sigmoid_from):
    j = pl.program_id(2)

    @pl.when(j == 0)
    def _():
        y = _rms(x_ref[0], g_ref[...])
        h_scr[...] = (y * (1.0 + sc_ref[0]) + sh_ref[0]).astype(BF16)

    r = _dot(h_scr[...], w_ref[...])
    if silu_from is None:
        o_ref[0] = r
    else:
        sig = jax.nn.sigmoid(r)
        o_ref[0] = jnp.where(j >= sigmoid_from, sig, jnp.where(j >= silu_from, r * sig, r))


def _mod_spec(arr, tm):
    d = arr.shape[-1]
    if arr.shape[1] == 1:
        return pl.BlockSpec((1, 1, d), lambda b, i, *_: (b, 0, 0))
    return pl.BlockSpec((1, tm, d), lambda b, i, *_: (b, i, 0))


def _nm_matmul(x, g, scale, shift, w, *, n_cols, tm, tn, silu_cols=None, sigmoid_cols=None, name):
    bsz, rows, d = x.shape
    if silu_cols is None and sigmoid_cols is None:
        silu_from = sigmoid_from = None
    else:
        sigmoid_from = n_cols // tn if sigmoid_cols is None else sigmoid_cols // tn
        silu_from = sigmoid_from if silu_cols is None else silu_cols // tn
    return pl.pallas_call(
        functools.partial(_nm_matmul_kernel, silu_from=silu_from, sigmoid_from=sigmoid_from),
        grid=(bsz, rows // tm, n_cols // tn),
        in_specs=[
            pl.BlockSpec((1, tm, d), lambda b, i, j: (b, i, 0)),
            pl.BlockSpec((1, d), lambda b, i, j: (0, 0)),
            _mod_spec(scale, tm),
            _mod_spec(shift, tm),
            pl.BlockSpec((d, tn), lambda b, i, j: (0, j)),
        ],
        out_specs=pl.BlockSpec((1, tm, tn), lambda b, i, j: (b, i, j)),
        out_shape=jax.ShapeDtypeStruct((bsz, rows, n_cols), F32),
        scratch_shapes=[pltpu.VMEM((tm, d), BF16)],
        compiler_params=_params("parallel", "parallel", "arbitrary"),
        name=name,
    )(x, g.reshape(1, d), scale, shift, w)


def _pool_mix(pooled_fn, z, wg_ref, ps_ref, store):
    grp = z.shape[-1] // len(POOL_WINDOWS)
    for gi, w in enumerate(POOL_WINDOWS):
        cs = slice(gi * grp, (gi + 1) * grp)
        mixed = _dot(pooled_fn(gi, w, cs).astype(BF16), wg_ref[gi])
        store(cs, mixed * ps_ref[:, cs] * _silu(z[:, cs]))


def _pool_prompt_kernel(u_ref, z_ref, halo_ref, wg_ref, ps_ref, wo_ref, x_ref, gate_ref, o_ref,
                        ue_scr, a_scr, b_scr, y_scr):
    i = pl.program_id(1)
    tm = u_ref.shape[1]
    top = SUBLANES + HALO
    n = top + tm
    u = u_ref[0]
    ue_scr[0:SUBLANES, :] = jnp.zeros((SUBLANES, u.shape[1]), F32)
    ue_scr[SUBLANES:top, :] = jnp.where(i > 0, halo_ref[0], 0.0)
    ue_scr[top:n, :] = u
    zero8 = jnp.zeros((SUBLANES, a_scr.shape[1]), F32)
    a_scr[0:SUBLANES, :] = zero8
    b_scr[0:SUBLANES, :] = zero8
    pos = i * tm + lax.broadcasted_iota(jnp.int32, (tm, 1), 0)

    def pooled(gi, w, cs):
        src = lambda lo, hi: ue_scr[lo:hi, cs]
        k = 1
        for dst in (a_scr, b_scr, a_scr):
            if 2 * k >= w:
                break
            dst[SUBLANES:n, :] = src(SUBLANES, n) + src(SUBLANES - k, n - k)
            src = lambda lo, hi, dst=dst: dst[lo:hi, :]
            k *= 2
        acc = src(top, n) + src(top - k, n - k)
        cnt = jnp.minimum(pos + 1, w).astype(F32)
        return acc / cnt - u[:, cs]

    def store(cs, val):
        y_scr[:, cs] = val.astype(BF16)

    _pool_mix(pooled, z_ref[0], wg_ref, ps_ref, store)
    o_ref[0] = x_ref[0] + gate_ref[0] * _dot(y_scr[...], wo_ref[...])


def _pool_prompt(uz, w_grp, pool_scale, w_out, x, gate, *, tm):
    bsz, t, d2 = uz.shape
    d = d2 // 2
    ng, grp, _ = w_grp.shape
    return pl.pallas_call(
        _pool_prompt_kernel,
        grid=(bsz, t // tm),
        in_specs=[
            pl.BlockSpec((1, tm, d), lambda b, i: (b, i, 0)),
            pl.BlockSpec((1, tm, d), lambda b, i: (b, i, 1)),
            pl.BlockSpec((1, HALO, d), lambda b, i: (b, jnp.maximum(i * (tm // HALO) - 1, 0), 0)),
            pl.BlockSpec((ng, grp, grp), lambda b, i: (0, 0, 0)),
            pl.BlockSpec((1, d), lambda b, i: (0, 0)),
            pl.BlockSpec(w_out.shape, lambda b, i: (0, 0)),
            pl.BlockSpec((1, tm, d), lambda b, i: (b, i, 0)),
            _mod_spec(gate, tm),
        ],
        out_specs=pl.BlockSpec((1, tm, d), lambda b, i: (b, i, 0)),
        out_shape=jax.ShapeDtypeStruct((bsz, t, d), F32),
        scratch_shapes=[pltpu.VMEM((SUBLANES + HALO + tm, d), F32),
                        pltpu.VMEM((SUBLANES + HALO + tm, grp), F32),
                        pltpu.VMEM((SUBLANES + HALO + tm, grp), F32),
                        pltpu.VMEM((tm, d), BF16)],
        compiler_params=_params("parallel", "parallel"),
        name="pool_mixer_prompt",
    )(uz, uz, uz, w_grp, pool_scale.reshape(1, d), w_out, x, gate)


def _pool_sample_kernel(u_ref, z_ref, pre_ref, wg_ref, ps_ref, y_ref, *, qpos):
    u = u_ref[0]

    def pooled(gi, w, cs):
        acc = u[:, cs]
        for k in range(1, w):
            acc = acc + pre_ref[:, POOL_BUF - k, cs]
        return acc / float(min(qpos + 1, w)) - u[:, cs]

    def store(cs, val):
        y_ref[0, :, cs] = val

    _pool_mix(pooled, z_ref[0], wg_ref, ps_ref, store)


def _pool_sample(uz, prefix, w_grp, pool_scale, *, qpos, tb):
    _, nb, d2 = uz.shape
    d = d2 // 2
    ng, grp, _ = w_grp.shape
    return pl.pallas_call(
        functools.partial(_pool_sample_kernel, qpos=qpos),
        grid=(nb // tb,),
        in_specs=[
            pl.BlockSpec((1, tb, d), lambda i: (0, i, 0)),
            pl.BlockSpec((1, tb, d), lambda i: (0, i, 1)),
            pl.BlockSpec((tb, POOL_BUF, d), lambda i: (i, 0, 0)),
            pl.BlockSpec((ng, grp, grp), lambda i: (0, 0, 0)),
            pl.BlockSpec((1, d), lambda i: (0, 0)),
        ],
        out_specs=pl.BlockSpec((1, tb, d), lambda i: (0, i, 0)),
        out_shape=jax.ShapeDtypeStruct((1, nb, d), F32),
        compiler_params=_params("parallel"),
        name="pool_mixer_sample",
    )(uz, uz, prefix, w_grp, pool_scale.reshape(1, d))


def _proj_residual_kernel(y_ref, w_ref, x_ref, gate_ref, g_ref, o_ref, *, final_norm):
    xo = x_ref[0] + gate_ref[0] * _dot(y_ref[0].astype(BF16), w_ref[...])
    o_ref[0] = _rms(xo, g_ref[...]) if final_norm else xo


def _proj_residual(y, w, x, gate, g_final, *, tm, final_norm, name):
    bsz, rows, d = x.shape
    dy = y.shape[-1]
    return pl.pallas_call(
        functools.partial(_proj_residual_kernel, final_norm=final_norm),
        grid=(bsz, rows // tm),
        in_specs=[
            pl.BlockSpec((1, tm, dy), lambda b, i: (b, i, 0)),
            pl.BlockSpec((dy, d), lambda b, i: (0, 0)),
            pl.BlockSpec((1, tm, d), lambda b, i: (b, i, 0)),
            _mod_spec(gate, tm),
            pl.BlockSpec((1, d), lambda b, i: (0, 0)),
        ],
        out_specs=pl.BlockSpec((1, tm, d), lambda b, i: (b, i, 0)),
        out_shape=jax.ShapeDtypeStruct((bsz, rows, d), F32),
        compiler_params=_params("parallel", "parallel"),
        name=name,
    )(y, w, x, gate, g_final.reshape(1, d))


KV_GROUP_COPIES = ((2, BF16), (3, F32), (4, BF16), (5, F32))


def _kv_kernel(x_ref, g_ref, w_ref, *o_refs):
    h = _rms(x_ref[0], g_ref[...]).astype(BF16)
    n_out = 2 * N_BRANCH
    tm = x_ref.shape[1]
    n = N_KV * HEAD_DIM
    copies = dict((o, (o_refs[n_out + i], dt)) for i, (o, dt) in enumerate(KV_GROUP_COPIES))
    for o in range(n_out):
        r = _dot(h, w_ref[:, o * n:(o + 1) * n])
        for g in range(N_KV):
            rg = r[:, g * HEAD_DIM:(g + 1) * HEAD_DIM]
            o_refs[o][0, pl.ds(g, tm, stride=N_KV), :] = rg
            if o in copies:
                c_ref, dt = copies[o]
                c_ref[0, g] = rg.astype(dt)


def _kv_proj(x, g_kv, w_kv, *, tm, name):
    bsz, rows, d = x.shape
    n_out = 2 * N_BRANCH
    return pl.pallas_call(
        _kv_kernel,
        grid=(bsz, rows // tm),
        in_specs=[
            pl.BlockSpec((1, tm, d), lambda b, i: (b, i, 0)),
            pl.BlockSpec((1, d), lambda b, i: (0, 0)),
            pl.BlockSpec(w_kv.shape, lambda b, i: (0, 0)),
        ],
        out_specs=([pl.BlockSpec((1, tm * N_KV, HEAD_DIM), lambda b, i: (b, i, 0))] * n_out
                   + [pl.BlockSpec((1, N_KV, tm, HEAD_DIM), lambda b, i: (b, 0, i, 0))] * len(KV_GROUP_COPIES)),
        out_shape=([jax.ShapeDtypeStruct((bsz, rows * N_KV, HEAD_DIM), F32)] * n_out
                   + [jax.ShapeDtypeStruct((bsz, N_KV, rows, HEAD_DIM), dt) for _, dt in KV_GROUP_COPIES]),
        compiler_params=_params("parallel", "parallel"),
        name=name,
    )(x, g_kv.reshape(1, d), w_kv)


def _bias_lookup(rb_ref, h, dist):
    acc = jnp.full(dist.shape, rb_ref[N_BUCKETS - 1, h], F32)
    for k in range(N_BUCKETS - 2, -1, -1):
        acc = jnp.where(dist < BUCKET_STARTS[k + 1], rb_ref[k, h], acc)
    return acc


def _bias_tables_kernel(rb_ref, t0_ref, t1_ref, tc_ref, ts_ref, tw_ref, tcs_ref, *, past):
    h = pl.program_id(0)
    key = lax.broadcasted_iota(jnp.int32, (TQ, TQ), 0)
    qry = lax.broadcasted_iota(jnp.int32, (TQ, TQ), 1)
    far = rb_ref[N_BUCKETS - 1, h]
    rel2 = lambda dist: (_bias_lookup(rb_ref, h, dist) - far) * LOG2E
    t0_ref[0] = rel2(qry - key)
    t1_ref[0] = rel2(TQ + qry - key)
    tc_ref[0] = rel2(qry - CMP_STRIDE * (key - TQ // 2) - (CMP_LEN - 1))
    ls = lax.broadcasted_iota(jnp.int32, ts_ref.shape[1:], 1)
    own = (ls % N_KV) == h // GROUP_REP
    tok = ls // N_KV
    ts_ref[0] = jnp.where(ls == N_KV * past, rb_ref[0, h],
                          jnp.where(own & (tok < past), _bias_lookup(rb_ref, h, past - tok), NEG_INF))
    lw = lax.broadcasted_iota(jnp.int32, tw_ref.shape[1:], 1)
    own = (lw % N_KV) == h // GROUP_REP
    dw = WINDOW - lw // N_KV
    tw_ref[0] = jnp.where(lw == N_KV * WINDOW, rb_ref[0, h],
                          jnp.where(own & (dw > 0) & (dw < WINDOW) & (past - dw >= 0), _bias_lookup(rb_ref, h, dw), NEG_INF))
    lc = lax.broadcasted_iota(jnp.int32, tcs_ref.shape[1:], 1)
    n_cmp = tcs_ref.shape[2] // N_KV
    dc = past - CMP_STRIDE * (lc % n_cmp) - (CMP_LEN - 1)
    tcs_ref[0] = jnp.where((lc // n_cmp == h // GROUP_REP) & (dc >= 0), _bias_lookup(rb_ref, h, dc), NEG_INF)


def _bias_tables(rel_bias, *, past, n_cmp_s):
    nh = rel_bias.shape[1]
    ls = N_KV * past + LANES
    lw = N_KV * WINDOW + LANES
    shapes = [(nh, TQ, TQ)] * 3 + [(nh, 1, ls), (nh, 1, lw), (nh, 1, N_KV * n_cmp_s)]
    return pl.pallas_call(
        functools.partial(_bias_tables_kernel, past=past),
        grid=(nh,),
        in_specs=[pl.BlockSpec(memory_space=pltpu.SMEM)],
        out_specs=[pl.BlockSpec((1,) + s[1:], lambda h: (h, 0, 0)) for s in shapes],
        out_shape=[jax.ShapeDtypeStruct(s, F32) for s in shapes],
        compiler_params=_params("parallel"),
        name="rel_bias_tables",
    )(rel_bias)


def _compress_fill(page_refs, lhs_scr):
    sub_per_page = PAGE_SIZE // CMP_STRIDE
    pairs = CMP_STRIDE // 2
    rows_pp = sub_per_page * SUBLANES
    even = (lax.broadcasted_iota(jnp.int32, (rows_pp, 1), 0) & (SUBLANES - 1)) < N_KV
    for p, pref in enumerate(page_refs):
        x4 = pref[0].reshape(sub_per_page, pairs, SUBLANES, HEAD_DIM)
        rs = slice(p * rows_pp, (p + 1) * rows_pp)
        for j in range(pairs):
            t = x4[:, j].reshape(rows_pp, HEAD_DIM)
            lhs_scr[rs, 2 * j * HEAD_DIM:(2 * j + 1) * HEAD_DIM] = jnp.where(even, t, 0.0).astype(BF16)
            lhs_scr[rs, (2 * j + 1) * HEAD_DIM:(2 * j + 2) * HEAD_DIM] = jnp.where(even, 0.0, t).astype(BF16)


def _compress_finish(acc, new_ref, wcat_ref, w2_ref, pe_ref, t_scr, p_scr, *, has_new):
    n_rows = acc.shape[0]
    n_sub = n_rows // SUBLANES
    wcat = wcat_ref[...]
    t_scr[0:n_rows, :] = acc
    t_scr[n_rows:, :] = jnp.zeros((2 * SUBLANES, 2 * HEAD_DIM), F32)
    t_scr[0:n_rows, :] = t_scr[0:n_rows, :] + t_scr[N_KV:n_rows + N_KV, :]
    if has_new:
        new8 = jnp.concatenate([new_ref[0, :, g * HEAD_DIM:(g + 1) * HEAD_DIM] for g in range(N_KV)]
                               + [jnp.zeros((SUBLANES - N_KV, HEAD_DIM), F32)], axis=0)
        t_scr[n_rows:n_rows + SUBLANES, HEAD_DIM:] = _dot(new8.astype(BF16), wcat[0:HEAD_DIM, HEAD_DIM:])
    pe_r = _dot(pe_ref[...].astype(BF16), wcat)
    pe_const = pe_r[0:1, :HEAD_DIM] + pe_r[1:2, HEAD_DIM:]
    p_scr[...] = t_scr[0:n_rows, :HEAD_DIM] + t_scr[SUBLANES:n_rows + SUBLANES, HEAD_DIM:] + pe_const
    pre = jnp.concatenate([p_scr[pl.ds(g, n_sub, stride=SUBLANES), :] for g in range(N_KV)], axis=0)
    out = _dot(_silu(pre).astype(BF16), w2_ref[...])
    if not has_new:
        blk = lax.broadcasted_iota(jnp.int32, (N_KV * n_sub, 1), 0) & (n_sub - 1)
        out = jnp.where(blk == n_sub - 1, 0.0, out)
    return out


def _compress_kv(k_pages, v_pages, k_new, v_new, k_w, v_w, scr, *, has_new):
    _compress_fill(k_pages, scr[0])
    _compress_fill(v_pages, scr[3])
    acc_k = _dot(scr[0][...], k_w[0][...])
    acc_v = _dot(scr[3][...], v_w[0][...])
    kcs = _compress_finish(acc_k, k_new, *k_w, scr[1], scr[2], has_new=has_new)
    vcs = _compress_finish(acc_v, v_new, *v_w, scr[4], scr[5], has_new=has_new)
    return kcs, vcs


def _compress_scratch(n_pages):
    n_rows = n_pages * (PAGE_SIZE // CMP_STRIDE) * SUBLANES
    return [pltpu.VMEM((n_rows, CMP_STRIDE * HEAD_DIM), BF16),
            pltpu.VMEM((n_rows + 2 * SUBLANES, 2 * HEAD_DIM), F32),
            pltpu.VMEM((n_rows, HEAD_DIM), F32)]


def _compress_prompt_kernel(pt_ref, *refs, n_pages):
    kp, vp = refs[:n_pages], refs[n_pages:2 * n_pages]
    wck, w2k, pek, wcv, w2v, pev, kc_ref, vc_ref, *scr = refs[2 * n_pages:]
    n_sub = n_pages * (PAGE_SIZE // CMP_STRIDE)
    assert n_sub & (n_sub - 1) == 0
    kc_ref[0], vc_ref[0] = _compress_kv(kp, vp, None, None, (wck, w2k, pek), (wcv, w2v, pev), scr, has_new=False)


def _page_specs(n_pages):
    return [pl.BlockSpec((1, PAGE_SIZE * N_KV, HEAD_DIM), lambda b, pt, p=p: (pt[b, p], 0, 0)) for p in range(n_pages)]


def _const_spec(shape):
    return pl.BlockSpec(shape, lambda b, pt: (0,) * len(shape))


def _compress_prompt(k_rows, v_rows, cw):
    bsz = k_rows.shape[0]
    t = k_rows.shape[1] // N_KV
    n_pages = t // PAGE_SIZE
    n_sub = t // CMP_STRIDE
    pt = jnp.arange(bsz * n_pages, dtype=jnp.int32).reshape(bsz, n_pages)
    kp = k_rows.reshape(bsz * n_pages, PAGE_SIZE * N_KV, HEAD_DIM)
    vp = v_rows.reshape(bsz * n_pages, PAGE_SIZE * N_KV, HEAD_DIM)
    wspecs = [_const_spec(a.shape) for a in cw]
    grid_spec = pltpu.PrefetchScalarGridSpec(
        num_scalar_prefetch=1,
        grid=(bsz,),
        in_specs=_page_specs(n_pages) * 2 + wspecs,
        out_specs=[pl.BlockSpec((1, N_KV * n_sub, HEAD_DIM), lambda b, pt: (b, 0, 0))] * 2,
        scratch_shapes=_compress_scratch(n_pages) * 2,
    )
    return pl.pallas_call(
        functools.partial(_compress_prompt_kernel, n_pages=n_pages),
        grid_spec=grid_spec,
        out_shape=[jax.ShapeDtypeStruct((bsz, N_KV * n_sub, HEAD_DIM), F32)] * 2,
        compiler_params=_params("parallel"),
        name="compress_prompt",
    )(pt, *([kp] * n_pages), *([vp] * n_pages), *cw)


def _head_rows(ref, base):
    return jnp.concatenate([ref[0, :, base + h * HEAD_DIM:base + (h + 1) * HEAD_DIM] for h in range(N_HEADS)], axis=0)


def _split_dot(a, b):
    hi = a.astype(BF16)
    lo = (a - hi.astype(F32)).astype(BF16)
    return _dot(hi, b) + _dot(lo, b)


def _compress_sample_kernel(pt_ref, *refs, n_pages, cur_blk):
    kp, vp = refs[:n_pages], refs[n_pages:2 * n_pages]
    (newk, newv, wck, w2k, pek, wcv, w2v, pev, q_ref, tcs_ref, ovl_ref, pow_ref,
     oc_ref, bits_ref, *scr) = refs[2 * n_pages:]
    kc, vc = _compress_kv(kp, vp, newk, newv, (wck, w2k, pek), (wcv, w2v, pev), scr, has_new=True)
    q16 = (_head_rows(q_ref, 0) * SCALE).astype(BF16)
    s = _dot_nt(q16, kc.astype(BF16)) + tcs_ref[...]
    e = jnp.exp(s - jnp.max(s, axis=-1, keepdims=True))
    p = e / jnp.sum(e, axis=-1, keepdims=True)
    oc_ref[0] = _dot(p.astype(BF16), vc.astype(BF16))
    imp_h = _split_dot(p, ovl_ref[...])
    imp = jnp.concatenate([jnp.sum(imp_h[GROUP_REP * g:GROUP_REP * (g + 1)], axis=0, keepdims=True)
                           for g in range(N_KV)], axis=0)
    lane = lax.broadcasted_iota(jnp.int32, (1, LANES), 1)
    forced = (lane == cur_blk) | (lane == 0)
    imp = jnp.where(forced, FORCE_SCORE, jnp.where(lane <= cur_blk, imp, -FORCE_SCORE))
    rank = jnp.zeros((N_KV, LANES), F32)
    for k in range(cur_blk + 1):
        ck = imp[:, k:k + 1]
        rank = rank + jnp.where(ck > imp, 1.0, jnp.where((ck == imp) & (lane > k), 1.0, 0.0))
    sel = jnp.where(rank < N_SEL, 1.0, 0.0) * pow_ref[...]
    lo = jnp.sum(jnp.where(lane < 16, sel, 0.0), axis=-1, keepdims=True).astype(jnp.int32)
    hi = jnp.sum(jnp.where((lane >= 16) & (lane < 32), sel, 0.0), axis=-1, keepdims=True).astype(jnp.int32)
    bits_ref[0] = jnp.broadcast_to(lo | (hi << 16), (N_KV, LANES))


def _compress_sample(cache_k, cache_v, page_table, new_k, new_v, cw, q, tcs, ovl, *, past):
    nb, n_pages = page_table.shape
    width = new_k.shape[-1]
    n_sub = n_pages * (PAGE_SIZE // CMP_STRIDE)
    pow2 = np.zeros((1, LANES), np.float32)
    pow2[0, :32] = 2.0 ** (np.arange(32) % 16)
    row_spec = pl.BlockSpec((1, 1, width), lambda b, pt: (b, 0, 0))
    grid_spec = pltpu.PrefetchScalarGridSpec(
        num_scalar_prefetch=1,
        grid=(nb,),
        in_specs=(_page_specs(n_pages) * 2 + [row_spec, row_spec] + [_const_spec(a.shape) for a in cw]
                  + [pl.BlockSpec((1, 1, q.shape[-1]), lambda b, pt: (b, 0, 0)),
                     _const_spec(tcs.shape), _const_spec(ovl.shape), _const_spec(pow2.shape)]),
        out_specs=[pl.BlockSpec((1, N_HEADS, HEAD_DIM), lambda b, pt: (b, 0, 0)),
                   pl.BlockSpec((1, N_KV, LANES), lambda b, pt: (b, 0, 0))],
        scratch_shapes=_compress_scratch(n_pages) * 2,
    )
    return pl.pallas_call(
        functools.partial(_compress_sample_kernel, n_pages=n_pages, cur_blk=past // SLC_BLOCK),
        grid_spec=grid_spec,
        out_shape=[jax.ShapeDtypeStruct((nb, N_HEADS, HEAD_DIM), F32),
                   jax.ShapeDtypeStruct((nb, N_KV, LANES), jnp.int32)],
        compiler_params=_params("parallel"),
        name="compress_select_sample",
    )(page_table, *([cache_k] * n_pages), *([cache_v] * n_pages), new_k, new_v, *cw, q, tcs, ovl, jnp.asarray(pow2))


def _attn_prompt_kernel(q_ref, z0_ref, z1_ref, z2_ref, gt_ref, kc_ref, vc_ref, ks_ref, vs_ref, kw_ref, vw_ref,
                        t0_ref, t1_ref, tc_ref, ovl_ref, y_ref, m0_scr, m1_scr, acc0_scr, acc1_scr, sel_scr):
    m_scrs, acc_scrs = (m0_scr, m1_scr), (acc0_scr, acc1_scr)
    qt = pl.program_id(2)
    q = q_ref[...]
    qf = jnp.concatenate([q[:, r * HEAD_DIM:(r + 1) * HEAD_DIM] for r in range(GROUP_REP)], axis=0)
    qf = (qf * (SCALE * LOG2E)).astype(BF16)
    lane_q = lax.broadcasted_iota(jnp.int32, (1, TQ), 1)
    qpos = qt * TQ + lane_q
    key_l = lax.broadcasted_iota(jnp.int32, (TQ, 1), 0)
    gate_t = gt_ref[...].T
    z_refs = (z0_ref, z1_ref, z2_ref)

    def emit(br, r, o_t, first):
        cs = slice(r * HEAD_DIM, (r + 1) * HEAD_DIM)
        term = o_t.T * z_refs[br][:, cs]
        if first:
            y_ref[:, cs] = term
        else:
            y_ref[:, cs] += term

    n_cmp = kc_ref.shape[1]
    s_all = _dot_nt(kc_ref[0].astype(BF16), qf)
    vc_t = vc_ref[0].T.astype(BF16)
    cmp_end = lax.broadcasted_iota(jnp.int32, (n_cmp, 1), 0) * CMP_STRIDE + (CMP_LEN - 1)
    vis = qpos >= cmp_end
    any_vis = qpos >= CMP_LEN - 1
    row0 = pl.multiple_of(TQ // 2 - (TQ // CMP_STRIDE) * qt, CMP_STRIDE)
    psum = jnp.zeros((n_cmp, TQ), F32)
    for r in range(GROUP_REP):
        s = s_all[:, r * TQ:(r + 1) * TQ] + tc_ref[r, pl.ds(row0, n_cmp), :]
        s = jnp.where(vis, s, NEG_INF)
        e = jnp.exp2(s - jnp.max(s, axis=0, keepdims=True))
        p = jnp.where(any_vis, e / jnp.sum(e, axis=0, keepdims=True), 0.0)
        psum = psum + p
        emit(0, r, _dot(vc_t, p.astype(BF16)) * gate_t[r:r + 1, :], True)

    n_slc = ovl_ref.shape[0]
    psum_hi = psum.astype(BF16)
    imp = _dot(ovl_ref[...], psum_hi) + _dot(ovl_ref[...], (psum - psum_hi.astype(F32)).astype(BF16))
    blk = lax.broadcasted_iota(jnp.int32, (n_slc, 1), 0)
    cur = lax.shift_right_logical(qpos, int(math.log2(SLC_BLOCK)))
    forced = (blk == cur) | (blk == 0)
    imp = jnp.where(forced, FORCE_SCORE, jnp.where(blk <= cur, imp, -FORCE_SCORE))
    rank = jnp.zeros((n_slc, TQ), F32)
    for k in range(n_slc):
        rk = imp[k:k + 1, :]
        rank = rank + jnp.where(rk > imp, 1.0, jnp.where((rk == imp) & (blk > k), 1.0, 0.0))
    sel = jnp.where(rank < min(N_SEL, n_slc), 1.0, 0.0)
    blk_per_tile = TQ // SLC_BLOCK
    for t in range(n_slc // blk_per_tile):
        sel_scr[t, 0:blk_per_tile, :] = sel[t * blk_per_tile:(t + 1) * blk_per_tile, :]

    ones_rows = jnp.ones((ACC_ROWS - HEAD_DIM, TQ), BF16)

    def attend(st, k_ref, v_ref, tiles, use_sel, first):
        k_ts, v_ts, masks = [], [], []
        for kt, kind in tiles:
            start = pl.multiple_of(kt * TQ, TQ)
            k_ts.append(k_ref[pl.ds(start, TQ), :])
            v_t = v_ref[pl.ds(start, TQ), :].T.astype(BF16)
            v_ts.append(jnp.concatenate([v_t, ones_rows], axis=0))
            mask = None
            if kind == "diag":
                mask = key_l <= lane_q
            elif kind == "winfar":
                mask = key_l > lane_q
            if use_sel:
                sel4 = sel_scr[kt, 0:blk_per_tile, :]
                selm = jnp.concatenate(
                    [jnp.broadcast_to(sel4[j:j + 1, :], (SLC_BLOCK, TQ)) for j in range(blk_per_tile)], axis=0) > 0.5
                mask = selm if mask is None else mask & selm
            masks.append(mask)
        for r in range(GROUP_REP):
            sl = slice(r * TQ, (r + 1) * TQ)
            if r % HEADS_PER_DOT == 0:
                s_parts = [_dot_nt(k_t, qf[r * TQ:(r + HEADS_PER_DOT) * TQ]) for k_t in k_ts]
            sp = slice((r % HEADS_PER_DOT) * TQ, (r % HEADS_PER_DOT + 1) * TQ)
            ss = []
            for (kt, kind), s_part, mask in zip(tiles, s_parts, masks):
                s = s_part[:, sp]
                if kind == "diag":
                    s = s + t0_ref[r]
                elif kind == "near":
                    s = s + t1_ref[r]
                if mask is not None:
                    s = jnp.where(mask, s, NEG_INF)
                ss.append(s)
            mx = jnp.max(ss[0], axis=0, keepdims=True)
            for s in ss[1:]:
                mx = jnp.maximum(mx, jnp.max(s, axis=0, keepdims=True))
            if first:
                m_new = mx
                upd = None
            else:
                m_old = m_scrs[st][:, sl]
                m_new = jnp.maximum(m_old, mx)
                upd = jnp.exp2(m_old - m_new) * acc_scrs[st][:, sl]
            for s, v_t in zip(ss, v_ts):
                pv = _dot(v_t, jnp.exp2(s - m_new).astype(BF16))
                upd = pv if upd is None else upd + pv
            acc_scrs[st][:, sl] = upd
            m_scrs[st][:, sl] = m_new

    attend(0, ks_ref, vs_ref, [(qt, "diag")], True, True)
    attend(1, kw_ref, vw_ref, [(qt, "diag")], False, True)

    @pl.when(qt >= 1)
    def _():
        attend(0, ks_ref, vs_ref, [(qt - 1, "near")], True, False)
        attend(1, kw_ref, vw_ref, [(qt - 1, "near")], False, False)

    @pl.when(qt >= 2)
    def _():
        attend(1, kw_ref, vw_ref, [(qt - 2, "winfar")], False, False)
        n_far = qt - 1

        def far_pair(i, carry):
            attend(0, ks_ref, vs_ref, [(2 * i, "far"), (2 * i + 1, "far")], True, False)
            return carry

        lax.fori_loop(0, lax.shift_right_logical(n_far, 1), far_pair, 0)

        @pl.when((n_far & 1) == 1)
        def _():
            attend(0, ks_ref, vs_ref, [(n_far - 1, "far")], True, False)

    for st in range(2):
        br = st + 1
        for r in range(GROUP_REP):
            sl = slice(r * TQ, (r + 1) * TQ)
            w = gate_t[br * GROUP_REP + r:br * GROUP_REP + r + 1, :] / acc_scrs[st][HEAD_DIM:HEAD_DIM + 1, sl]
            emit(br, r, acc_scrs[st][0:HEAD_DIM, sl] * w, False)


def _attn_prompt(proj, kc, vc, k_slc, v_slc, k_win, v_win, t0, t1, tc, ovl_t, *, bsz, t):
    nq = t // TQ
    gw = GROUP_REP * HEAD_DIM
    assert TQ // 2 - (TQ // CMP_STRIDE) * (nq - 1) >= 0 and kc.shape[1] == N_KV * (t // CMP_STRIDE)
    n_cmp = t // CMP_STRIDE
    d_att = N_HEADS * HEAD_DIM
    zoff = d_att // gw
    row = lambda b, g, i: b * nq + i
    in_specs = [
        pl.BlockSpec((TQ, gw), lambda b, g, i: (row(b, g, i), g)),
    ] + [
        pl.BlockSpec((TQ, gw), lambda b, g, i, br=br: (row(b, g, i), zoff * (1 + br) + g)) for br in range(N_BRANCH)
    ] + [
        pl.BlockSpec((TQ, LANES), lambda b, g, i: (row(b, g, i), (1 + N_BRANCH) * d_att // LANES + g)),
        pl.BlockSpec((1, n_cmp, HEAD_DIM), lambda b, g, i: (b, g, 0)),
        pl.BlockSpec((1, n_cmp, HEAD_DIM), lambda b, g, i: (b, g, 0)),
    ] + [pl.BlockSpec((None, None, t, HEAD_DIM), lambda b, g, i: (b, g, 0, 0))] * 4 + [
        pl.BlockSpec((GROUP_REP, TQ, TQ), lambda b, g, i: (g, 0, 0))] * 3 + [
        pl.BlockSpec(ovl_t.shape, lambda b, g, i: (0, 0)),
    ]
    return pl.pallas_call(
        _attn_prompt_kernel,
        grid=(bsz, N_KV, nq),
        in_specs=in_specs,
        out_specs=pl.BlockSpec((TQ, gw), lambda b, g, i: (row(b, g, i), g)),
        out_shape=jax.ShapeDtypeStruct((bsz * t, d_att), F32),
        scratch_shapes=[
            pltpu.VMEM((1, GROUP_REP * TQ), F32),
            pltpu.VMEM((1, GROUP_REP * TQ), F32),
            pltpu.VMEM((ACC_ROWS, GROUP_REP * TQ), F32),
            pltpu.VMEM((ACC_ROWS, GROUP_REP * TQ), F32),
            pltpu.VMEM((t // TQ, SUBLANES, TQ), F32),
        ],
        compiler_params=_params("parallel", "parallel", "arbitrary"),
        name="nsa_attention_prompt",
    )(proj, proj, proj, proj, proj, kc, vc, k_slc, v_slc, k_win, v_win, t0, t1, tc, ovl_t)


def _attn_sample_kernel(bits_ref, pt_ref, *refs, n_pages):
    kp, vp = refs[:n_pages], refs[n_pages:2 * n_pages]
    (kwin_ref, vwin_ref, nks_ref, nvs_ref, nkw_ref, nvw_ref, q_ref, oc_ref, ts_ref, tw_ref,
     y_ref, kwo_ref, vwo_ref) = refs[2 * n_pages:]
    b = pl.program_id(0)
    rows_pp = PAGE_SIZE * N_KV
    blk_rows = SLC_BLOCK * N_KV
    d_att = N_HEADS * HEAD_DIM
    n_win_rows = kwin_ref.shape[1]
    head_rows = functools.partial(_head_rows, q_ref)

    for src, new, dst in ((kwin_ref, nkw_ref, kwo_ref), (vwin_ref, nvw_ref, vwo_ref)):
        dst[0, 0:n_win_rows - N_KV, :] = src[0, N_KV:n_win_rows, :]
        dst[0, n_win_rows - N_KV:n_win_rows, :] = jnp.concatenate(
            [new[0, :, g * HEAD_DIM:(g + 1) * HEAD_DIM] for g in range(N_KV)], axis=0)

    def group_rows(ref):
        return jnp.concatenate([ref[0, :, (h // GROUP_REP) * HEAD_DIM:(h // GROUP_REP + 1) * HEAD_DIM]
                                for h in range(N_HEADS)], axis=0)

    def softmax_pv(s_tiles, v_tiles, s_new, v_new):
        m = s_new
        for s in s_tiles:
            m = jnp.maximum(m, jnp.max(s, axis=-1, keepdims=True))
        p_new = jnp.exp(s_new - m)
        l = p_new
        acc = p_new * v_new
        for s, v in zip(s_tiles, v_tiles):
            p = jnp.exp(s - m)
            l = l + jnp.sum(p, axis=-1, keepdims=True)
            acc = acc + _dot(p.astype(BF16), v.astype(BF16))
        return acc / l

    q32 = head_rows(0) * SCALE
    q16 = q32.astype(BF16)
    head = lax.broadcasted_iota(jnp.int32, (N_HEADS, 1), 0)
    bits = jnp.zeros((N_HEADS, 1), jnp.int32)
    for g in range(N_KV):
        bits = jnp.where((head >= g * GROUP_REP) & (head < (g + 1) * GROUP_REP), bits_ref[b, g], bits)
    lane_blk = lax.broadcasted_iota(jnp.int32, (1, rows_pp), 1) // blk_rows
    s_tiles, v_tiles = [], []
    for p in range(n_pages):
        s = _dot_nt(q16, kp[p][0].astype(BF16)) + ts_ref[:, p * rows_pp:(p + 1) * rows_pp]
        shift = jnp.broadcast_to(lane_blk + p * (rows_pp // blk_rows), s.shape)
        sel = lax.shift_right_logical(jnp.broadcast_to(bits, s.shape), shift) & 1
        s_tiles.append(jnp.where(sel == 1, s, NEG_INF))
        v_tiles.append(vp[p][0])
    tail = n_pages * rows_pp
    s_new = jnp.sum(q32 * group_rows(nks_ref), axis=-1, keepdims=True) + ts_ref[:, tail:tail + 1]
    o_slc = softmax_pv(s_tiles, v_tiles, s_new, group_rows(nvs_ref))
    s_tiles, v_tiles = [], []
    for p in range(n_win_rows // rows_pp):
        rows = slice(p * rows_pp, (p + 1) * rows_pp)
        s_tiles.append(_dot_nt(q16, kwin_ref[0, rows, :].astype(BF16)) + tw_ref[:, rows])
        v_tiles.append(vwin_ref[0, rows, :])
    s_new = jnp.sum(q32 * group_rows(nkw_ref), axis=-1, keepdims=True) + tw_ref[:, n_win_rows:n_win_rows + 1]
    o_win = softmax_pv(s_tiles, v_tiles, s_new, group_rows(nvw_ref))
    y = jnp.zeros((N_HEADS, HEAD_DIM), F32)
    gate0 = (1 + N_BRANCH) * d_att
    for br, o in enumerate((oc_ref[0], o_slc, o_win)):
        cols = [gate0 + (h // GROUP_REP) * LANES + br * GROUP_REP + h % GROUP_REP for h in range(N_HEADS)]
        gate = jnp.concatenate([q_ref[0, :, c:c + 1] for c in cols], axis=0)
        y = y + gate * o * head_rows((1 + br) * d_att)
    y_ref[0] = y


def _attn_sample(cache_k, cache_v, page_table, bits, k_win, v_win, new_rows, proj, o_cmp, ts, tw):
    nb, n_pages = page_table.shape
    rows_pp = PAGE_SIZE * N_KV
    win_spec = pl.BlockSpec((1, k_win.shape[1], HEAD_DIM), lambda b, bits, pt: (b, 0, 0))
    row_spec = lambda w: pl.BlockSpec((1, 1, w), lambda b, bits, pt: (b, 0, 0))
    const = lambda a: pl.BlockSpec(a.shape, lambda b, bits, pt: (0,) * a.ndim)
    page_specs = [pl.BlockSpec((1, rows_pp, HEAD_DIM), lambda b, bits, pt, p=p: (pt[b, p], 0, 0)) for p in range(n_pages)]
    grid_spec = pltpu.PrefetchScalarGridSpec(
        num_scalar_prefetch=2,
        grid=(nb,),
        in_specs=(page_specs * 2 + [win_spec] * 2
                  + [row_spec(N_KV * HEAD_DIM)] * 4
                  + [row_spec(proj.shape[-1]),
                     pl.BlockSpec((1, N_HEADS, HEAD_DIM), lambda b, bits, pt: (b, 0, 0)),
                     const(ts), const(tw)]),
        out_specs=[pl.BlockSpec((1, N_HEADS, HEAD_DIM), lambda b, bits, pt: (b, 0, 0)), win_spec, win_spec],
    )
    return pl.pallas_call(
        functools.partial(_attn_sample_kernel, n_pages=n_pages),
        grid_spec=grid_spec,
        out_shape=[jax.ShapeDtypeStruct((nb, N_HEADS, HEAD_DIM), F32),
                   jax.ShapeDtypeStruct(k_win.shape, F32), jax.ShapeDtypeStruct(v_win.shape, F32)],
        compiler_params=_params("parallel"),
        name="nsa_attention_sample",
    )(bits, page_table, *([cache_k] * n_pages), *([cache_v] * n_pages), k_win, v_win, *new_rows,
      proj, o_cmp, ts, tw)


def _overlap(n_cmp, n_slc):
    cs = np.arange(n_cmp) * CMP_STRIDE
    bs = np.arange(n_slc) * SLC_BLOCK
    ov = np.clip(np.minimum(cs[:, None] + CMP_LEN, bs[None, :] + SLC_BLOCK) - np.maximum(cs[:, None], bs[None, :]), 0, None)
    return (ov / CMP_LEN).astype(np.float32)


def _compress_weights(w1, w2, pe):
    half = CMP_STRIDE * HEAD_DIM
    wcat = jnp.concatenate([w1[:half], w1[half:]], axis=1).astype(BF16)
    pe8 = jnp.pad(pe.reshape(2, half), ((0, SUBLANES - 2), (0, 0)))
    return wcat, w2.astype(BF16), pe8


def kernel(x_prompt, x_sample, state_pool, cache_k_cmp, cache_v_cmp, cache_k_slc, cache_v_slc, state_k_win, state_v_win, page_table, c_prompt, c_sample, g_norm, w_ada, b_ada, w_in_a, w_grp, pool_scale, w_out_a, g_kv, w_kv, pe_k, w_ck1, w_ck2, pe_v, w_cv1, w_cv2, rel_bias, w_in_b, w_out_b, g_final):
    bsz, t, d = x_prompt.shape
    nb = x_sample.shape[0]
    n_pages = page_table.shape[1]
    past = n_pages * PAGE_SIZE
    d_att = N_HEADS * HEAD_DIM
    d_kv = N_KV * HEAD_DIM
    n_a = w_in_a.shape[0]
    assert n_a == 1 and w_in_b.shape[0] == 1 and x_sample.shape[1] == 1
    assert t % TQ == 0 and t >= WINDOW and past % PAGE_SIZE == 0 and state_k_win.shape[1] == WINDOW

    w_in_a16 = w_in_a[0].astype(BF16)
    w_grp16 = w_grp[0].astype(BF16)
    w_out_a16 = w_out_a[0].astype(BF16)
    w_kv16 = w_kv.astype(BF16)
    w_out_b16 = w_out_b[0].astype(BF16)
    n_qz = (1 + N_BRANCH) * d_att
    wg = w_in_b[0][:, n_qz:].reshape(d, N_BRANCH, N_KV, GROUP_REP).transpose(0, 2, 1, 3)
    wg = jnp.pad(wg.reshape(d, N_KV, N_BRANCH * GROUP_REP), ((0, 0), (0, 0), (0, LANES - N_BRANCH * GROUP_REP)))
    n_proj = n_qz + TN_PROJ
    wg = jnp.pad(wg.reshape(d, N_KV * LANES), ((0, 0), (0, TN_PROJ - N_KV * LANES)))
    w_in_b16 = jnp.concatenate([w_in_b[0][:, :n_qz], wg], axis=1).astype(BF16)
    nsa_proj = functools.partial(_nm_matmul, w=w_in_b16, n_cols=n_proj, tn=TN_PROJ, silu_cols=d_att, sigmoid_cols=n_qz)
    cw_k = _compress_weights(w_ck1, w_ck2, pe_k)
    cw_v = _compress_weights(w_cv1, w_cv2, pe_v)
    cw = cw_k + cw_v

    mod = _ada(jnp.concatenate([c_prompt, c_sample], axis=0), w_ada, b_ada)

    def modulation(l, lo, hi, per_row):
        parts = [mod[l, lo:hi, k * d:(k + 1) * d] for k in range(3)]
        return [p[None] if per_row else p[:, None] for p in parts]

    t0, t1, tc, ts, tw, tcs = _bias_tables(rel_bias, past=past, n_cmp_s=past // CMP_STRIDE)
    ts, tw, tcs = ts.reshape(N_HEADS, -1), tw.reshape(N_HEADS, -1), tcs.reshape(N_HEADS, -1)

    shift, scale, gate = modulation(0, 0, bsz, False)
    uz = _nm_matmul(x_prompt, g_norm[0], scale, shift, w_in_a16, n_cols=2 * d, tm=TM_PROJ, tn=TN_PROJ, name="in_proj_pool_prompt")
    pool_p = uz[:, t - POOL_BUF:, :d][None]
    x1 = _pool_prompt(uz, w_grp16, pool_scale[0], w_out_a16, x_prompt, gate, tm=256)
    kv_p = _kv_proj(x1, g_kv, w_kv16, tm=256, name="kv_proj_prompt")
    heads = lambda a: a.reshape(a.shape[0], -1, N_KV, HEAD_DIM)
    kv_state_p = [heads(a) for a in kv_p[:4]] + [heads(a[:, (t - WINDOW) * N_KV:]) for a in kv_p[4:6]]
    kc_p, vc_p = _compress_prompt(kv_p[0], kv_p[1], cw)
    shift, scale, gate = modulation(1, 0, bsz, False)
    proj = nsa_proj(x1, g_norm[1], scale, shift, tm=TM_PROJ, name="in_proj_nsa_prompt")
    ovl_t = jnp.asarray(np.pad(_overlap(t // CMP_STRIDE - 1, t // SLC_BLOCK), ((0, 1), (0, 0))).T).astype(BF16)
    flat = lambda a: a.reshape(bsz * t, a.shape[-1])
    y1 = _attn_prompt(flat(proj), kc_p, vc_p, kv_p[6], kv_p[7], kv_p[8], kv_p[9],
                      t0, t1, tc, ovl_t, bsz=bsz, t=t)
    y_prompt = _proj_residual(y1.reshape(bsz, t, d_att), w_out_b16, x1, gate, g_final, tm=512, final_norm=True,
                              name="out_proj_nsa_prompt")

    xs = x_sample.reshape(1, nb, d)
    shift, scale, gate = modulation(0, bsz, bsz + nb, True)
    uz_s = _nm_matmul(xs, g_norm[0], scale, shift, w_in_a16, n_cols=2 * d, tm=nb, tn=512, name="in_proj_pool_sample")
    pool_s = jnp.concatenate([state_pool[:, :, 1:], uz_s[0, :, None, :d][None]], axis=2)
    y0_s = _pool_sample(uz_s, state_pool[0], w_grp16, pool_scale[0], qpos=past, tb=min(32, nb))
    x1_s = _proj_residual(y0_s, w_out_a16, xs, gate, g_final, tm=nb, final_norm=False, name="out_proj_pool_sample")
    kv_s = _kv_proj(x1_s, g_kv, w_kv16, tm=nb, name="kv_proj_sample")
    new_rows = [a.reshape(nb, 1, d_kv) for a in kv_s[:2 * N_BRANCH]]
    shift, scale, gate = modulation(1, bsz, bsz + nb, True)
    proj_s = nsa_proj(x1_s, g_norm[1], scale, shift, tm=nb, name="in_proj_nsa_sample").reshape(nb, 1, n_proj)
    n_cmp_s = past // CMP_STRIDE
    n_slc_s = past // SLC_BLOCK + 1
    ovl_s = jnp.asarray(np.tile(np.pad(_overlap(n_cmp_s, n_slc_s), ((0, 0), (0, LANES - n_slc_s))), (N_KV, 1))).astype(BF16)
    rows_of = lambda c: c.reshape(c.shape[0], PAGE_SIZE * N_KV, HEAD_DIM)
    o_cmp, bits = _compress_sample(rows_of(cache_k_cmp), rows_of(cache_v_cmp), page_table, new_rows[0], new_rows[1], cw,
                                   proj_s, tcs, ovl_s, past=past)
    y1_s, k_win_s, v_win_s = _attn_sample(
        rows_of(cache_k_slc), rows_of(cache_v_slc), page_table, bits[:, :, 0],
        state_k_win.reshape(nb, WINDOW * N_KV, HEAD_DIM), state_v_win.reshape(nb, WINDOW * N_KV, HEAD_DIM),
        new_rows[2:], proj_s, o_cmp, ts, tw)
    y_sample = _proj_residual(y1_s.reshape(1, nb, d_att), w_out_b16, x1_s, gate, g_final, tm=nb, final_norm=True,
                              name="out_proj_nsa_sample").reshape(nb, 1, d)
    new4 = [a.reshape(nb, 1, N_KV, HEAD_DIM) for a in kv_s[:2 * N_BRANCH]]
    kv_state_s = new4[:4] + [k_win_s.reshape(state_k_win.shape), v_win_s.reshape(state_v_win.shape)]

    return (y_prompt, y_sample, pool_p, *kv_state_p, pool_s, *kv_state_s)
```

```python
import functools
import math

import numpy as np
import jax
import jax.numpy as jnp
from jax import lax
from jax.experimental import pallas as pl
from jax.experimental.pallas import tpu as pltpu

F32 = jnp.float32
BF16 = jnp.bfloat16

PAGE_SIZE = 128
POOL_WINDOWS = (2, 4, 8, 16)
POOL_BUF = max(POOL_WINDOWS) - 1
HEAD_DIM = 128
N_KV = 4
GROUP_REP = 4
N_HEADS = N_KV * GROUP_REP
N_BRANCH = 3
CMP_LEN = 32
CMP_STRIDE = 16
SLC_BLOCK = 64
N_SEL = 8
WINDOW = 512
N_BUCKETS = 32
MAX_DISTANCE = 128
RMS_EPS = 1e-6
SCALE = HEAD_DIM ** -0.5
LOG2E = math.log2(math.e)
NEG_INF = -1e30
FORCE_SCORE = 1e9

LANES = 128
SUBLANES = 8
VMEM_LIMIT_BYTES = 56 * 1024 * 1024
TQ = 256
COMPRESS_CHUNK_PAGES = 4
ACC_ROWS = HEAD_DIM + 16
TN_PROJ = 1024
TM_PROJ = 1024
HALO = 16


def _bucket_starts():
    max_exact = N_BUCKETS // 2
    d = np.arange(0, MAX_DISTANCE + 1)
    large = max_exact + np.floor(
        np.log(np.maximum(d, max_exact) / max_exact) / math.log(MAX_DISTANCE / max_exact) * (N_BUCKETS - max_exact)
    ).astype(np.int64)
    bucket = np.where(d < max_exact, d, np.minimum(large, N_BUCKETS - 1))
    return [int(np.argmax(bucket >= k)) for k in range(N_BUCKETS)]


BUCKET_STARTS = _bucket_starts()
FAR_DIST = BUCKET_STARTS[-1]


def _params(*sem):
    return pltpu.CompilerParams(dimension_semantics=sem, vmem_limit_bytes=VMEM_LIMIT_BYTES)


def _silu(x):
    return x * jax.nn.sigmoid(x)


def _rms(x, g):
    return x * lax.rsqrt(jnp.mean(x * x, axis=-1, keepdims=True) + RMS_EPS) * g


def _dot(a, b):
    return jnp.dot(a, b, preferred_element_type=F32)


def _dot_nt(a, b):
    return lax.dot_general(a, b, (((1,), (1,)), ((), ())), preferred_element_type=F32)


def _ada_kernel(c_ref, w_ref, b_ref, o_ref):
    a = _silu(c_ref[...]).astype(BF16)
    o_ref[0] = _dot(a, w_ref[0].astype(BF16)) + b_ref[0]


def _ada(c_all, w_ada, b_ada):
    depth, d, n = w_ada.shape
    m = c_all.shape[0]
    tn = 512
    return pl.pallas_call(
        _ada_kernel,
        grid=(depth, n // tn),
        in_specs=[
            pl.BlockSpec((m, d), lambda l, j: (0, 0)),
            pl.BlockSpec((1, d, tn), lambda l, j: (l, 0, j)),
            pl.BlockSpec((1, 1, tn), lambda l, j: (l, 0, j)),
        ],
        out_specs=pl.BlockSpec((1, m, tn), lambda l, j: (l, 0, j)),
        out_shape=jax.ShapeDtypeStruct((depth, m, n), F32),
        compiler_params=_params("parallel", "parallel"),
        name="ada_modulation",
    )(c_all, w_ada, b_ada.reshape(depth, 1, n))


def _nm_matmul_kernel(x_ref, g_ref, sc_ref, sh_ref, w_ref, o_ref, h_scr, *, silu_from, sigmoid_from):
    j = pl.program_id(2)

    @pl.when(j == 0)
    def _():
        y = _rms(x_ref[0], g_ref[...])
        h_scr[...] = (y * (1.0 + sc_ref[0]) + sh_ref[0]).astype(BF16)

    r = _dot(h_scr[...], w_ref[...])
    if silu_from is None:
        o_ref[0] = r
    else:
        sig = jax.nn.sigmoid(r)
        o_ref[0] = jnp.where(j >= sigmoid_from, sig, jnp.where(j >= silu_from, r * sig, r))


def _mod_spec(arr, tm):
    d = arr.shape[-1]
    if arr.shape[1] == 1:
        return pl.BlockSpec((1, 1, d), lambda b, i, *_: (b, 0, 0))
    return pl.BlockSpec((1, tm, d), lambda b, i, *_: (b, i, 0))


def _nm_matmul(x, g, scale, shift, w, *, n_cols, tm, tn, silu_cols=None, sigmoid_cols=None, name):
    bsz, rows, d = x.shape
    if silu_cols is None and sigmoid_cols is None:
        silu_from = sigmoid_from = None
    else:
        sigmoid_from = n_cols // tn if sigmoid_cols is None else sigmoid_cols // tn
        silu_from = sigmoid_from if silu_cols is None else silu_cols // tn
    return pl.pallas_call(
        functools.partial(_nm_matmul_kernel, silu_from=silu_from, sigmoid_from=sigmoid_from),
        grid=(bsz, rows // tm, n_cols // tn),
        in_specs=[
            pl.BlockSpec((1, tm, d), lambda b, i, j: (b, i, 0)),
            pl.BlockSpec((1, d), lambda b, i, j: (0, 0)),
            _mod_spec(scale, tm),
            _mod_spec(shift, tm),
            pl.BlockSpec((d, tn), lambda b, i, j: (0, j)),
        ],
        out_specs=pl.BlockSpec((1, tm, tn), lambda b, i, j: (b, i, j)),
        out_shape=jax.ShapeDtypeStruct((bsz, rows, n_cols), F32),
        scratch_shapes=[pltpu.VMEM((tm, d), BF16)],
        compiler_params=_params("parallel", "parallel", "arbitrary"),
        name=name,
    )(x, g.reshape(1, d), scale, shift, w)


def _pool_mix(pooled_fn, z, wg_ref, ps_ref, store):
    grp = z.shape[-1] // len(POOL_WINDOWS)
    for gi, w in enumerate(POOL_WINDOWS):
        cs = slice(gi * grp, (gi + 1) * grp)
        mixed = _dot(pooled_fn(gi, w, cs).astype(BF16), wg_ref[gi])
        store(cs, mixed * ps_ref[:, cs] * _silu(z[:, cs]))


def _pool_prompt_kernel(u_ref, z_ref, halo_ref, wg_ref, ps_ref, wo_ref, x_ref, gate_ref, o_ref,
                        ue_scr, a_scr, b_scr, y_scr):
    i = pl.program_id(1)
    tm = u_ref.shape[1]
    top = SUBLANES + HALO
    n = top + tm
    u = u_ref[0]
    ue_scr[0:SUBLANES, :] = jnp.zeros((SUBLANES, u.shape[1]), F32)
    ue_scr[SUBLANES:top, :] = jnp.where(i > 0, halo_ref[0], 0.0)
    ue_scr[top:n, :] = u
    zero8 = jnp.zeros((SUBLANES, a_scr.shape[1]), F32)
    a_scr[0:SUBLANES, :] = zero8
    b_scr[0:SUBLANES, :] = zero8
    pos = i * tm + lax.broadcasted_iota(jnp.int32, (tm, 1), 0)

    def pooled(gi, w, cs):
        src = lambda lo, hi: ue_scr[lo:hi, cs]
        k = 1
        for dst in (a_scr, b_scr, a_scr):
            if 2 * k >= w:
                break
            dst[SUBLANES:n, :] = src(SUBLANES, n) + src(SUBLANES - k, n - k)
            src = lambda lo, hi, dst=dst: dst[lo:hi, :]
            k *= 2
        acc = src(top, n) + src(top - k, n - k)
        cnt = jnp.minimum(pos + 1, w).astype(F32)
        return acc / cnt - u[:, cs]

    def store(cs, val):
        y_scr[:, cs] = val.astype(BF16)

    _pool_mix(pooled, z_ref[0], wg_ref, ps_ref, store)
    o_ref[0] = x_ref[0] + gate_ref[0] * _dot(y_scr[...], wo_ref[...])


def _pool_prompt(uz, w_grp, pool_scale, w_out, x, gate, *, tm):
    bsz, t, d2 = uz.shape
    d = d2 // 2
    ng, grp, _ = w_grp.shape
    return pl.pallas_call(
        _pool_prompt_kernel,
        grid=(bsz, t // tm),
        in_specs=[
            pl.BlockSpec((1, tm, d), lambda b, i: (b, i, 0)),
            pl.BlockSpec((1, tm, d), lambda b, i: (b, i, 1)),
            pl.BlockSpec((1, HALO, d), lambda b, i: (b, jnp.maximum(i * (tm // HALO) - 1, 0), 0)),
            pl.BlockSpec((ng, grp, grp), lambda b, i: (0, 0, 0)),
            pl.BlockSpec((1, d), lambda b, i: (0, 0)),
            pl.BlockSpec(w_out.shape, lambda b, i: (0, 0)),
            pl.BlockSpec((1, tm, d), lambda b, i: (b, i, 0)),
            _mod_spec(gate, tm),
        ],
        out_specs=pl.BlockSpec((1, tm, d), lambda b, i: (b, i, 0)),
        out_shape=jax.ShapeDtypeStruct((bsz, t, d), F32),
        scratch_shapes=[pltpu.VMEM((SUBLANES + HALO + tm, d), F32),
                        pltpu.VMEM((SUBLANES + HALO + tm, grp), F32),
                        pltpu.VMEM((SUBLANES + HALO + tm, grp), F32),
                        pltpu.VMEM((tm, d), BF16)],
        compiler_params=_params("parallel", "parallel"),
        name="pool_mixer_prompt",
    )(uz, uz, uz, w_grp, pool_scale.reshape(1, d), w_out, x, gate)


def _pool_sample_kernel(u_ref, z_ref, pre_ref, wg_ref, ps_ref, y_ref, *, qpos):
    u = u_ref[0]

    def pooled(gi, w, cs):
        acc = u[:, cs]
        for k in range(1, w):
            acc = acc + pre_ref[:, POOL_BUF - k, cs]
        return acc / float(min(qpos + 1, w)) - u[:, cs]

    def store(cs, val):
        y_ref[0, :, cs] = val

    _pool_mix(pooled, z_ref[0], wg_ref, ps_ref, store)


def _pool_sample(uz, prefix, w_grp, pool_scale, *, qpos, tb):
    _, nb, d2 = uz.shape
    d = d2 // 2
    ng, grp, _ = w_grp.shape
    return pl.pallas_call(
        functools.partial(_pool_sample_kernel, qpos=qpos),
        grid=(nb // tb,),
        in_specs=[
            pl.BlockSpec((1, tb, d), lambda i: (0, i, 0)),
            pl.BlockSpec((1, tb, d), lambda i: (0, i, 1)),
            pl.BlockSpec((tb, POOL_BUF, d), lambda i: (i, 0, 0)),
            pl.BlockSpec((ng, grp, grp), lambda i: (0, 0, 0)),
            pl.BlockSpec((1, d), lambda i: (0, 0)),
        ],
        out_specs=pl.BlockSpec((1, tb, d), lambda i: (0, i, 0)),
        out_shape=jax.ShapeDtypeStruct((1, nb, d), F32),
        compiler_params=_params("parallel"),
        name="pool_mixer_sample",
    )(uz, uz, prefix, w_grp, pool_scale.reshape(1, d))


def _proj_residual_kernel(y_ref, w_ref, x_ref, gate_ref, g_ref, o_ref, *, final_norm):
    xo = x_ref[0] + gate_ref[0] * _dot(y_ref[0].astype(BF16), w_ref[...])
    o_ref[0] = _rms(xo, g_ref[...]) if final_norm else xo


def _proj_residual(y, w, x, gate, g_final, *, tm, final_norm, name):
    bsz, rows, d = x.shape
    dy = y.shape[-1]
    return pl.pallas_call(
        functools.partial(_proj_residual_kernel, final_norm=final_norm),
        grid=(bsz, rows // tm),
        in_specs=[
            pl.BlockSpec((1, tm, dy), lambda b, i: (b, i, 0)),
            pl.BlockSpec((dy, d), lambda b, i: (0, 0)),
            pl.BlockSpec((1, tm, d), lambda b, i: (b, i, 0)),
            _mod_spec(gate, tm),
            pl.BlockSpec((1, d), lambda b, i: (0, 0)),
        ],
        out_specs=pl.BlockSpec((1, tm, d), lambda b, i: (b, i, 0)),
        out_shape=jax.ShapeDtypeStruct((bsz, rows, d), F32),
        compiler_params=_params("parallel", "parallel"),
        name=name,
    )(y, w, x, gate, g_final.reshape(1, d))


KV_GROUP_COPIES = ((2, BF16), (3, F32), (4, BF16), (5, F32))


def _kv_kernel(x_ref, g_ref, w_ref, *o_refs):
    h = _rms(x_ref[0], g_ref[...]).astype(BF16)
    n_out = 2 * N_BRANCH
    tm = x_ref.shape[1]
    n = N_KV * HEAD_DIM
    copies = dict((o, (o_refs[n_out + i], dt)) for i, (o, dt) in enumerate(KV_GROUP_COPIES))
    for o in range(n_out):
        r = _dot(h, w_ref[:, o * n:(o + 1) * n])
        for g in range(N_KV):
            rg = r[:, g * HEAD_DIM:(g + 1) * HEAD_DIM]
            o_refs[o][0, pl.ds(g, tm, stride=N_KV), :] = rg
            if o in copies:
                c_ref, dt = copies[o]
                c_ref[0, g] = rg.astype(dt)


def _kv_proj(x, g_kv, w_kv, *, tm, name):
    bsz, rows, d = x.shape
    n_out = 2 * N_BRANCH
    return pl.pallas_call(
        _kv_kernel,
        grid=(bsz, rows // tm),
        in_specs=[
            pl.BlockSpec((1, tm, d), lambda b, i: (b, i, 0)),
            pl.BlockSpec((1, d), lambda b, i: (0, 0)),
            pl.BlockSpec(w_kv.shape, lambda b, i: (0, 0)),
        ],
        out_specs=([pl.BlockSpec((1, tm * N_KV, HEAD_DIM), lambda b, i: (b, i, 0))] * n_out
                   + [pl.BlockSpec((1, N_KV, tm, HEAD_DIM), lambda b, i: (b, 0, i, 0))] * len(KV_GROUP_COPIES)),
        out_shape=([jax.ShapeDtypeStruct((bsz, rows * N_KV, HEAD_DIM), F32)] * n_out
                   + [jax.ShapeDtypeStruct((bsz, N_KV, rows, HEAD_DIM), dt) for _, dt in KV_GROUP_COPIES]),
        compiler_params=_params("parallel", "parallel"),
        name=name,
    )(x, g_kv.reshape(1, d), w_kv)


def _bias_lookup(rb_ref, h, dist):
    acc = jnp.full(dist.shape, rb_ref[N_BUCKETS - 1, h], F32)
    for k in range(N_BUCKETS - 2, -1, -1):
        acc = jnp.where(dist < BUCKET_STARTS[k + 1], rb_ref[k, h], acc)
    return acc


def _bias_tables_kernel(rb_ref, t0_ref, t1_ref, tc_ref, ts_ref, tw_ref, tcs_ref, *, past):
    h = pl.program_id(0)
    key = lax.broadcasted_iota(jnp.int32, (TQ, TQ), 0)
    qry = lax.broadcasted_iota(jnp.int32, (TQ, TQ), 1)
    far = rb_ref[N_BUCKETS - 1, h]
    rel2 = lambda dist: (_bias_lookup(rb_ref, h, dist) - far) * LOG2E
    t0_ref[0] = rel2(qry - key)
    t1_ref[0] = rel2(TQ + qry - key)
    tc_ref[0] = rel2(qry - CMP_STRIDE * (key - TQ // 2) - (CMP_LEN - 1))
    ls = lax.broadcasted_iota(jnp.int32, ts_ref.shape[1:], 1)
    own = (ls % N_KV) == h // GROUP_REP
    tok = ls // N_KV
    ts_ref[0] = jnp.where(ls == N_KV * past, rb_ref[0, h],
                          jnp.where(own & (tok < past), _bias_lookup(rb_ref, h, past - tok), NEG_INF))
    lw = lax.broadcasted_iota(jnp.int32, tw_ref.shape[1:], 1)
    own = (lw % N_KV) == h // GROUP_REP
    dw = WINDOW - lw // N_KV
    tw_ref[0] = jnp.where(lw == N_KV * WINDOW, rb_ref[0, h],
                          jnp.where(own & (dw > 0) & (dw < WINDOW) & (past - dw >= 0), _bias_lookup(rb_ref, h, dw), NEG_INF))
    lc = lax.broadcasted_iota(jnp.int32, tcs_ref.shape[1:], 1)
    n_cmp = tcs_ref.shape[2] // N_KV
    dc = past - CMP_STRIDE * (lc % n_cmp) - (CMP_LEN - 1)
    tcs_ref[0] = jnp.where((lc // n_cmp == h // GROUP_REP) & (dc >= 0), _bias_lookup(rb_ref, h, dc), NEG_INF)


def _bias_tables(rel_bias, *, past, n_cmp_s):
    nh = rel_bias.shape[1]
    ls = N_KV * past + LANES
    lw = N_KV * WINDOW + LANES
    shapes = [(nh, TQ, TQ)] * 3 + [(nh, 1, ls), (nh, 1, lw), (nh, 1, N_KV * n_cmp_s)]
    return pl.pallas_call(
        functools.partial(_bias_tables_kernel, past=past),
        grid=(nh,),
        in_specs=[pl.BlockSpec(memory_space=pltpu.SMEM)],
        out_specs=[pl.BlockSpec((1,) + s[1:], lambda h: (h, 0, 0)) for s in shapes],
        out_shape=[jax.ShapeDtypeStruct(s, F32) for s in shapes],
        compiler_params=_params("parallel"),
        name="rel_bias_tables",
    )(rel_bias)


def _compress_fill(page_refs, lhs_scr, first_page):
    sub_per_page = PAGE_SIZE // CMP_STRIDE
    pairs = CMP_STRIDE // 2
    rows_pp = sub_per_page * SUBLANES
    even = (lax.broadcasted_iota(jnp.int32, (rows_pp, 1), 0) & (SUBLANES - 1)) < N_KV
    for p, pref in enumerate(page_refs, first_page):
        x4 = pref[0].reshape(sub_per_page, pairs, SUBLANES, HEAD_DIM)
        rs = slice(p * rows_pp, (p + 1) * rows_pp)
        for j in range(pairs):
            t = x4[:, j].reshape(rows_pp, HEAD_DIM)
            lhs_scr[rs, 2 * j * HEAD_DIM:(2 * j + 1) * HEAD_DIM] = jnp.where(even, t, 0.0).astype(BF16)
            lhs_scr[rs, (2 * j + 1) * HEAD_DIM:(2 * j + 2) * HEAD_DIM] = jnp.where(even, 0.0, t).astype(BF16)


def _compress_finish(new_ref, wcat_ref, w2_ref, pe_ref, t_scr, p_scr, *, has_new):
    n_rows = p_scr.shape[0]
    n_sub = n_rows // SUBLANES
    wcat = wcat_ref[...]
    t_scr[n_rows:, :] = jnp.zeros((2 * SUBLANES, 2 * HEAD_DIM), F32)
    t_scr[0:n_rows, :] = t_scr[0:n_rows, :] + t_scr[N_KV:n_rows + N_KV, :]
    if has_new:
        new8 = jnp.concatenate([new_ref[0, :, g * HEAD_DIM:(g + 1) * HEAD_DIM] for g in range(N_KV)]
                               + [jnp.zeros((SUBLANES - N_KV, HEAD_DIM), F32)], axis=0)
        t_scr[n_rows:n_rows + SUBLANES, HEAD_DIM:] = _dot(new8.astype(BF16), wcat[0:HEAD_DIM, HEAD_DIM:])
    pe_r = _dot(pe_ref[...].astype(BF16), wcat)
    pe_const = pe_r[0:1, :HEAD_DIM] + pe_r[1:2, HEAD_DIM:]
    p_scr[...] = t_scr[0:n_rows, :HEAD_DIM] + t_scr[SUBLANES:n_rows + SUBLANES, HEAD_DIM:] + pe_const
    pre = jnp.concatenate([p_scr[pl.ds(g, n_sub, stride=SUBLANES), :] for g in range(N_KV)], axis=0)
    out = _dot(_silu(pre).astype(BF16), w2_ref[...])
    if not has_new:
        blk = lax.broadcasted_iota(jnp.int32, (N_KV * n_sub, 1), 0) & (n_sub - 1)
        out = jnp.where(blk == n_sub - 1, 0.0, out)
    return out


def _compress_kv(k_pages, v_pages, k_new, v_new, k_w, v_w, scr, *, has_new):
    rows_pp = (PAGE_SIZE // CMP_STRIDE) * SUBLANES
    for p0 in range(0, len(k_pages), COMPRESS_CHUNK_PAGES):
        p1 = min(p0 + COMPRESS_CHUNK_PAGES, len(k_pages))
        rs = slice(p0 * rows_pp, p1 * rows_pp)
        _compress_fill(k_pages[p0:p1], scr[0], p0)
        _compress_fill(v_pages[p0:p1], scr[3], p0)
        scr[1][rs, :] = _dot(scr[0][rs, :], k_w[0][...])
        scr[4][rs, :] = _dot(scr[3][rs, :], v_w[0][...])
    kc = _compress_finish(k_new, *k_w, scr[1], scr[2], has_new=has_new)
    vc = _compress_finish(v_new, *v_w, scr[4], scr[5], has_new=has_new)
    return kc, vc


def _compress_scratch(n_pages):
    n_rows = n_pages * (PAGE_SIZE // CMP_STRIDE) * SUBLANES
    return [pltpu.VMEM((n_rows, CMP_STRIDE * HEAD_DIM), BF16),
            pltpu.VMEM((n_rows + 2 * SUBLANES, 2 * HEAD_DIM), F32),
            pltpu.VMEM((n_rows, HEAD_DIM), F32)]


def _compress_prompt_kernel(pt_ref, *refs, n_pages):
    kp, vp = refs[:n_pages], refs[n_pages:2 * n_pages]
    wck, w2k, pek, wcv, w2v, pev, kc_ref, vc_ref, *scr = refs[2 * n_pages:]
    n_sub = n_pages * (PAGE_SIZE // CMP_STRIDE)
    assert n_sub & (n_sub - 1) == 0
    kc_ref[0], vc_ref[0] = _compress_kv(kp, vp, None, None, (wck, w2k, pek), (wcv, w2v, pev), scr, has_new=False)


def _page_specs(n_pages):
    return [pl.BlockSpec((1, PAGE_SIZE * N_KV, HEAD_DIM), lambda b, pt, p=p: (pt[b, p], 0, 0)) for p in range(n_pages)]


def _const_spec(shape):
    return pl.BlockSpec(shape, lambda b, pt: (0,) * len(shape))


def _compress_prompt(k_rows, v_rows, cw):
    bsz = k_rows.shape[0]
    t = k_rows.shape[1] // N_KV
    n_pages = t // PAGE_SIZE
    n_sub = t // CMP_STRIDE
    pt = jnp.arange(bsz * n_pages, dtype=jnp.int32).reshape(bsz, n_pages)
    kp = k_rows.reshape(bsz * n_pages, PAGE_SIZE * N_KV, HEAD_DIM)
    vp = v_rows.reshape(bsz * n_pages, PAGE_SIZE * N_KV, HEAD_DIM)
    wspecs = [_const_spec(a.shape) for a in cw]
    grid_spec = pltpu.PrefetchScalarGridSpec(
        num_scalar_prefetch=1,
        grid=(bsz,),
        in_specs=_page_specs(n_pages) * 2 + wspecs,
        out_specs=[pl.BlockSpec((1, N_KV * n_sub, HEAD_DIM), lambda b, pt: (b, 0, 0))] * 2,
        scratch_shapes=_compress_scratch(n_pages) * 2,
    )
    return pl.pallas_call(
        functools.partial(_compress_prompt_kernel, n_pages=n_pages),
        grid_spec=grid_spec,
        out_shape=[jax.ShapeDtypeStruct((bsz, N_KV * n_sub, HEAD_DIM), F32)] * 2,
        compiler_params=_params("parallel"),
        name="compress_prompt",
    )(pt, *([kp] * n_pages), *([vp] * n_pages), *cw)


def _head_rows(ref, base):
    return jnp.concatenate([ref[0, :, base + h * HEAD_DIM:base + (h + 1) * HEAD_DIM] for h in range(N_HEADS)], axis=0)


def _split_dot(a, b):
    hi = a.astype(BF16)
    lo = (a - hi.astype(F32)).astype(BF16)
    return _dot(hi, b) + _dot(lo, b)


def _compress_sample_kernel(pt_ref, *refs, n_pages, cur_blk):
    kp, vp = refs[:n_pages], refs[n_pages:2 * n_pages]
    (newk, newv, wck, w2k, pek, wcv, w2v, pev, q_ref, tcs_ref, ovl_ref, pow_ref,
     oc_ref, bits_ref, *scr) = refs[2 * n_pages:]
    kc, vc = _compress_kv(kp, vp, newk, newv, (wck, w2k, pek), (wcv, w2v, pev), scr, has_new=True)
    q16 = (_head_rows(q_ref, 0) * SCALE).astype(BF16)
    s = _dot_nt(q16, kc.astype(BF16)) + tcs_ref[...]
    e = jnp.exp(s - jnp.max(s, axis=-1, keepdims=True))
    p = e / jnp.sum(e, axis=-1, keepdims=True)
    oc_ref[0] = _dot(p.astype(BF16), vc.astype(BF16))
    imp_h = _split_dot(p, ovl_ref[...])
    imp = jnp.concatenate([jnp.sum(imp_h[GROUP_REP * g:GROUP_REP * (g + 1)], axis=0, keepdims=True)
                           for g in range(N_KV)], axis=0)
    lane = lax.broadcasted_iota(jnp.int32, (1, LANES), 1)
    forced = (lane == cur_blk) | (lane == 0)
    imp = jnp.where(forced, FORCE_SCORE, jnp.where(lane <= cur_blk, imp, -FORCE_SCORE))
    rank = jnp.zeros((N_KV, LANES), F32)
    for k in range(cur_blk + 1):
        ck = imp[:, k:k + 1]
        rank = rank + jnp.where(ck > imp, 1.0, jnp.where((ck == imp) & (lane > k), 1.0, 0.0))
    sel = jnp.where(rank < N_SEL, 1.0, 0.0) * pow_ref[...]
    lo = jnp.sum(jnp.where(lane < 16, sel, 0.0), axis=-1, keepdims=True).astype(jnp.int32)
    hi = jnp.sum(jnp.where((lane >= 16) & (lane < 32), sel, 0.0), axis=-1, keepdims=True).astype(jnp.int32)
    bits_ref[0] = jnp.broadcast_to(lo | (hi << 16), (N_KV, LANES))


def _compress_sample(cache_k, cache_v, page_table, new_k, new_v, cw, q, tcs, ovl, *, past):
    nb, n_pages = page_table.shape
    width = new_k.shape[-1]
    n_sub = n_pages * (PAGE_SIZE // CMP_STRIDE)
    pow2 = np.zeros((1, LANES), np.float32)
    pow2[0, :32] = 2.0 ** (np.arange(32) % 16)
    row_spec = pl.BlockSpec((1, 1, width), lambda b, pt: (b, 0, 0))
    grid_spec = pltpu.PrefetchScalarGridSpec(
        num_scalar_prefetch=1,
        grid=(nb,),
        in_specs=(_page_specs(n_pages) * 2 + [row_spec, row_spec] + [_const_spec(a.shape) for a in cw]
                  + [pl.BlockSpec((1, 1, q.shape[-1]), lambda b, pt: (b, 0, 0)),
                     _const_spec(tcs.shape), _const_spec(ovl.shape), _const_spec(pow2.shape)]),
        out_specs=[pl.BlockSpec((1, N_HEADS, HEAD_DIM), lambda b, pt: (b, 0, 0)),
                   pl.BlockSpec((1, N_KV, LANES), lambda b, pt: (b, 0, 0))],
        scratch_shapes=_compress_scratch(n_pages) * 2,
    )
    return pl.pallas_call(
        functools.partial(_compress_sample_kernel, n_pages=n_pages, cur_blk=past // SLC_BLOCK),
        grid_spec=grid_spec,
        out_shape=[jax.ShapeDtypeStruct((nb, N_HEADS, HEAD_DIM), F32),
                   jax.ShapeDtypeStruct((nb, N_KV, LANES), jnp.int32)],
        compiler_params=_params("parallel"),
        name="compress_select_sample",
    )(page_table, *([cache_k] * n_pages), *([cache_v] * n_pages), new_k, new_v, *cw, q, tcs, ovl, jnp.asarray(pow2))


def _attn_prompt_kernel(q_ref, z0_ref, z1_ref, z2_ref, gt_ref, kc_ref, vc_ref, ks_ref, vs_ref, kw_ref, vw_ref,
                        t0_ref, t1_ref, tc_ref, ovl_ref, y_ref, m0_scr, m1_scr, acc0_scr, acc1_scr, sel_scr):
    m_scrs, acc_scrs = (m0_scr, m1_scr), (acc0_scr, acc1_scr)
    qt = pl.program_id(2)
    q = q_ref[...]
    qf = jnp.concatenate([q[:, r * HEAD_DIM:(r + 1) * HEAD_DIM] for r in range(GROUP_REP)], axis=0)
    qf = (qf * (SCALE * LOG2E)).astype(BF16)
    lane_q = lax.broadcasted_iota(jnp.int32, (1, TQ), 1)
    qpos = qt * TQ + lane_q
    key_l = lax.broadcasted_iota(jnp.int32, (TQ, 1), 0)
    gate_t = gt_ref[...].T
    z_refs = (z0_ref, z1_ref, z2_ref)

    def emit(br, r, o_t, first):
        cs = slice(r * HEAD_DIM, (r + 1) * HEAD_DIM)
        term = o_t.T * z_refs[br][:, cs]
        if first:
            y_ref[:, cs] = term
        else:
            y_ref[:, cs] += term

    blk_per_tile = TQ // SLC_BLOCK

    def compressed_and_select():
        n_cmp = kc_ref.shape[1]
        s_all = _dot_nt(kc_ref[0].astype(BF16), qf)
        vc_t = vc_ref[0].T.astype(BF16)
        cmp_end = lax.broadcasted_iota(jnp.int32, (n_cmp, 1), 0) * CMP_STRIDE + (CMP_LEN - 1)
        vis = qpos >= cmp_end
        any_vis = qpos >= CMP_LEN - 1
        row0 = pl.multiple_of(TQ // 2 - (TQ // CMP_STRIDE) * qt, CMP_STRIDE)
        psum = jnp.zeros((n_cmp, TQ), F32)
        for r in range(GROUP_REP):
            s = s_all[:, r * TQ:(r + 1) * TQ] + tc_ref[r, pl.ds(row0, n_cmp), :]
            s = jnp.where(vis, s, NEG_INF)
            e = jnp.exp2(s - jnp.max(s, axis=0, keepdims=True))
            p = jnp.where(any_vis, e / jnp.sum(e, axis=0, keepdims=True), 0.0)
            psum = psum + p
            emit(0, r, _dot(vc_t, p.astype(BF16)) * gate_t[r:r + 1, :], True)
        n_slc = ovl_ref.shape[0]
        psum_hi = psum.astype(BF16)
        imp = _dot(ovl_ref[...], psum_hi) + _dot(ovl_ref[...], (psum - psum_hi.astype(F32)).astype(BF16))
        blk = lax.broadcasted_iota(jnp.int32, (n_slc, 1), 0)
        cur = lax.shift_right_logical(qpos, int(math.log2(SLC_BLOCK)))
        forced = (blk == cur) | (blk == 0)
        imp = jnp.where(forced, FORCE_SCORE, jnp.where(blk <= cur, imp, -FORCE_SCORE))
        rank = jnp.zeros((n_slc, TQ), F32)
        for k in range(n_slc):
            rk = imp[k:k + 1, :]
            rank = rank + jnp.where(rk > imp, 1.0, jnp.where((rk == imp) & (blk > k), 1.0, 0.0))
        sel = jnp.where(rank < min(N_SEL, n_slc), 1.0, 0.0)
        for t in range(n_slc // blk_per_tile):
            sel_scr[t, 0:blk_per_tile, :] = sel[t * blk_per_tile:(t + 1) * blk_per_tile, :]

    ones_rows = jnp.ones((ACC_ROWS - HEAD_DIM, TQ), BF16)

    refs = ((ks_ref, vs_ref), (kw_ref, vw_ref))

    def attend(tiles, first):
        k_ts, v_ts, masks = [], [], []
        for st, kt, kind in tiles:
            k_ref, v_ref = refs[st]
            start = pl.multiple_of(kt * TQ, TQ)
            k_ts.append(k_ref[pl.ds(start, TQ), :])
            v_t = v_ref[pl.ds(start, TQ), :].T.astype(BF16)
            v_ts.append(jnp.concatenate([v_t, ones_rows], axis=0))
            mask = None
            if kind == "diag":
                mask = key_l <= lane_q
            elif kind == "winfar":
                mask = key_l > lane_q
            if st == 0:
                sel4 = sel_scr[kt, 0:blk_per_tile, :]
                selm = jnp.concatenate(
                    [jnp.broadcast_to(sel4[j:j + 1, :], (SLC_BLOCK, TQ)) for j in range(blk_per_tile)], axis=0) > 0.5
                mask = selm if mask is None else mask & selm
            masks.append(mask)
        s_all = _dot_nt(k_ts[0] if len(k_ts) == 1 else jnp.concatenate(k_ts, axis=0), qf)
        for r in range(GROUP_REP):
            sl = slice(r * TQ, (r + 1) * TQ)
            for st in sorted(set(t[0] for t in tiles)):
                ss, vs = [], []
                for i, ((st_i, kt, kind), mask) in enumerate(zip(tiles, masks)):
                    if st_i != st:
                        continue
                    s = s_all[i * TQ:(i + 1) * TQ, sl]
                    if kind == "diag":
                        s = s + t0_ref[r]
                    elif kind == "near":
                        s = s + t1_ref[r]
                    if mask is not None:
                        s = jnp.where(mask, s, NEG_INF)
                    ss.append(s)
                    vs.append(v_ts[i])
                mx = jnp.max(ss[0], axis=0, keepdims=True)
                for s in ss[1:]:
                    mx = jnp.maximum(mx, jnp.max(s, axis=0, keepdims=True))
                if first:
                    m_new = mx
                    upd = None
                else:
                    m_old = m_scrs[st][:, sl]
                    m_new = jnp.maximum(m_old, mx)
                    upd = jnp.exp2(m_old - m_new) * acc_scrs[st][:, sl]
                for s, v_t in zip(ss, vs):
                    pv = _dot(v_t, jnp.exp2(s - m_new).astype(BF16))
                    upd = pv if upd is None else upd + pv
                acc_scrs[st][:, sl] = upd
                m_scrs[st][:, sl] = m_new

    compressed_and_select()
    attend([(0, qt, "diag"), (1, qt, "diag")], True)

    @pl.when(qt >= 1)
    def _():
        attend([(0, qt - 1, "near"), (1, qt - 1, "near")], False)

    @pl.when(qt >= 2)
    def _():
        n_far = qt - 1

        def far_pair(i, carry):
            attend([(0, 2 * i, "far"), (0, 2 * i + 1, "far")], False)
            return carry

        lax.fori_loop(0, lax.shift_right_logical(n_far, 1), far_pair, 0)

        @pl.when((n_far & 1) == 1)
        def _():
            attend([(1, qt - 2, "winfar"), (0, n_far - 1, "far")], False)

        @pl.when((n_far & 1) == 0)
        def _():
            attend([(1, qt - 2, "winfar")], False)

    for st in range(2):
        br = st + 1
        for r in range(GROUP_REP):
            sl = slice(r * TQ, (r + 1) * TQ)
            w = gate_t[br * GROUP_REP + r:br * GROUP_REP + r + 1, :] / acc_scrs[st][HEAD_DIM:HEAD_DIM + 1, sl]
            emit(br, r, acc_scrs[st][0:HEAD_DIM, sl] * w, False)


def _attn_prompt(proj, kc, vc, k_slc, v_slc, k_win, v_win, t0, t1, tc, ovl_t, *, bsz, t):
    nq = t // TQ
    gw = GROUP_REP * HEAD_DIM
    assert TQ // 2 - (TQ // CMP_STRIDE) * (nq - 1) >= 0 and kc.shape[1] == N_KV * (t // CMP_STRIDE)
    n_cmp = t // CMP_STRIDE
    d_att = N_HEADS * HEAD_DIM
    zoff = d_att // gw
    row = lambda b, g, i: b * nq + i
    in_specs = [
        pl.BlockSpec((TQ, gw), lambda b, g, i: (row(b, g, i), g)),
    ] + [
        pl.BlockSpec((TQ, gw), lambda b, g, i, br=br: (row(b, g, i), zoff * (1 + br) + g)) for br in range(N_BRANCH)
    ] + [
        pl.BlockSpec((TQ, LANES), lambda b, g, i: (row(b, g, i), (1 + N_BRANCH) * d_att // LANES + g)),
        pl.BlockSpec((1, n_cmp, HEAD_DIM), lambda b, g, i: (b, g, 0)),
        pl.BlockSpec((1, n_cmp, HEAD_DIM), lambda b, g, i: (b, g, 0)),
    ] + [pl.BlockSpec((None, None, t, HEAD_DIM), lambda b, g, i: (b, g, 0, 0))] * 4 + [
        pl.BlockSpec((GROUP_REP, TQ, TQ), lambda b, g, i: (g, 0, 0))] * 3 + [
        pl.BlockSpec(ovl_t.shape, lambda b, g, i: (0, 0)),
    ]
    return pl.pallas_call(
        _attn_prompt_kernel,
        grid=(bsz, N_KV, nq),
        in_specs=in_specs,
        out_specs=pl.BlockSpec((TQ, gw), lambda b, g, i: (row(b, g, i), g)),
        out_shape=jax.ShapeDtypeStruct((bsz * t, d_att), F32),
        scratch_shapes=[
            pltpu.VMEM((1, GROUP_REP * TQ), F32),
            pltpu.VMEM((1, GROUP_REP * TQ), F32),
            pltpu.VMEM((ACC_ROWS, GROUP_REP * TQ), F32),
            pltpu.VMEM((ACC_ROWS, GROUP_REP * TQ), F32),
            pltpu.VMEM((t // TQ, SUBLANES, TQ), F32),
        ],
        compiler_params=_params("parallel", "parallel", "arbitrary"),
        name="nsa_attention_prompt",
    )(proj, proj, proj, proj, proj, kc, vc, k_slc, v_slc, k_win, v_win, t0, t1, tc, ovl_t)


def _attn_sample_kernel(bits_ref, pt_ref, *refs, n_pages):
    kp, vp = refs[:n_pages], refs[n_pages:2 * n_pages]
    (kwin_ref, vwin_ref, nks_ref, nvs_ref, nkw_ref, nvw_ref, q_ref, oc_ref, ts_ref, tw_ref,
     y_ref, kwo_ref, vwo_ref) = refs[2 * n_pages:]
    b = pl.program_id(0)
    rows_pp = PAGE_SIZE * N_KV
    blk_rows = SLC_BLOCK * N_KV
    d_att = N_HEADS * HEAD_DIM
    n_win_rows = kwin_ref.shape[1]
    head_rows = functools.partial(_head_rows, q_ref)

    for src, new, dst in ((kwin_ref, nkw_ref, kwo_ref), (vwin_ref, nvw_ref, vwo_ref)):
        dst[0, 0:n_win_rows - N_KV, :] = src[0, N_KV:n_win_rows, :]
        dst[0, n_win_rows - N_KV:n_win_rows, :] = jnp.concatenate(
            [new[0, :, g * HEAD_DIM:(g + 1) * HEAD_DIM] for g in range(N_KV)], axis=0)

    def group_rows(ref):
        return jnp.concatenate([ref[0, :, (h // GROUP_REP) * HEAD_DIM:(h // GROUP_REP + 1) * HEAD_DIM]
                                for h in range(N_HEADS)], axis=0)

    def softmax_pv(s_tiles, v_tiles, s_new, v_new):
        m = s_new
        for s in s_tiles:
            m = jnp.maximum(m, jnp.max(s, axis=-1, keepdims=True))
        p_new = jnp.exp(s_new - m)
        l = p_new
        acc = p_new * v_new
        for s, v in zip(s_tiles, v_tiles):
            p = jnp.exp(s - m)
            l = l + jnp.sum(p, axis=-1, keepdims=True)
            acc = acc + _dot(p.astype(BF16), v.astype(BF16))
        return acc / l

    q32 = head_rows(0) * SCALE
    q16 = q32.astype(BF16)
    head = lax.broadcasted_iota(jnp.int32, (N_HEADS, 1), 0)
    bits = jnp.zeros((N_HEADS, 1), jnp.int32)
    for g in range(N_KV):
        bits = jnp.where((head >= g * GROUP_REP) & (head < (g + 1) * GROUP_REP), bits_ref[b, g], bits)
    lane_blk = lax.broadcasted_iota(jnp.int32, (1, rows_pp), 1) // blk_rows
    s_tiles, v_tiles = [], []
    for p in range(n_pages):
        s = _dot_nt(q16, kp[p][0].astype(BF16)) + ts_ref[:, p * rows_pp:(p + 1) * rows_pp]
        shift = jnp.broadcast_to(lane_blk + p * (rows_pp // blk_rows), s.shape)
        sel = lax.shift_right_logical(jnp.broadcast_to(bits, s.shape), shift) & 1
        s_tiles.append(jnp.where(sel == 1, s, NEG_INF))
        v_tiles.append(vp[p][0])
    tail = n_pages * rows_pp
    s_new = jnp.sum(q32 * group_rows(nks_ref), axis=-1, keepdims=True) + ts_ref[:, tail:tail + 1]
    o_slc = softmax_pv(s_tiles, v_tiles, s_new, group_rows(nvs_ref))
    s_tiles, v_tiles = [], []
    for p in range(n_win_rows // rows_pp):
        rows = slice(p * rows_pp, (p + 1) * rows_pp)
        s_tiles.append(_dot_nt(q16, kwin_ref[0, rows, :].astype(BF16)) + tw_ref[:, rows])
        v_tiles.append(vwin_ref[0, rows, :])
    s_new = jnp.sum(q32 * group_rows(nkw_ref), axis=-1, keepdims=True) + tw_ref[:, n_win_rows:n_win_rows + 1]
    o_win = softmax_pv(s_tiles, v_tiles, s_new, group_rows(nvw_ref))
    y = jnp.zeros((N_HEADS, HEAD_DIM), F32)
    gate0 = (1 + N_BRANCH) * d_att
    for br, o in enumerate((oc_ref[0], o_slc, o_win)):
        cols = [gate0 + (h // GROUP_REP) * LANES + br * GROUP_REP + h % GROUP_REP for h in range(N_HEADS)]
        gate = jnp.concatenate([q_ref[0, :, c:c + 1] for c in cols], axis=0)
        y = y + gate * o * head_rows((1 + br) * d_att)
    y_ref[0] = y


def _attn_sample(cache_k, cache_v, page_table, bits, k_win, v_win, new_rows, proj, o_cmp, ts, tw):
    nb, n_pages = page_table.shape
    rows_pp = PAGE_SIZE * N_KV
    win_spec = pl.BlockSpec((1, k_win.shape[1], HEAD_DIM), lambda b, bits, pt: (b, 0, 0))
    row_spec = lambda w: pl.BlockSpec((1, 1, w), lambda b, bits, pt: (b, 0, 0))
    const = lambda a: pl.BlockSpec(a.shape, lambda b, bits, pt: (0,) * a.ndim)
    page_specs = [pl.BlockSpec((1, rows_pp, HEAD_DIM), lambda b, bits, pt, p=p: (pt[b, p], 0, 0)) for p in range(n_pages)]
    grid_spec = pltpu.PrefetchScalarGridSpec(
        num_scalar_prefetch=2,
        grid=(nb,),
        in_specs=(page_specs * 2 + [win_spec] * 2
                  + [row_spec(N_KV * HEAD_DIM)] * 4
                  + [row_spec(proj.shape[-1]),
                     pl.BlockSpec((1, N_HEADS, HEAD_DIM), lambda b, bits, pt: (b, 0, 0)),
                     const(ts), const(tw)]),
        out_specs=[pl.BlockSpec((1, N_HEADS, HEAD_DIM), lambda b, bits, pt: (b, 0, 0)), win_spec, win_spec],
    )
    return pl.pallas_call(
        functools.partial(_attn_sample_kernel, n_pages=n_pages),
        grid_spec=grid_spec,
        out_shape=[jax.ShapeDtypeStruct((nb, N_HEADS, HEAD_DIM), F32),
                   jax.ShapeDtypeStruct(k_win.shape, F32), jax.ShapeDtypeStruct(v_win.shape, F32)],
        compiler_params=_params("parallel"),
        name="nsa_attention_sample",
    )(bits, page_table, *([cache_k] * n_pages), *([cache_v] * n_pages), k_win, v_win, *new_rows,
      proj, o_cmp, ts, tw)


def _overlap(n_cmp, n_slc):
    cs = np.arange(n_cmp) * CMP_STRIDE
    bs = np.arange(n_slc) * SLC_BLOCK
    ov = np.clip(np.minimum(cs[:, None] + CMP_LEN, bs[None, :] + SLC_BLOCK) - np.maximum(cs[:, None], bs[None, :]), 0, None)
    return (ov / CMP_LEN).astype(np.float32)


def _compress_weights(w1, w2, pe):
    half = CMP_STRIDE * HEAD_DIM
    wcat = jnp.concatenate([w1[:half], w1[half:]], axis=1).astype(BF16)
    pe8 = jnp.pad(pe.reshape(2, half), ((0, SUBLANES - 2), (0, 0)))
    return wcat, w2.astype(BF16), pe8


def kernel(x_prompt, x_sample, state_pool, cache_k_cmp, cache_v_cmp, cache_k_slc, cache_v_slc, state_k_win, state_v_win, page_table, c_prompt, c_sample, g_norm, w_ada, b_ada, w_in_a, w_grp, pool_scale, w_out_a, g_kv, w_kv, pe_k, w_ck1, w_ck2, pe_v, w_cv1, w_cv2, rel_bias, w_in_b, w_out_b, g_final):
    bsz, t, d = x_prompt.shape
    nb = x_sample.shape[0]
    n_pages = page_table.shape[1]
    past = n_pages * PAGE_SIZE
    d_att = N_HEADS * HEAD_DIM
    d_kv = N_KV * HEAD_DIM
    n_a = w_in_a.shape[0]
    assert n_a == 1 and w_in_b.shape[0] == 1 and x_sample.shape[1] == 1
    assert t % TQ == 0 and t >= WINDOW and past % PAGE_SIZE == 0 and state_k_win.shape[1] == WINDOW

    w_in_a16 = w_in_a[0].astype(BF16)
    w_grp16 = w_grp[0].astype(BF16)
    w_out_a16 = w_out_a[0].astype(BF16)
    w_kv16 = w_kv.astype(BF16)
    w_out_b16 = w_out_b[0].astype(BF16)
    n_qz = (1 + N_BRANCH) * d_att
    wg = w_in_b[0][:, n_qz:].reshape(d, N_BRANCH, N_KV, GROUP_REP).transpose(0, 2, 1, 3)
    wg = jnp.pad(wg.reshape(d, N_KV, N_BRANCH * GROUP_REP), ((0, 0), (0, 0), (0, LANES - N_BRANCH * GROUP_REP)))
    n_proj = n_qz + TN_PROJ
    wg = jnp.pad(wg.reshape(d, N_KV * LANES), ((0, 0), (0, TN_PROJ - N_KV * LANES)))
    w_in_b16 = jnp.concatenate([w_in_b[0][:, :n_qz], wg], axis=1).astype(BF16)
    nsa_proj = functools.partial(_nm_matmul, w=w_in_b16, n_cols=n_proj, tn=TN_PROJ, silu_cols=d_att, sigmoid_cols=n_qz)
    cw_k = _compress_weights(w_ck1, w_ck2, pe_k)
    cw_v = _compress_weights(w_cv1, w_cv2, pe_v)
    cw = cw_k + cw_v

    mod = _ada(jnp.concatenate([c_prompt, c_sample], axis=0), w_ada, b_ada)

    def modulation(l, lo, hi, per_row):
        parts = [mod[l, lo:hi, k * d:(k + 1) * d] for k in range(3)]
        return [p[None] if per_row else p[:, None] for p in parts]

    t0, t1, tc, ts, tw, tcs = _bias_tables(rel_bias, past=past, n_cmp_s=past // CMP_STRIDE)
    ts, tw, tcs = ts.reshape(N_HEADS, -1), tw.reshape(N_HEADS, -1), tcs.reshape(N_HEADS, -1)

    shift, scale, gate = modulation(0, 0, bsz, False)
    uz = _nm_matmul(x_prompt, g_norm[0], scale, shift, w_in_a16, n_cols=2 * d, tm=TM_PROJ, tn=TN_PROJ, name="in_proj_pool_prompt")
    pool_p = uz[:, t - POOL_BUF:, :d][None]
    x1 = _pool_prompt(uz, w_grp16, pool_scale[0], w_out_a16, x_prompt, gate, tm=256)
    kv_p = _kv_proj(x1, g_kv, w_kv16, tm=256, name="kv_proj_prompt")
    heads = lambda a: a.reshape(a.shape[0], -1, N_KV, HEAD_DIM)
    kv_state_p = [heads(a) for a in kv_p[:4]] + [heads(a[:, (t - WINDOW) * N_KV:]) for a in kv_p[4:6]]
    kc_p, vc_p = _compress_prompt(kv_p[0], kv_p[1], cw)
    shift, scale, gate = modulation(1, 0, bsz, False)
    proj = nsa_proj(x1, g_norm[1], scale, shift, tm=TM_PROJ, name="in_proj_nsa_prompt")
    ovl_t = jnp.asarray(np.pad(_overlap(t // CMP_STRIDE - 1, t // SLC_BLOCK), ((0, 1), (0, 0))).T).astype(BF16)
    flat = lambda a: a.reshape(bsz * t, a.shape[-1])
    y1 = _attn_prompt(flat(proj), kc_p, vc_p, kv_p[6], kv_p[7], kv_p[8], kv_p[9],
                      t0, t1, tc, ovl_t, bsz=bsz, t=t)
    y_prompt = _proj_residual(y1.reshape(bsz, t, d_att), w_out_b16, x1, gate, g_final, tm=512, final_norm=True,
                              name="out_proj_nsa_prompt")

    xs = x_sample.reshape(1, nb, d)
    shift, scale, gate = modulation(0, bsz, bsz + nb, True)
    uz_s = _nm_matmul(xs, g_norm[0], scale, shift, w_in_a16, n_cols=2 * d, tm=nb, tn=512, name="in_proj_pool_sample")
    pool_s = jnp.concatenate([state_pool[:, :, 1:], uz_s[0, :, None, :d][None]], axis=2)
    y0_s = _pool_sample(uz_s, state_pool[0], w_grp16, pool_scale[0], qpos=past, tb=min(32, nb))
    x1_s = _proj_residual(y0_s, w_out_a16, xs, gate, g_final, tm=nb, final_norm=False, name="out_proj_pool_sample")
    kv_s = _kv_proj(x1_s, g_kv, w_kv16, tm=nb, name="kv_proj_sample")
    new_rows = [a.reshape(nb, 1, d_kv) for a in kv_s[:2 * N_BRANCH]]
    shift, scale, gate = modulation(1, bsz, bsz + nb, True)
    proj_s = nsa_proj(x1_s, g_norm[1], scale, shift, tm=nb, name="in_proj_nsa_sample").reshape(nb, 1, n_proj)
    n_cmp_s = past // CMP_STRIDE
    n_slc_s = past // SLC_BLOCK + 1
    ovl_s = jnp.asarray(np.tile(np.pad(_overlap(n_cmp_s, n_slc_s), ((0, 0), (0, LANES - n_slc_s))), (N_KV, 1))).astype(BF16)
    rows_of = lambda c: c.reshape(c.shape[0], PAGE_SIZE * N_KV, HEAD_DIM)
    o_cmp, bits = _compress_sample(rows_of(cache_k_cmp), rows_of(cache_v_cmp), page_table, new_rows[0], new_rows[1], cw,
                                   proj_s, tcs, ovl_s, past=past)
    y1_s, k_win_s, v_win_s = _attn_sample(
        rows_of(cache_k_slc), rows_of(cache_v_slc), page_table, bits[:, :, 0],
        state_k_win.reshape(nb, WINDOW * N_KV, HEAD_DIM), state_v_win.reshape(nb, WINDOW * N_KV, HEAD_DIM),
        new_rows[2:], proj_s, o_cmp, ts, tw)
    y_sample = _proj_residual(y1_s.reshape(1, nb, d_att), w_out_b16, x1_s, gate, g_final, tm=nb, final_norm=True,
                              name="out_proj_nsa_sample").reshape(nb, 1, d)
    new4 = [a.reshape(nb, 1, N_KV, HEAD_DIM) for a in kv_s[:2 * N_BRANCH]]
    kv_state_s = new4[:4] + [k_win_s.reshape(state_k_win.shape), v_win_s.reshape(state_v_win.shape)]

    return (y_prompt, y_sample, pool_p, *kv_state_p, pool_s, *kv_state_s)
```

```python
import functools
import math

import numpy as np
import jax
import jax.numpy as jnp
from jax import lax
from jax.experimental import pallas as pl
from jax.experimental.pallas import tpu as pltpu

F32 = jnp.float32
BF16 = jnp.bfloat16

PAGE_SIZE = 128
POOL_WINDOWS = (2, 4, 8, 16)
POOL_BUF = max(POOL_WINDOWS) - 1
HEAD_DIM = 128
N_KV = 4
GROUP_REP = 4
N_HEADS = N_KV * GROUP_REP
N_BRANCH = 3
CMP_LEN = 32
CMP_STRIDE = 16
SLC_BLOCK = 64
N_SEL = 8
WINDOW = 512
N_BUCKETS = 32
MAX_DISTANCE = 128
RMS_EPS = 1e-6
SCALE = HEAD_DIM ** -0.5
LOG2E = math.log2(math.e)
NEG_INF = -1e30
FORCE_SCORE = 1e9

LANES = 128
SUBLANES = 8
VMEM_LIMIT_BYTES = 56 * 1024 * 1024
TQ = 256
COMPRESS_CHUNK_PAGES = 4
ACC_ROWS = HEAD_DIM + 16
TN_PROJ = 1024
TM_PROJ = 1024
HALO = 16


def _bucket_starts():
    max_exact = N_BUCKETS // 2
    d = np.arange(0, MAX_DISTANCE + 1)
    large = max_exact + np.floor(
        np.log(np.maximum(d, max_exact) / max_exact) / math.log(MAX_DISTANCE / max_exact) * (N_BUCKETS - max_exact)
    ).astype(np.int64)
    bucket = np.where(d < max_exact, d, np.minimum(large, N_BUCKETS - 1))
    return [int(np.argmax(bucket >= k)) for k in range(N_BUCKETS)]


BUCKET_STARTS = _bucket_starts()
FAR_DIST = BUCKET_STARTS[-1]


def _params(*sem):
    return pltpu.CompilerParams(dimension_semantics=sem, vmem_limit_bytes=VMEM_LIMIT_BYTES)


def _silu(x):
    return x * jax.nn.sigmoid(x)


def _rms(x, g):
    return x * lax.rsqrt(jnp.mean(x * x, axis=-1, keepdims=True) + RMS_EPS) * g


def _dot(a, b):
    return jnp.dot(a, b, preferred_element_type=F32)


def _dot_nt(a, b):
    return lax.dot_general(a, b, (((1,), (1,)), ((), ())), preferred_element_type=F32)


def _ada_kernel(c_ref, w_ref, b_ref, o_ref):
    a = _silu(c_ref[...]).astype(BF16)
    o_ref[0] = _dot(a, w_ref[0].astype(BF16)) + b_ref[0]


def _ada(c_all, w_ada, b_ada):
    depth, d, n = w_ada.shape
    m = c_all.shape[0]
    tn = 512
    return pl.pallas_call(
        _ada_kernel,
        grid=(depth, n // tn),
        in_specs=[
            pl.BlockSpec((m, d), lambda l, j: (0, 0)),
            pl.BlockSpec((1, d, tn), lambda l, j: (l, 0, j)),
            pl.BlockSpec((1, 1, tn), lambda l, j: (l, 0, j)),
        ],
        out_specs=pl.BlockSpec((1, m, tn), lambda l, j: (l, 0, j)),
        out_shape=jax.ShapeDtypeStruct((depth, m, n), F32),
        compiler_params=_params("parallel", "parallel"),
        name="ada_modulation",
    )(c_all, w_ada, b_ada.reshape(depth, 1, n))


def _nm_matmul_kernel(x_ref, g_ref, sc_ref, sh_ref, w_ref, *rest, silu_from, sigmoid_from):
    *tail, o_ref, h_scr = rest
    j = pl.program_id(2)

    @pl.when(j == 0)
    def _():
        y = _rms(x_ref[0], g_ref[...])
        h_scr[...] = (y * (1.0 + sc_ref[0]) + sh_ref[0]).astype(BF16)

    def column_tile(weight_ref):
        r = _dot(h_scr[...], weight_ref[...])
        if silu_from is None:
            o_ref[0] = r
        else:
            sig = jax.nn.sigmoid(r)
            o_ref[0] = jnp.where(j >= sigmoid_from, sig, jnp.where(j >= silu_from, r * sig, r))

    if tail:
        last = pl.num_programs(2) - 1
        pl.when(j < last)(lambda: column_tile(w_ref))
        pl.when(j == last)(lambda: column_tile(tail[0]))
    else:
        column_tile(w_ref)


def _mod_spec(arr, tm):
    d = arr.shape[-1]
    if arr.shape[1] == 1:
        return pl.BlockSpec((1, 1, d), lambda b, i, *_: (b, 0, 0))
    return pl.BlockSpec((1, tm, d), lambda b, i, *_: (b, i, 0))


def _nm_matmul(x, g, scale, shift, w, *, n_cols, tm, tn, silu_cols=None, sigmoid_cols=None, w_tail=None, name):
    bsz, rows, d = x.shape
    n_j = n_cols // tn
    if w_tail is None:
        w_specs, w_args = [pl.BlockSpec((d, tn), lambda b, i, j: (0, j))], [w]
    else:
        w_specs = [pl.BlockSpec((d, tn), lambda b, i, j: (0, jnp.minimum(j, n_j - 2))),
                   pl.BlockSpec((d, tn), lambda b, i, j: (0, 0))]
        w_args = [w, w_tail]
    if silu_cols is None and sigmoid_cols is None:
        silu_from = sigmoid_from = None
    else:
        sigmoid_from = n_cols // tn if sigmoid_cols is None else sigmoid_cols // tn
        silu_from = sigmoid_from if silu_cols is None else silu_cols // tn
    return pl.pallas_call(
        functools.partial(_nm_matmul_kernel, silu_from=silu_from, sigmoid_from=sigmoid_from),
        grid=(bsz, rows // tm, n_cols // tn),
        in_specs=[
            pl.BlockSpec((1, tm, d), lambda b, i, j: (b, i, 0)),
            pl.BlockSpec((1, d), lambda b, i, j: (0, 0)),
            _mod_spec(scale, tm),
            _mod_spec(shift, tm),
        ] + w_specs,
        out_specs=pl.BlockSpec((1, tm, tn), lambda b, i, j: (b, i, j)),
        out_shape=jax.ShapeDtypeStruct((bsz, rows, n_cols), F32),
        scratch_shapes=[pltpu.VMEM((tm, d), BF16)],
        compiler_params=_params("parallel", "parallel", "arbitrary"),
        name=name,
    )(x, g.reshape(1, d), scale, shift, *w_args)


def _pool_mix(pooled_fn, z, wg_ref, ps_ref, store):
    grp = z.shape[-1] // len(POOL_WINDOWS)
    for gi, w in enumerate(POOL_WINDOWS):
        cs = slice(gi * grp, (gi + 1) * grp)
        mixed = _dot(pooled_fn(gi, w, cs).astype(BF16), wg_ref[gi])
        store(cs, mixed * ps_ref[:, cs] * _silu(z[:, cs]))


def _pool_prompt_kernel(u_ref, z_ref, halo_ref, wg_ref, ps_ref, wo_ref, x_ref, gate_ref, o_ref,
                        ue_scr, a_scr, b_scr, y_scr):
    i = pl.program_id(1)
    tm = u_ref.shape[1]
    top = SUBLANES + HALO
    n = top + tm
    u = u_ref[0]
    ue_scr[0:SUBLANES, :] = jnp.zeros((SUBLANES, u.shape[1]), F32)
    ue_scr[SUBLANES:top, :] = jnp.where(i > 0, halo_ref[0], 0.0)
    ue_scr[top:n, :] = u
    zero8 = jnp.zeros((SUBLANES, a_scr.shape[1]), F32)
    a_scr[0:SUBLANES, :] = zero8
    b_scr[0:SUBLANES, :] = zero8
    pos = i * tm + lax.broadcasted_iota(jnp.int32, (tm, 1), 0)

    def pooled(gi, w, cs):
        src = lambda lo, hi: ue_scr[lo:hi, cs]
        k = 1
        for dst in (a_scr, b_scr, a_scr):
            if 2 * k >= w:
                break
            dst[SUBLANES:n, :] = src(SUBLANES, n) + src(SUBLANES - k, n - k)
            src = lambda lo, hi, dst=dst: dst[lo:hi, :]
            k *= 2
        acc = src(top, n) + src(top - k, n - k)
        cnt = jnp.minimum(pos + 1, w).astype(F32)
        return acc / cnt - u[:, cs]

    def store(cs, val):
        y_scr[:, cs] = val.astype(BF16)

    _pool_mix(pooled, z_ref[0], wg_ref, ps_ref, store)
    o_ref[0] = x_ref[0] + gate_ref[0] * _dot(y_scr[...], wo_ref[...])


def _pool_prompt(uz, w_grp, pool_scale, w_out, x, gate, *, tm):
    bsz, t, d2 = uz.shape
    d = d2 // 2
    ng, grp, _ = w_grp.shape
    return pl.pallas_call(
        _pool_prompt_kernel,
        grid=(bsz, t // tm),
        in_specs=[
            pl.BlockSpec((1, tm, d), lambda b, i: (b, i, 0)),
            pl.BlockSpec((1, tm, d), lambda b, i: (b, i, 1)),
            pl.BlockSpec((1, HALO, d), lambda b, i: (b, jnp.maximum(i * (tm // HALO) - 1, 0), 0)),
            pl.BlockSpec((ng, grp, grp), lambda b, i: (0, 0, 0)),
            pl.BlockSpec((1, d), lambda b, i: (0, 0)),
            pl.BlockSpec(w_out.shape, lambda b, i: (0, 0)),
            pl.BlockSpec((1, tm, d), lambda b, i: (b, i, 0)),
            _mod_spec(gate, tm),
        ],
        out_specs=pl.BlockSpec((1, tm, d), lambda b, i: (b, i, 0)),
        out_shape=jax.ShapeDtypeStruct((bsz, t, d), F32),
        scratch_shapes=[pltpu.VMEM((SUBLANES + HALO + tm, d), F32),
                        pltpu.VMEM((SUBLANES + HALO + tm, grp), F32),
                        pltpu.VMEM((SUBLANES + HALO + tm, grp), F32),
                        pltpu.VMEM((tm, d), BF16)],
        compiler_params=_params("parallel", "parallel"),
        name="pool_mixer_prompt",
    )(uz, uz, uz, w_grp, pool_scale.reshape(1, d), w_out, x, gate)


def _pool_sample_kernel(u_ref, z_ref, pre_ref, wg_ref, ps_ref, y_ref, *, qpos):
    u = u_ref[0]

    def pooled(gi, w, cs):
        acc = u[:, cs]
        for k in range(1, w):
            acc = acc + pre_ref[:, POOL_BUF - k, cs]
        return acc / float(min(qpos + 1, w)) - u[:, cs]

    def store(cs, val):
        y_ref[0, :, cs] = val

    _pool_mix(pooled, z_ref[0], wg_ref, ps_ref, store)


def _pool_sample(uz, prefix, w_grp, pool_scale, *, qpos, tb):
    _, nb, d2 = uz.shape
    d = d2 // 2
    ng, grp, _ = w_grp.shape
    return pl.pallas_call(
        functools.partial(_pool_sample_kernel, qpos=qpos),
        grid=(nb // tb,),
        in_specs=[
            pl.BlockSpec((1, tb, d), lambda i: (0, i, 0)),
            pl.BlockSpec((1, tb, d), lambda i: (0, i, 1)),
            pl.BlockSpec((tb, POOL_BUF, d), lambda i: (i, 0, 0)),
            pl.BlockSpec((ng, grp, grp), lambda i: (0, 0, 0)),
            pl.BlockSpec((1, d), lambda i: (0, 0)),
        ],
        out_specs=pl.BlockSpec((1, tb, d), lambda i: (0, i, 0)),
        out_shape=jax.ShapeDtypeStruct((1, nb, d), F32),
        compiler_params=_params("parallel"),
        name="pool_mixer_sample",
    )(uz, uz, prefix, w_grp, pool_scale.reshape(1, d))


def _proj_residual_kernel(y_ref, w_ref, x_ref, gate_ref, g_ref, o_ref, *, final_norm):
    xo = x_ref[0] + gate_ref[0] * _dot(y_ref[0].astype(BF16), w_ref[...])
    o_ref[0] = _rms(xo, g_ref[...]) if final_norm else xo


def _proj_residual(y, w, x, gate, g_final, *, tm, final_norm, name):
    bsz, rows, d = x.shape
    dy = y.shape[-1]
    return pl.pallas_call(
        functools.partial(_proj_residual_kernel, final_norm=final_norm),
        grid=(bsz, rows // tm),
        in_specs=[
            pl.BlockSpec((1, tm, dy), lambda b, i: (b, i, 0)),
            pl.BlockSpec((dy, d), lambda b, i: (0, 0)),
            pl.BlockSpec((1, tm, d), lambda b, i: (b, i, 0)),
            _mod_spec(gate, tm),
            pl.BlockSpec((1, d), lambda b, i: (0, 0)),
        ],
        out_specs=pl.BlockSpec((1, tm, d), lambda b, i: (b, i, 0)),
        out_shape=jax.ShapeDtypeStruct((bsz, rows, d), F32),
        compiler_params=_params("parallel", "parallel"),
        name=name,
    )(y, w, x, gate, g_final.reshape(1, d))


KV_GROUP_COPIES = ((2, BF16), (3, F32), (4, BF16), (5, F32))


def _kv_kernel(x_ref, g_ref, w_ref, *o_refs):
    h = _rms(x_ref[0], g_ref[...]).astype(BF16)
    n_out = 2 * N_BRANCH
    tm = x_ref.shape[1]
    n = N_KV * HEAD_DIM
    copies = dict((o, (o_refs[n_out + i], dt)) for i, (o, dt) in enumerate(KV_GROUP_COPIES))
    for o in range(n_out):
        r = _dot(h, w_ref[:, o * n:(o + 1) * n])
        for g in range(N_KV):
            rg = r[:, g * HEAD_DIM:(g + 1) * HEAD_DIM]
            o_refs[o][0, pl.ds(g, tm, stride=N_KV), :] = rg
            if o in copies:
                c_ref, dt = copies[o]
                c_ref[0, g] = rg.astype(dt)


def _kv_proj(x, g_kv, w_kv, *, tm, name):
    bsz, rows, d = x.shape
    n_out = 2 * N_BRANCH
    return pl.pallas_call(
        _kv_kernel,
        grid=(bsz, rows // tm),
        in_specs=[
            pl.BlockSpec((1, tm, d), lambda b, i: (b, i, 0)),
            pl.BlockSpec((1, d), lambda b, i: (0, 0)),
            pl.BlockSpec(w_kv.shape, lambda b, i: (0, 0)),
        ],
        out_specs=([pl.BlockSpec((1, tm * N_KV, HEAD_DIM), lambda b, i: (b, i, 0))] * n_out
                   + [pl.BlockSpec((1, N_KV, tm, HEAD_DIM), lambda b, i: (b, 0, i, 0))] * len(KV_GROUP_COPIES)),
        out_shape=([jax.ShapeDtypeStruct((bsz, rows * N_KV, HEAD_DIM), F32)] * n_out
                   + [jax.ShapeDtypeStruct((bsz, N_KV, rows, HEAD_DIM), dt) for _, dt in KV_GROUP_COPIES]),
        compiler_params=_params("parallel", "parallel"),
        name=name,
    )(x, g_kv.reshape(1, d), w_kv)


def _bias_lookup(rb_ref, h, dist):
    acc = jnp.full(dist.shape, rb_ref[N_BUCKETS - 1, h], F32)
    for k in range(N_BUCKETS - 2, -1, -1):
        acc = jnp.where(dist < BUCKET_STARTS[k + 1], rb_ref[k, h], acc)
    return acc


def _bias_tables_kernel(rb_ref, t0_ref, t1_ref, tc_ref, ts_ref, tw_ref, tcs_ref, *, past):
    h = pl.program_id(0)
    key = lax.broadcasted_iota(jnp.int32, (TQ, TQ), 0)
    qry = lax.broadcasted_iota(jnp.int32, (TQ, TQ), 1)
    far = rb_ref[N_BUCKETS - 1, h]
    rel2 = lambda dist: (_bias_lookup(rb_ref, h, dist) - far) * LOG2E
    t0_ref[0] = rel2(qry - key)
    t1_ref[0] = rel2(TQ + qry - key)
    tc_ref[0] = rel2(qry - CMP_STRIDE * (key - TQ // 2) - (CMP_LEN - 1))
    ls = lax.broadcasted_iota(jnp.int32, ts_ref.shape[1:], 1)
    own = (ls % N_KV) == h // GROUP_REP
    tok = ls // N_KV
    ts_ref[0] = jnp.where(ls == N_KV * past, rb_ref[0, h],
                          jnp.where(own & (tok < past), _bias_lookup(rb_ref, h, past - tok), NEG_INF))
    lw = lax.broadcasted_iota(jnp.int32, tw_ref.shape[1:], 1)
    own = (lw % N_KV) == h // GROUP_REP
    dw = WINDOW - lw // N_KV
    tw_ref[0] = jnp.where(lw == N_KV * WINDOW, rb_ref[0, h],
                          jnp.where(own & (dw > 0) & (dw < WINDOW) & (past - dw >= 0), _bias_lookup(rb_ref, h, dw), NEG_INF))
    lc = lax.broadcasted_iota(jnp.int32, tcs_ref.shape[1:], 1)
    n_cmp = tcs_ref.shape[2] // N_KV
    dc = past - CMP_STRIDE * (lc % n_cmp) - (CMP_LEN - 1)
    tcs_ref[0] = jnp.where((lc // n_cmp == h // GROUP_REP) & (dc >= 0), _bias_lookup(rb_ref, h, dc), NEG_INF)


def _bias_tables(rel_bias, *, past, n_cmp_s):
    nh = rel_bias.shape[1]
    ls = N_KV * past + LANES
    lw = N_KV * WINDOW + LANES
    shapes = [(nh, TQ, TQ)] * 3 + [(nh, 1, ls), (nh, 1, lw), (nh, 1, N_KV * n_cmp_s)]
    return pl.pallas_call(
        functools.partial(_bias_tables_kernel, past=past),
        grid=(nh,),
        in_specs=[pl.BlockSpec(memory_space=pltpu.SMEM)],
        out_specs=[pl.BlockSpec((1,) + s[1:], lambda h: (h, 0, 0)) for s in shapes],
        out_shape=[jax.ShapeDtypeStruct(s, F32) for s in shapes],
        compiler_params=_params("parallel"),
        name="rel_bias_tables",
    )(rel_bias)


def _compress_fill(page_refs, lhs_scr, first_page):
    sub_per_page = PAGE_SIZE // CMP_STRIDE
    pairs = CMP_STRIDE // 2
    rows_pp = sub_per_page * SUBLANES
    even = (lax.broadcasted_iota(jnp.int32, (rows_pp, 1), 0) & (SUBLANES - 1)) < N_KV
    for p, pref in enumerate(page_refs, first_page):
        x4 = pref[0].reshape(sub_per_page, pairs, SUBLANES, HEAD_DIM)
        rs = slice(p * rows_pp, (p + 1) * rows_pp)
        for j in range(pairs):
            t = x4[:, j].reshape(rows_pp, HEAD_DIM)
            lhs_scr[rs, 2 * j * HEAD_DIM:(2 * j + 1) * HEAD_DIM] = jnp.where(even, t, 0.0).astype(BF16)
            lhs_scr[rs, (2 * j + 1) * HEAD_DIM:(2 * j + 2) * HEAD_DIM] = jnp.where(even, 0.0, t).astype(BF16)


def _compress_finish(new_ref, wcat_ref, w2_ref, pe_ref, t_scr, p_scr, *, has_new):
    n_rows = p_scr.shape[0]
    n_sub = n_rows // SUBLANES
    wcat = wcat_ref[...]
    t_scr[n_rows:, :] = jnp.zeros((2 * SUBLANES, 2 * HEAD_DIM), F32)
    t_scr[0:n_rows, :] = t_scr[0:n_rows, :] + t_scr[N_KV:n_rows + N_KV, :]
    if has_new:
        new8 = jnp.concatenate([new_ref[0, :, g * HEAD_DIM:(g + 1) * HEAD_DIM] for g in range(N_KV)]
                               + [jnp.zeros((SUBLANES - N_KV, HEAD_DIM), F32)], axis=0)
        t_scr[n_rows:n_rows + SUBLANES, HEAD_DIM:] = _dot(new8.astype(BF16), wcat[0:HEAD_DIM, HEAD_DIM:])
    pe_r = _dot(pe_ref[...].astype(BF16), wcat)
    pe_const = pe_r[0:1, :HEAD_DIM] + pe_r[1:2, HEAD_DIM:]
    p_scr[...] = t_scr[0:n_rows, :HEAD_DIM] + t_scr[SUBLANES:n_rows + SUBLANES, HEAD_DIM:] + pe_const
    pre = jnp.concatenate([p_scr[pl.ds(g, n_sub, stride=SUBLANES), :] for g in range(N_KV)], axis=0)
    out = _dot(_silu(pre).astype(BF16), w2_ref[...])
    if not has_new:
        blk = lax.broadcasted_iota(jnp.int32, (N_KV * n_sub, 1), 0) & (n_sub - 1)
        out = jnp.where(blk == n_sub - 1, 0.0, out)
    return out


def _compress_kv(k_pages, v_pages, k_new, v_new, k_w, v_w, scr, *, has_new):
    rows_pp = (PAGE_SIZE // CMP_STRIDE) * SUBLANES
    for p0 in range(0, len(k_pages), COMPRESS_CHUNK_PAGES):
        p1 = min(p0 + COMPRESS_CHUNK_PAGES, len(k_pages))
        rs = slice(p0 * rows_pp, p1 * rows_pp)
        _compress_fill(k_pages[p0:p1], scr[0], p0)
        _compress_fill(v_pages[p0:p1], scr[3], p0)
        scr[1][rs, :] = _dot(scr[0][rs, :], k_w[0][...])
        scr[4][rs, :] = _dot(scr[3][rs, :], v_w[0][...])
    kc = _compress_finish(k_new, *k_w, scr[1], scr[2], has_new=has_new)
    vc = _compress_finish(v_new, *v_w, scr[4], scr[5], has_new=has_new)
    return kc, vc


def _compress_scratch(n_pages):
    n_rows = n_pages * (PAGE_SIZE // CMP_STRIDE) * SUBLANES
    return [pltpu.VMEM((n_rows, CMP_STRIDE * HEAD_DIM), BF16),
            pltpu.VMEM((n_rows + 2 * SUBLANES, 2 * HEAD_DIM), F32),
            pltpu.VMEM((n_rows, HEAD_DIM), F32)]


def _compress_prompt_kernel(pt_ref, *refs, n_pages):
    kp, vp = refs[:n_pages], refs[n_pages:2 * n_pages]
    wck, w2k, pek, wcv, w2v, pev, kc_ref, vc_ref, *scr = refs[2 * n_pages:]
    n_sub = n_pages * (PAGE_SIZE // CMP_STRIDE)
    assert n_sub & (n_sub - 1) == 0
    kc_ref[0], vc_ref[0] = _compress_kv(kp, vp, None, None, (wck, w2k, pek), (wcv, w2v, pev), scr, has_new=False)


def _page_specs(n_pages):
    return [pl.BlockSpec((1, PAGE_SIZE * N_KV, HEAD_DIM), lambda b, pt, p=p: (pt[b, p], 0, 0)) for p in range(n_pages)]


def _const_spec(shape):
    return pl.BlockSpec(shape, lambda b, pt: (0,) * len(shape))


def _compress_prompt(k_rows, v_rows, cw):
    bsz = k_rows.shape[0]
    t = k_rows.shape[1] // N_KV
    n_pages = t // PAGE_SIZE
    n_sub = t // CMP_STRIDE
    pt = jnp.arange(bsz * n_pages, dtype=jnp.int32).reshape(bsz, n_pages)
    kp = k_rows.reshape(bsz * n_pages, PAGE_SIZE * N_KV, HEAD_DIM)
    vp = v_rows.reshape(bsz * n_pages, PAGE_SIZE * N_KV, HEAD_DIM)
    wspecs = [_const_spec(a.shape) for a in cw]
    grid_spec = pltpu.PrefetchScalarGridSpec(
        num_scalar_prefetch=1,
        grid=(bsz,),
        in_specs=_page_specs(n_pages) * 2 + wspecs,
        out_specs=[pl.BlockSpec((1, N_KV * n_sub, HEAD_DIM), lambda b, pt: (b, 0, 0))] * 2,
        scratch_shapes=_compress_scratch(n_pages) * 2,
    )
    return pl.pallas_call(
        functools.partial(_compress_prompt_kernel, n_pages=n_pages),
        grid_spec=grid_spec,
        out_shape=[jax.ShapeDtypeStruct((bsz, N_KV * n_sub, HEAD_DIM), F32)] * 2,
        compiler_params=_params("parallel"),
        name="compress_prompt",
    )(pt, *([kp] * n_pages), *([vp] * n_pages), *cw)


def _head_rows(ref, base):
    return jnp.concatenate([ref[0, :, base + h * HEAD_DIM:base + (h + 1) * HEAD_DIM] for h in range(N_HEADS)], axis=0)


def _split_dot(a, b):
    hi = a.astype(BF16)
    lo = (a - hi.astype(F32)).astype(BF16)
    return _dot(hi, b) + _dot(lo, b)


def _compress_sample_kernel(pt_ref, *refs, n_pages, cur_blk):
    kp, vp = refs[:n_pages], refs[n_pages:2 * n_pages]
    (newk, newv, wck, w2k, pek, wcv, w2v, pev, q_ref, tcs_ref, ovl_ref, pow_ref,
     oc_ref, bits_ref, *scr) = refs[2 * n_pages:]
    kc, vc = _compress_kv(kp, vp, newk, newv, (wck, w2k, pek), (wcv, w2v, pev), scr, has_new=True)
    q16 = (_head_rows(q_ref, 0) * SCALE).astype(BF16)
    s = _dot_nt(q16, kc.astype(BF16)) + tcs_ref[...]
    e = jnp.exp(s - jnp.max(s, axis=-1, keepdims=True))
    p = e / jnp.sum(e, axis=-1, keepdims=True)
    oc_ref[0] = _dot(p.astype(BF16), vc.astype(BF16))
    imp_h = _split_dot(p, ovl_ref[...])
    imp = jnp.concatenate([jnp.sum(imp_h[GROUP_REP * g:GROUP_REP * (g + 1)], axis=0, keepdims=True)
                           for g in range(N_KV)], axis=0)
    lane = lax.broadcasted_iota(jnp.int32, (1, LANES), 1)
    forced = (lane == cur_blk) | (lane == 0)
    imp = jnp.where(forced, FORCE_SCORE, jnp.where(lane <= cur_blk, imp, -FORCE_SCORE))
    rank = jnp.zeros((N_KV, LANES), F32)
    for k in range(cur_blk + 1):
        ck = imp[:, k:k + 1]
        rank = rank + jnp.where(ck > imp, 1.0, jnp.where((ck == imp) & (lane > k), 1.0, 0.0))
    sel = jnp.where(rank < N_SEL, 1.0, 0.0) * pow_ref[...]
    lo = jnp.sum(jnp.where(lane < 16, sel, 0.0), axis=-1, keepdims=True).astype(jnp.int32)
    hi = jnp.sum(jnp.where((lane >= 16) & (lane < 32), sel, 0.0), axis=-1, keepdims=True).astype(jnp.int32)
    bits_ref[0] = jnp.broadcast_to(lo | (hi << 16), (N_KV, LANES))


def _compress_sample(cache_k, cache_v, page_table, new_k, new_v, cw, q, tcs, ovl, *, past):
    nb, n_pages = page_table.shape
    width = new_k.shape[-1]
    n_sub = n_pages * (PAGE_SIZE // CMP_STRIDE)
    pow2 = np.zeros((1, LANES), np.float32)
    pow2[0, :32] = 2.0 ** (np.arange(32) % 16)
    row_spec = pl.BlockSpec((1, 1, width), lambda b, pt: (b, 0, 0))
    grid_spec = pltpu.PrefetchScalarGridSpec(
        num_scalar_prefetch=1,
        grid=(nb,),
        in_specs=(_page_specs(n_pages) * 2 + [row_spec, row_spec] + [_const_spec(a.shape) for a in cw]
                  + [pl.BlockSpec((1, 1, q.shape[-1]), lambda b, pt: (b, 0, 0)),
                     _const_spec(tcs.shape), _const_spec(ovl.shape), _const_spec(pow2.shape)]),
        out_specs=[pl.BlockSpec((1, N_HEADS, HEAD_DIM), lambda b, pt: (b, 0, 0)),
                   pl.BlockSpec((1, N_KV, LANES), lambda b, pt: (b, 0, 0))],
        scratch_shapes=_compress_scratch(n_pages) * 2,
    )
    return pl.pallas_call(
        functools.partial(_compress_sample_kernel, n_pages=n_pages, cur_blk=past // SLC_BLOCK),
        grid_spec=grid_spec,
        out_shape=[jax.ShapeDtypeStruct((nb, N_HEADS, HEAD_DIM), F32),
                   jax.ShapeDtypeStruct((nb, N_KV, LANES), jnp.int32)],
        compiler_params=_params("parallel"),
        name="compress_select_sample",
    )(page_table, *([cache_k] * n_pages), *([cache_v] * n_pages), new_k, new_v, *cw, q, tcs, ovl, jnp.asarray(pow2))


def _attn_prompt_kernel(q_ref, z0_ref, z1_ref, z2_ref, gt_ref, kc_ref, vc_ref, ks_ref, vs_ref, kw_ref, vw_ref,
                        t0_ref, t1_ref, tc_ref, ovl_ref, y_ref, m0_scr, m1_scr, acc0_scr, acc1_scr, sel_scr):
    m_scrs, acc_scrs = (m0_scr, m1_scr), (acc0_scr, acc1_scr)
    qt = pl.program_id(2)
    q = q_ref[...]
    qf = jnp.concatenate([q[:, r * HEAD_DIM:(r + 1) * HEAD_DIM] for r in range(GROUP_REP)], axis=0)
    qf = (qf * (SCALE * LOG2E)).astype(BF16)
    lane_q = lax.broadcasted_iota(jnp.int32, (1, TQ), 1)
    qpos = qt * TQ + lane_q
    key_l = lax.broadcasted_iota(jnp.int32, (TQ, 1), 0)
    gate_t = gt_ref[...].T
    z_refs = (z0_ref, z1_ref, z2_ref)

    def emit(br, r, o_t, first):
        cs = slice(r * HEAD_DIM, (r + 1) * HEAD_DIM)
        term = o_t.T * z_refs[br][:, cs]
        if first:
            y_ref[:, cs] = term
        else:
            y_ref[:, cs] += term

    blk_per_tile = TQ // SLC_BLOCK

    def compressed_and_select():
        n_cmp = kc_ref.shape[1]
        s_all = _dot_nt(kc_ref[0].astype(BF16), qf)
        vc_t = vc_ref[0].T.astype(BF16)
        cmp_end = lax.broadcasted_iota(jnp.int32, (n_cmp, 1), 0) * CMP_STRIDE + (CMP_LEN - 1)
        vis = qpos >= cmp_end
        any_vis = qpos >= CMP_LEN - 1
        row0 = pl.multiple_of(TQ // 2 - (TQ // CMP_STRIDE) * qt, CMP_STRIDE)
        psum = jnp.zeros((n_cmp, TQ), F32)
        for r in range(GROUP_REP):
            s = s_all[:, r * TQ:(r + 1) * TQ] + tc_ref[r, pl.ds(row0, n_cmp), :]
            s = jnp.where(vis, s, NEG_INF)
            e = jnp.exp2(s - jnp.max(s, axis=0, keepdims=True))
            p = jnp.where(any_vis, e / jnp.sum(e, axis=0, keepdims=True), 0.0)
            psum = psum + p
            emit(0, r, _dot(vc_t, p.astype(BF16)) * gate_t[r:r + 1, :], True)
        n_slc = ovl_ref.shape[0]
        psum_hi = psum.astype(BF16)
        imp = _dot(ovl_ref[...], psum_hi) + _dot(ovl_ref[...], (psum - psum_hi.astype(F32)).astype(BF16))
        blk = lax.broadcasted_iota(jnp.int32, (n_slc, 1), 0)
        cur = lax.shift_right_logical(qpos, int(math.log2(SLC_BLOCK)))
        forced = (blk == cur) | (blk == 0)
        imp = jnp.where(forced, FORCE_SCORE, jnp.where(blk <= cur, imp, -FORCE_SCORE))
        rank = jnp.zeros((n_slc, TQ), F32)
        for k in range(n_slc):
            rk = imp[k:k + 1, :]
            rank = rank + jnp.where(rk > imp, 1.0, jnp.where((rk == imp) & (blk > k), 1.0, 0.0))
        sel = jnp.where(rank < min(N_SEL, n_slc), 1.0, 0.0)
        for t in range(n_slc // blk_per_tile):
            sel_scr[t, 0:blk_per_tile, :] = sel[t * blk_per_tile:(t + 1) * blk_per_tile, :]

    ones_rows = jnp.ones((ACC_ROWS - HEAD_DIM, TQ), BF16)

    refs = ((ks_ref, vs_ref), (kw_ref, vw_ref))

    def attend(tiles, first):
        k_ts, v_ts, masks = [], [], []
        for st, kt, kind in tiles:
            k_ref, v_ref = refs[st]
            start = pl.multiple_of(kt * TQ, TQ)
            k_ts.append(k_ref[pl.ds(start, TQ), :])
            v_t = v_ref[pl.ds(start, TQ), :].T.astype(BF16)
            v_ts.append(jnp.concatenate([v_t, ones_rows], axis=0))
            mask = None
            if kind == "diag":
                mask = key_l <= lane_q
            elif kind == "winfar":
                mask = key_l > lane_q
            if st == 0:
                sel4 = sel_scr[kt, 0:blk_per_tile, :]
                selm = jnp.concatenate(
                    [jnp.broadcast_to(sel4[j:j + 1, :], (SLC_BLOCK, TQ)) for j in range(blk_per_tile)], axis=0) > 0.5
                mask = selm if mask is None else mask & selm
            masks.append(mask)
        s_all = _dot_nt(k_ts[0] if len(k_ts) == 1 else jnp.concatenate(k_ts, axis=0), qf)
        for r in range(GROUP_REP):
            sl = slice(r * TQ, (r + 1) * TQ)
            for st in sorted(set(t[0] for t in tiles)):
                ss, vs = [], []
                for i, ((st_i, kt, kind), mask) in enumerate(zip(tiles, masks)):
                    if st_i != st:
                        continue
                    s = s_all[i * TQ:(i + 1) * TQ, sl]
                    if kind == "diag":
                        s = s + t0_ref[r]
                    elif kind == "near":
                        s = s + t1_ref[r]
                    if mask is not None:
                        s = jnp.where(mask, s, NEG_INF)
                    ss.append(s)
                    vs.append(v_ts[i])
                mx = jnp.max(ss[0], axis=0, keepdims=True)
                for s in ss[1:]:
                    mx = jnp.maximum(mx, jnp.max(s, axis=0, keepdims=True))
                if first:
                    m_new = mx
                    upd = None
                else:
                    m_old = m_scrs[st][:, sl]
                    m_new = jnp.maximum(m_old, mx)
                    upd = jnp.exp2(m_old - m_new) * acc_scrs[st][:, sl]
                for s, v_t in zip(ss, vs):
                    pv = _dot(v_t, jnp.exp2(s - m_new).astype(BF16))
                    upd = pv if upd is None else upd + pv
                acc_scrs[st][:, sl] = upd
                m_scrs[st][:, sl] = m_new

    compressed_and_select()
    attend([(0, qt, "diag"), (1, qt, "diag")], True)

    @pl.when(qt >= 1)
    def _():
        attend([(0, qt - 1, "near"), (1, qt - 1, "near")], False)

    @pl.when(qt >= 2)
    def _():
        n_far = qt - 1

        def far_pair(i, carry):
            attend([(0, 2 * i, "far"), (0, 2 * i + 1, "far")], False)
            return carry

        lax.fori_loop(0, lax.shift_right_logical(n_far, 1), far_pair, 0)

        @pl.when((n_far & 1) == 1)
        def _():
            attend([(1, qt - 2, "winfar"), (0, n_far - 1, "far")], False)

        @pl.when((n_far & 1) == 0)
        def _():
            attend([(1, qt - 2, "winfar")], False)

    for st in range(2):
        br = st + 1
        for r in range(GROUP_REP):
            sl = slice(r * TQ, (r + 1) * TQ)
            w = gate_t[br * GROUP_REP + r:br * GROUP_REP + r + 1, :] / acc_scrs[st][HEAD_DIM:HEAD_DIM + 1, sl]
            emit(br, r, acc_scrs[st][0:HEAD_DIM, sl] * w, False)


def _attn_prompt(proj, kc, vc, k_slc, v_slc, k_win, v_win, t0, t1, tc, ovl_t, *, bsz, t):
    nq = t // TQ
    gw = GROUP_REP * HEAD_DIM
    assert TQ // 2 - (TQ // CMP_STRIDE) * (nq - 1) >= 0 and kc.shape[1] == N_KV * (t // CMP_STRIDE)
    n_cmp = t // CMP_STRIDE
    d_att = N_HEADS * HEAD_DIM
    zoff = d_att // gw
    row = lambda b, g, i: b * nq + i
    in_specs = [
        pl.BlockSpec((TQ, gw), lambda b, g, i: (row(b, g, i), g)),
    ] + [
        pl.BlockSpec((TQ, gw), lambda b, g, i, br=br: (row(b, g, i), zoff * (1 + br) + g)) for br in range(N_BRANCH)
    ] + [
        pl.BlockSpec((TQ, LANES), lambda b, g, i: (row(b, g, i), (1 + N_BRANCH) * d_att // LANES + g)),
        pl.BlockSpec((1, n_cmp, HEAD_DIM), lambda b, g, i: (b, g, 0)),
        pl.BlockSpec((1, n_cmp, HEAD_DIM), lambda b, g, i: (b, g, 0)),
    ] + [pl.BlockSpec((None, None, t, HEAD_DIM), lambda b, g, i: (b, g, 0, 0))] * 4 + [
        pl.BlockSpec((GROUP_REP, TQ, TQ), lambda b, g, i: (g, 0, 0))] * 3 + [
        pl.BlockSpec(ovl_t.shape, lambda b, g, i: (0, 0)),
    ]
    return pl.pallas_call(
        _attn_prompt_kernel,
        grid=(bsz, N_KV, nq),
        in_specs=in_specs,
        out_specs=pl.BlockSpec((TQ, gw), lambda b, g, i: (row(b, g, i), g)),
        out_shape=jax.ShapeDtypeStruct((bsz * t, d_att), F32),
        scratch_shapes=[
            pltpu.VMEM((1, GROUP_REP * TQ), F32),
            pltpu.VMEM((1, GROUP_REP * TQ), F32),
            pltpu.VMEM((ACC_ROWS, GROUP_REP * TQ), F32),
            pltpu.VMEM((ACC_ROWS, GROUP_REP * TQ), F32),
            pltpu.VMEM((t // TQ, SUBLANES, TQ), F32),
        ],
        compiler_params=_params("parallel", "parallel", "arbitrary"),
        name="nsa_attention_prompt",
    )(proj, proj, proj, proj, proj, kc, vc, k_slc, v_slc, k_win, v_win, t0, t1, tc, ovl_t)


def _attn_sample_kernel(bits_ref, pt_ref, *refs, n_pages):
    kp, vp = refs[:n_pages], refs[n_pages:2 * n_pages]
    (kwin_ref, vwin_ref, nks_ref, nvs_ref, nkw_ref, nvw_ref, q_ref, oc_ref, ts_ref, tw_ref,
     y_ref, kwo_ref, vwo_ref) = refs[2 * n_pages:]
    b = pl.program_id(0)
    rows_pp = PAGE_SIZE * N_KV
    blk_rows = SLC_BLOCK * N_KV
    d_att = N_HEADS * HEAD_DIM
    n_win_rows = kwin_ref.shape[1]
    head_rows = functools.partial(_head_rows, q_ref)

    for src, new, dst in ((kwin_ref, nkw_ref, kwo_ref), (vwin_ref, nvw_ref, vwo_ref)):
        dst[0, 0:n_win_rows - N_KV, :] = src[0, N_KV:n_win_rows, :]
        dst[0, n_win_rows - N_KV:n_win_rows, :] = jnp.concatenate(
            [new[0, :, g * HEAD_DIM:(g + 1) * HEAD_DIM] for g in range(N_KV)], axis=0)

    def group_rows(ref):
        return jnp.concatenate([ref[0, :, (h // GROUP_REP) * HEAD_DIM:(h // GROUP_REP + 1) * HEAD_DIM]
                                for h in range(N_HEADS)], axis=0)

    def softmax_pv(s_tiles, v_tiles, s_new, v_new):
        m = s_new
        for s in s_tiles:
            m = jnp.maximum(m, jnp.max(s, axis=-1, keepdims=True))
        p_new = jnp.exp(s_new - m)
        l = p_new
        acc = p_new * v_new
        for s, v in zip(s_tiles, v_tiles):
            p = jnp.exp(s - m)
            l = l + jnp.sum(p, axis=-1, keepdims=True)
            acc = acc + _dot(p.astype(BF16), v.astype(BF16))
        return acc / l

    q32 = head_rows(0) * SCALE
    q16 = q32.astype(BF16)
    head = lax.broadcasted_iota(jnp.int32, (N_HEADS, 1), 0)
    bits = jnp.zeros((N_HEADS, 1), jnp.int32)
    for g in range(N_KV):
        bits = jnp.where((head >= g * GROUP_REP) & (head < (g + 1) * GROUP_REP), bits_ref[b, g], bits)
    lane_blk = lax.broadcasted_iota(jnp.int32, (1, rows_pp), 1) // blk_rows
    s_tiles, v_tiles = [], []
    for p in range(n_pages):
        s = _dot_nt(q16, kp[p][0].astype(BF16)) + ts_ref[:, p * rows_pp:(p + 1) * rows_pp]
        shift = jnp.broadcast_to(lane_blk + p * (rows_pp // blk_rows), s.shape)
        sel = lax.shift_right_logical(jnp.broadcast_to(bits, s.shape), shift) & 1
        s_tiles.append(jnp.where(sel == 1, s, NEG_INF))
        v_tiles.append(vp[p][0])
    tail = n_pages * rows_pp
    s_new = jnp.sum(q32 * group_rows(nks_ref), axis=-1, keepdims=True) + ts_ref[:, tail:tail + 1]
    o_slc = softmax_pv(s_tiles, v_tiles, s_new, group_rows(nvs_ref))
    s_tiles, v_tiles = [], []
    for p in range(n_win_rows // rows_pp):
        rows = slice(p * rows_pp, (p + 1) * rows_pp)
        s_tiles.append(_dot_nt(q16, kwin_ref[0, rows, :].astype(BF16)) + tw_ref[:, rows])
        v_tiles.append(vwin_ref[0, rows, :])
    s_new = jnp.sum(q32 * group_rows(nkw_ref), axis=-1, keepdims=True) + tw_ref[:, n_win_rows:n_win_rows + 1]
    o_win = softmax_pv(s_tiles, v_tiles, s_new, group_rows(nvw_ref))
    y = jnp.zeros((N_HEADS, HEAD_DIM), F32)
    gate0 = (1 + N_BRANCH) * d_att
    for br, o in enumerate((oc_ref[0], o_slc, o_win)):
        cols = [gate0 + (h // GROUP_REP) * LANES + br * GROUP_REP + h % GROUP_REP for h in range(N_HEADS)]
        gate = jnp.concatenate([q_ref[0, :, c:c + 1] for c in cols], axis=0)
        y = y + gate * o * head_rows((1 + br) * d_att)
    y_ref[0] = y


def _attn_sample(cache_k, cache_v, page_table, bits, k_win, v_win, new_rows, proj, o_cmp, ts, tw):
    nb, n_pages = page_table.shape
    rows_pp = PAGE_SIZE * N_KV
    win_spec = pl.BlockSpec((1, k_win.shape[1], HEAD_DIM), lambda b, bits, pt: (b, 0, 0))
    row_spec = lambda w: pl.BlockSpec((1, 1, w), lambda b, bits, pt: (b, 0, 0))
    const = lambda a: pl.BlockSpec(a.shape, lambda b, bits, pt: (0,) * a.ndim)
    page_specs = [pl.BlockSpec((1, rows_pp, HEAD_DIM), lambda b, bits, pt, p=p: (pt[b, p], 0, 0)) for p in range(n_pages)]
    grid_spec = pltpu.PrefetchScalarGridSpec(
        num_scalar_prefetch=2,
        grid=(nb,),
        in_specs=(page_specs * 2 + [win_spec] * 2
                  + [row_spec(N_KV * HEAD_DIM)] * 4
                  + [row_spec(proj.shape[-1]),
                     pl.BlockSpec((1, N_HEADS, HEAD_DIM), lambda b, bits, pt: (b, 0, 0)),
                     const(ts), const(tw)]),
        out_specs=[pl.BlockSpec((1, N_HEADS, HEAD_DIM), lambda b, bits, pt: (b, 0, 0)), win_spec, win_spec],
    )
    return pl.pallas_call(
        functools.partial(_attn_sample_kernel, n_pages=n_pages),
        grid_spec=grid_spec,
        out_shape=[jax.ShapeDtypeStruct((nb, N_HEADS, HEAD_DIM), F32),
                   jax.ShapeDtypeStruct(k_win.shape, F32), jax.ShapeDtypeStruct(v_win.shape, F32)],
        compiler_params=_params("parallel"),
        name="nsa_attention_sample",
    )(bits, page_table, *([cache_k] * n_pages), *([cache_v] * n_pages), k_win, v_win, *new_rows,
      proj, o_cmp, ts, tw)


def _overlap(n_cmp, n_slc):
    cs = np.arange(n_cmp) * CMP_STRIDE
    bs = np.arange(n_slc) * SLC_BLOCK
    ov = np.clip(np.minimum(cs[:, None] + CMP_LEN, bs[None, :] + SLC_BLOCK) - np.maximum(cs[:, None], bs[None, :]), 0, None)
    return (ov / CMP_LEN).astype(np.float32)


def _skip_unselected_pages(page_table, bits):
    nb, n_pages = page_table.shape
    blk_per_page = PAGE_SIZE // SLC_BLOCK
    assert n_pages * blk_per_page <= 32
    any_bits = functools.reduce(jnp.bitwise_or, [bits[:, g] for g in range(bits.shape[1])])
    shifts = blk_per_page * jnp.arange(n_pages, dtype=jnp.int32)
    needed = ((any_bits[:, None] >> shifts[None, :]) & ((1 << blk_per_page) - 1)) != 0
    seq = jnp.arange(nb, dtype=jnp.int32)[:, None]
    last_needed = lax.cummax(jnp.where(needed | (seq == 0), seq, 0), axis=0)
    return jnp.take_along_axis(page_table, last_needed, axis=0)


def _compress_weights(w1, w2, pe):
    half = CMP_STRIDE * HEAD_DIM
    wcat = jnp.concatenate([w1[:half], w1[half:]], axis=1).astype(BF16)
    pe8 = jnp.pad(pe.reshape(2, half), ((0, SUBLANES - 2), (0, 0)))
    return wcat, w2.astype(BF16), pe8


def kernel(x_prompt, x_sample, state_pool, cache_k_cmp, cache_v_cmp, cache_k_slc, cache_v_slc, state_k_win, state_v_win, page_table, c_prompt, c_sample, g_norm, w_ada, b_ada, w_in_a, w_grp, pool_scale, w_out_a, g_kv, w_kv, pe_k, w_ck1, w_ck2, pe_v, w_cv1, w_cv2, rel_bias, w_in_b, w_out_b, g_final):
    bsz, t, d = x_prompt.shape
    nb = x_sample.shape[0]
    n_pages = page_table.shape[1]
    past = n_pages * PAGE_SIZE
    d_att = N_HEADS * HEAD_DIM
    d_kv = N_KV * HEAD_DIM
    n_a = w_in_a.shape[0]
    assert n_a == 1 and w_in_b.shape[0] == 1 and x_sample.shape[1] == 1
    assert t % TQ == 0 and t >= WINDOW and past % PAGE_SIZE == 0 and state_k_win.shape[1] == WINDOW

    w_in_a16 = w_in_a[0].astype(BF16)
    w_grp16 = w_grp[0].astype(BF16)
    w_out_a16 = w_out_a[0].astype(BF16)
    w_kv16 = w_kv.astype(BF16)
    w_out_b16 = w_out_b[0].astype(BF16)
    n_qz = (1 + N_BRANCH) * d_att
    wg = w_in_b[0][:, n_qz:].reshape(d, N_BRANCH, N_KV, GROUP_REP).transpose(0, 2, 1, 3)
    wg = jnp.pad(wg.reshape(d, N_KV, N_BRANCH * GROUP_REP), ((0, 0), (0, 0), (0, LANES - N_BRANCH * GROUP_REP)))
    n_proj = n_qz + TN_PROJ
    w_gate16 = jnp.pad(wg.reshape(d, N_KV * LANES), ((0, 0), (0, TN_PROJ - N_KV * LANES))).astype(BF16)
    nsa_proj = functools.partial(_nm_matmul, w=w_in_b[0].astype(BF16), w_tail=w_gate16, n_cols=n_proj, tn=TN_PROJ,
                                 silu_cols=d_att, sigmoid_cols=n_qz)
    cw_k = _compress_weights(w_ck1, w_ck2, pe_k)
    cw_v = _compress_weights(w_cv1, w_cv2, pe_v)
    cw = cw_k + cw_v

    mod = _ada(jnp.concatenate([c_prompt, c_sample], axis=0), w_ada, b_ada)

    def modulation(l, lo, hi, per_row):
        parts = [mod[l, lo:hi, k * d:(k + 1) * d] for k in range(3)]
        return [p[None] if per_row else p[:, None] for p in parts]

    t0, t1, tc, ts, tw, tcs = _bias_tables(rel_bias, past=past, n_cmp_s=past // CMP_STRIDE)
    ts, tw, tcs = ts.reshape(N_HEADS, -1), tw.reshape(N_HEADS, -1), tcs.reshape(N_HEADS, -1)

    shift, scale, gate = modulation(0, 0, bsz, False)
    uz = _nm_matmul(x_prompt, g_norm[0], scale, shift, w_in_a16, n_cols=2 * d, tm=TM_PROJ, tn=TN_PROJ, name="in_proj_pool_prompt")
    pool_p = uz[:, t - POOL_BUF:, :d][None]
    x1 = _pool_prompt(uz, w_grp16, pool_scale[0], w_out_a16, x_prompt, gate, tm=256)
    kv_p = _kv_proj(x1, g_kv, w_kv16, tm=256, name="kv_proj_prompt")
    heads = lambda a: a.reshape(a.shape[0], -1, N_KV, HEAD_DIM)
    kv_state_p = [heads(a) for a in kv_p[:4]] + [heads(a[:, (t - WINDOW) * N_KV:]) for a in kv_p[4:6]]
    kc_p, vc_p = _compress_prompt(kv_p[0], kv_p[1], cw)
    shift, scale, gate = modulation(1, 0, bsz, False)
    proj = nsa_proj(x1, g_norm[1], scale, shift, tm=TM_PROJ, name="in_proj_nsa_prompt")
    ovl_t = jnp.asarray(np.pad(_overlap(t // CMP_STRIDE - 1, t // SLC_BLOCK), ((0, 1), (0, 0))).T).astype(BF16)
    flat = lambda a: a.reshape(bsz * t, a.shape[-1])
    y1 = _attn_prompt(flat(proj), kc_p, vc_p, kv_p[6], kv_p[7], kv_p[8], kv_p[9],
                      t0, t1, tc, ovl_t, bsz=bsz, t=t)
    y_prompt = _proj_residual(y1.reshape(bsz, t, d_att), w_out_b16, x1, gate, g_final, tm=512, final_norm=True,
                              name="out_proj_nsa_prompt")

    xs = x_sample.reshape(1, nb, d)
    shift, scale, gate = modulation(0, bsz, bsz + nb, True)
    uz_s = _nm_matmul(xs, g_norm[0], scale, shift, w_in_a16, n_cols=2 * d, tm=nb, tn=512, name="in_proj_pool_sample")
    pool_s = jnp.concatenate([state_pool[:, :, 1:], uz_s[0, :, None, :d][None]], axis=2)
    y0_s = _pool_sample(uz_s, state_pool[0], w_grp16, pool_scale[0], qpos=past, tb=min(32, nb))
    x1_s = _proj_residual(y0_s, w_out_a16, xs, gate, g_final, tm=nb, final_norm=False, name="out_proj_pool_sample")
    kv_s = _kv_proj(x1_s, g_kv, w_kv16, tm=nb, name="kv_proj_sample")
    new_rows = [a.reshape(nb, 1, d_kv) for a in kv_s[:2 * N_BRANCH]]
    shift, scale, gate = modulation(1, bsz, bsz + nb, True)
    proj_s = nsa_proj(x1_s, g_norm[1], scale, shift, tm=nb, name="in_proj_nsa_sample").reshape(nb, 1, n_proj)
    n_cmp_s = past // CMP_STRIDE
    n_slc_s = past // SLC_BLOCK + 1
    ovl_s = jnp.asarray(np.tile(np.pad(_overlap(n_cmp_s, n_slc_s), ((0, 0), (0, LANES - n_slc_s))), (N_KV, 1))).astype(BF16)
    rows_of = lambda c: c.reshape(c.shape[0], PAGE_SIZE * N_KV, HEAD_DIM)
    o_cmp, bits = _compress_sample(rows_of(cache_k_cmp), rows_of(cache_v_cmp), page_table, new_rows[0], new_rows[1], cw,
                                   proj_s, tcs, ovl_s, past=past)
    bits = bits[:, :, 0]
    y1_s, k_win_s, v_win_s = _attn_sample(
        rows_of(cache_k_slc), rows_of(cache_v_slc), _skip_unselected_pages(page_table, bits), bits,
        state_k_win.reshape(nb, WINDOW * N_KV, HEAD_DIM), state_v_win.reshape(nb, WINDOW * N_KV, HEAD_DIM),
        new_rows[2:], proj_s, o_cmp, ts, tw)
    y_sample = _proj_residual(y1_s.reshape(1, nb, d_att), w_out_b16, x1_s, gate, g_final, tm=nb, final_norm=True,
                              name="out_proj_nsa_sample").reshape(nb, 1, d)
    new4 = [a.reshape(nb, 1, N_KV, HEAD_DIM) for a in kv_s[:2 * N_BRANCH]]
    kv_state_s = new4[:4] + [k_win_s.reshape(state_k_win.shape), v_win_s.reshape(state_v_win.shape)]

    return (y_prompt, y_sample, pool_p, *kv_state_p, pool_s, *kv_state_s)
```

```python
import functools
import math

import numpy as np
import jax
import jax.numpy as jnp
from jax import lax
from jax.experimental import pallas as pl
from jax.experimental.pallas import tpu as pltpu

F32 = jnp.float32
BF16 = jnp.bfloat16

PAGE_SIZE = 128
POOL_WINDOWS = (2, 4, 8, 16)
POOL_BUF = max(POOL_WINDOWS) - 1
HEAD_DIM = 128
N_KV = 4
GROUP_REP = 4
N_HEADS = N_KV * GROUP_REP
N_BRANCH = 3
CMP_LEN = 32
CMP_STRIDE = 16
SLC_BLOCK = 64
N_SEL = 8
WINDOW = 512
N_BUCKETS = 32
MAX_DISTANCE = 128
RMS_EPS = 1e-6
SCALE = HEAD_DIM ** -0.5
LOG2E = math.log2(math.e)
NEG_INF = -1e30
FORCE_SCORE = 1e9

LANES = 128
SUBLANES = 8
VMEM_LIMIT_BYTES = 56 * 1024 * 1024
TQ = 256
COMPRESS_CHUNK_PAGES = 4
ACC_ROWS = HEAD_DIM + 16
TN_PROJ = 1024
TM_PROJ = 1024
HALO = 16


def _bucket_starts():
    max_exact = N_BUCKETS // 2
    d = np.arange(0, MAX_DISTANCE + 1)
    large = max_exact + np.floor(
        np.log(np.maximum(d, max_exact) / max_exact) / math.log(MAX_DISTANCE / max_exact) * (N_BUCKETS - max_exact)
    ).astype(np.int64)
    bucket = np.where(d < max_exact, d, np.minimum(large, N_BUCKETS - 1))
    return [int(np.argmax(bucket >= k)) for k in range(N_BUCKETS)]


BUCKET_STARTS = _bucket_starts()
FAR_DIST = BUCKET_STARTS[-1]


def _params(*sem):
    return pltpu.CompilerParams(dimension_semantics=sem, vmem_limit_bytes=VMEM_LIMIT_BYTES)


def _silu(x):
    return x * jax.nn.sigmoid(x)


def _rms(x, g):
    return x * lax.rsqrt(jnp.mean(x * x, axis=-1, keepdims=True) + RMS_EPS) * g


def _dot(a, b):
    return jnp.dot(a, b, preferred_element_type=F32)


def _dot_nt(a, b):
    return lax.dot_general(a, b, (((1,), (1,)), ((), ())), preferred_element_type=F32)


def _ada_kernel(c_ref, w_ref, b_ref, o_ref):
    a = _silu(c_ref[...]).astype(BF16)
    o_ref[0] = _dot(a, w_ref[0].astype(BF16)) + b_ref[0]


def _ada(c_all, w_ada, b_ada):
    depth, d, n = w_ada.shape
    m = c_all.shape[0]
    tn = 512
    return pl.pallas_call(
        _ada_kernel,
        grid=(depth, n // tn),
        in_specs=[
            pl.BlockSpec((m, d), lambda l, j: (0, 0)),
            pl.BlockSpec((1, d, tn), lambda l, j: (l, 0, j)),
            pl.BlockSpec((1, 1, tn), lambda l, j: (l, 0, j)),
        ],
        out_specs=pl.BlockSpec((1, m, tn), lambda l, j: (l, 0, j)),
        out_shape=jax.ShapeDtypeStruct((depth, m, n), F32),
        compiler_params=_params("parallel", "parallel"),
        name="ada_modulation",
    )(c_all, w_ada, b_ada.reshape(depth, 1, n))


def _nm_matmul_kernel(x_ref, g_ref, sc_ref, sh_ref, w_ref, *rest, silu_from, sigmoid_from):
    *tail, o_ref, h_scr = rest
    j = pl.program_id(2)

    @pl.when(j == 0)
    def _():
        y = _rms(x_ref[0], g_ref[...])
        h_scr[...] = (y * (1.0 + sc_ref[0]) + sh_ref[0]).astype(BF16)

    def column_tile(weight_ref):
        r = _dot(h_scr[...], weight_ref[...])
        if silu_from is None:
            o_ref[0] = r
        else:
            sig = jax.nn.sigmoid(r)
            o_ref[0] = jnp.where(j >= sigmoid_from, sig, jnp.where(j >= silu_from, r * sig, r))

    if tail:
        last = pl.num_programs(2) - 1
        pl.when(j < last)(lambda: column_tile(w_ref))
        pl.when(j == last)(lambda: column_tile(tail[0]))
    else:
        column_tile(w_ref)


def _mod_spec(arr, tm):
    d = arr.shape[-1]
    if arr.shape[1] == 1:
        return pl.BlockSpec((1, 1, d), lambda b, i, *_: (b, 0, 0))
    return pl.BlockSpec((1, tm, d), lambda b, i, *_: (b, i, 0))


def _nm_matmul(x, g, scale, shift, w, *, n_cols, tm, tn, silu_cols=None, sigmoid_cols=None, w_tail=None, name):
    bsz, rows, d = x.shape
    n_j = n_cols // tn
    if w_tail is None:
        w_specs, w_args = [pl.BlockSpec((d, tn), lambda b, i, j: (0, j))], [w]
    else:
        w_specs = [pl.BlockSpec((d, tn), lambda b, i, j: (0, jnp.minimum(j, n_j - 2))),
                   pl.BlockSpec((d, tn), lambda b, i, j: (0, 0))]
        w_args = [w, w_tail]
    if silu_cols is None and sigmoid_cols is None:
        silu_from = sigmoid_from = None
    else:
        sigmoid_from = n_cols // tn if sigmoid_cols is None else sigmoid_cols // tn
        silu_from = sigmoid_from if silu_cols is None else silu_cols // tn
    return pl.pallas_call(
        functools.partial(_nm_matmul_kernel, silu_from=silu_from, sigmoid_from=sigmoid_from),
        grid=(bsz, rows // tm, n_cols // tn),
        in_specs=[
            pl.BlockSpec((1, tm, d), lambda b, i, j: (b, i, 0)),
            pl.BlockSpec((1, d), lambda b, i, j: (0, 0)),
            _mod_spec(scale, tm),
            _mod_spec(shift, tm),
        ] + w_specs,
        out_specs=pl.BlockSpec((1, tm, tn), lambda b, i, j: (b, i, j)),
        out_shape=jax.ShapeDtypeStruct((bsz, rows, n_cols), F32),
        scratch_shapes=[pltpu.VMEM((tm, d), BF16)],
        compiler_params=_params("parallel", "parallel", "arbitrary"),
        name=name,
    )(x, g.reshape(1, d), scale, shift, *w_args)


def _pool_mix(pooled_fn, z, wg_ref, ps_ref, store):
    grp = z.shape[-1] // len(POOL_WINDOWS)
    for gi, w in enumerate(POOL_WINDOWS):
        cs = slice(gi * grp, (gi + 1) * grp)
        mixed = _dot(pooled_fn(gi, w, cs).astype(BF16), wg_ref[gi])
        store(cs, mixed * ps_ref[:, cs] * _silu(z[:, cs]))


def _pool_prompt_kernel(u_ref, z_ref, halo_ref, wg_ref, ps_ref, wo_ref, x_ref, gate_ref, o_ref,
                        ue_scr, a_scr, b_scr, y_scr):
    i = pl.program_id(1)
    tm = u_ref.shape[1]
    top = SUBLANES + HALO
    n = top + tm
    u = u_ref[0]
    ue_scr[0:SUBLANES, :] = jnp.zeros((SUBLANES, u.shape[1]), F32)
    ue_scr[SUBLANES:top, :] = jnp.where(i > 0, halo_ref[0], 0.0)
    ue_scr[top:n, :] = u
    zero8 = jnp.zeros((SUBLANES, a_scr.shape[1]), F32)
    a_scr[0:SUBLANES, :] = zero8
    b_scr[0:SUBLANES, :] = zero8
    pos = i * tm + lax.broadcasted_iota(jnp.int32, (tm, 1), 0)

    def pooled(gi, w, cs):
        src = lambda lo, hi: ue_scr[lo:hi, cs]
        k = 1
        for dst in (a_scr, b_scr, a_scr):
            if 2 * k >= w:
                break
            dst[SUBLANES:n, :] = src(SUBLANES, n) + src(SUBLANES - k, n - k)
            src = lambda lo, hi, dst=dst: dst[lo:hi, :]
            k *= 2
        acc = src(top, n) + src(top - k, n - k)
        cnt = jnp.minimum(pos + 1, w).astype(F32)
        return acc / cnt - u[:, cs]

    def store(cs, val):
        y_scr[:, cs] = val.astype(BF16)

    _pool_mix(pooled, z_ref[0], wg_ref, ps_ref, store)
    o_ref[0] = x_ref[0] + gate_ref[0] * _dot(y_scr[...], wo_ref[...])


def _pool_prompt(uz, w_grp, pool_scale, w_out, x, gate, *, tm):
    bsz, t, d2 = uz.shape
    d = d2 // 2
    ng, grp, _ = w_grp.shape
    return pl.pallas_call(
        _pool_prompt_kernel,
        grid=(bsz, t // tm),
        in_specs=[
            pl.BlockSpec((1, tm, d), lambda b, i: (b, i, 0)),
            pl.BlockSpec((1, tm, d), lambda b, i: (b, i, 1)),
            pl.BlockSpec((1, HALO, d), lambda b, i: (b, jnp.maximum(i * (tm // HALO) - 1, 0), 0)),
            pl.BlockSpec((ng, grp, grp), lambda b, i: (0, 0, 0)),
            pl.BlockSpec((1, d), lambda b, i: (0, 0)),
            pl.BlockSpec(w_out.shape, lambda b, i: (0, 0)),
            pl.BlockSpec((1, tm, d), lambda b, i: (b, i, 0)),
            _mod_spec(gate, tm),
        ],
        out_specs=pl.BlockSpec((1, tm, d), lambda b, i: (b, i, 0)),
        out_shape=jax.ShapeDtypeStruct((bsz, t, d), F32),
        scratch_shapes=[pltpu.VMEM((SUBLANES + HALO + tm, d), F32),
                        pltpu.VMEM((SUBLANES + HALO + tm, grp), F32),
                        pltpu.VMEM((SUBLANES + HALO + tm, grp), F32),
                        pltpu.VMEM((tm, d), BF16)],
        compiler_params=_params("parallel", "parallel"),
        name="pool_mixer_prompt",
    )(uz, uz, uz, w_grp, pool_scale.reshape(1, d), w_out, x, gate)


def _pool_sample_kernel(u_ref, z_ref, pre_ref, wg_ref, ps_ref, y_ref, *, qpos):
    u = u_ref[0]

    def pooled(gi, w, cs):
        acc = u[:, cs]
        for k in range(1, w):
            acc = acc + pre_ref[:, POOL_BUF - k, cs]
        return acc / float(min(qpos + 1, w)) - u[:, cs]

    def store(cs, val):
        y_ref[0, :, cs] = val

    _pool_mix(pooled, z_ref[0], wg_ref, ps_ref, store)


def _pool_sample(uz, prefix, w_grp, pool_scale, *, qpos, tb):
    _, nb, d2 = uz.shape
    d = d2 // 2
    ng, grp, _ = w_grp.shape
    return pl.pallas_call(
        functools.partial(_pool_sample_kernel, qpos=qpos),
        grid=(nb // tb,),
        in_specs=[
            pl.BlockSpec((1, tb, d), lambda i: (0, i, 0)),
            pl.BlockSpec((1, tb, d), lambda i: (0, i, 1)),
            pl.BlockSpec((tb, POOL_BUF, d), lambda i: (i, 0, 0)),
            pl.BlockSpec((ng, grp, grp), lambda i: (0, 0, 0)),
            pl.BlockSpec((1, d), lambda i: (0, 0)),
        ],
        out_specs=pl.BlockSpec((1, tb, d), lambda i: (0, i, 0)),
        out_shape=jax.ShapeDtypeStruct((1, nb, d), F32),
        compiler_params=_params("parallel"),
        name="pool_mixer_sample",
    )(uz, uz, prefix, w_grp, pool_scale.reshape(1, d))


def _proj_residual_kernel(y_ref, w_ref, x_ref, gate_ref, g_ref, o_ref, *, final_norm):
    xo = x_ref[0] + gate_ref[0] * _dot(y_ref[0].astype(BF16), w_ref[...])
    o_ref[0] = _rms(xo, g_ref[...]) if final_norm else xo


def _proj_residual(y, w, x, gate, g_final, *, tm, final_norm, name):
    bsz, rows, d = x.shape
    dy = y.shape[-1]
    return pl.pallas_call(
        functools.partial(_proj_residual_kernel, final_norm=final_norm),
        grid=(bsz, rows // tm),
        in_specs=[
            pl.BlockSpec((1, tm, dy), lambda b, i: (b, i, 0)),
            pl.BlockSpec((dy, d), lambda b, i: (0, 0)),
            pl.BlockSpec((1, tm, d), lambda b, i: (b, i, 0)),
            _mod_spec(gate, tm),
            pl.BlockSpec((1, d), lambda b, i: (0, 0)),
        ],
        out_specs=pl.BlockSpec((1, tm, d), lambda b, i: (b, i, 0)),
        out_shape=jax.ShapeDtypeStruct((bsz, rows, d), F32),
        compiler_params=_params("parallel", "parallel"),
        name=name,
    )(y, w, x, gate, g_final.reshape(1, d))


KV_GROUP_COPIES = ((2, BF16), (3, F32), (4, BF16), (5, F32))


def _kv_kernel(x_ref, g_ref, w_ref, *o_refs):
    h = _rms(x_ref[0], g_ref[...]).astype(BF16)
    n_out = 2 * N_BRANCH
    tm = x_ref.shape[1]
    n = N_KV * HEAD_DIM
    copies = dict((o, (o_refs[n_out + i], dt)) for i, (o, dt) in enumerate(KV_GROUP_COPIES))
    for o in range(n_out):
        r = _dot(h, w_ref[:, o * n:(o + 1) * n])
        for g in range(N_KV):
            rg = r[:, g * HEAD_DIM:(g + 1) * HEAD_DIM]
            o_refs[o][0, pl.ds(g, tm, stride=N_KV), :] = rg
            if o in copies:
                c_ref, dt = copies[o]
                c_ref[0, g] = rg.astype(dt)


def _kv_proj(x, g_kv, w_kv, *, tm, name):
    bsz, rows, d = x.shape
    n_out = 2 * N_BRANCH
    return pl.pallas_call(
        _kv_kernel,
        grid=(bsz, rows // tm),
        in_specs=[
            pl.BlockSpec((1, tm, d), lambda b, i: (b, i, 0)),
            pl.BlockSpec((1, d), lambda b, i: (0, 0)),
            pl.BlockSpec(w_kv.shape, lambda b, i: (0, 0)),
        ],
        out_specs=([pl.BlockSpec((1, tm * N_KV, HEAD_DIM), lambda b, i: (b, i, 0))] * n_out
                   + [pl.BlockSpec((1, N_KV, tm, HEAD_DIM), lambda b, i: (b, 0, i, 0))] * len(KV_GROUP_COPIES)),
        out_shape=([jax.ShapeDtypeStruct((bsz, rows * N_KV, HEAD_DIM), F32)] * n_out
                   + [jax.ShapeDtypeStruct((bsz, N_KV, rows, HEAD_DIM), dt) for _, dt in KV_GROUP_COPIES]),
        compiler_params=_params("parallel", "parallel"),
        name=name,
    )(x, g_kv.reshape(1, d), w_kv)


def _bias_lookup(rb_ref, h, dist):
    acc = jnp.full(dist.shape, rb_ref[N_BUCKETS - 1, h], F32)
    for k in range(N_BUCKETS - 2, -1, -1):
        acc = jnp.where(dist < BUCKET_STARTS[k + 1], rb_ref[k, h], acc)
    return acc


def _bias_tables_kernel(rb_ref, t0_ref, t1_ref, tc_ref, ts_ref, tw_ref, tcs_ref, *, past):
    h = pl.program_id(0)
    key = lax.broadcasted_iota(jnp.int32, (TQ, TQ), 0)
    qry = lax.broadcasted_iota(jnp.int32, (TQ, TQ), 1)
    far = rb_ref[N_BUCKETS - 1, h]
    rel2 = lambda dist: (_bias_lookup(rb_ref, h, dist) - far) * LOG2E
    t0_ref[0] = rel2(qry - key)
    t1_ref[0] = rel2(TQ + qry - key)
    tc_ref[0] = rel2(qry - CMP_STRIDE * (key - TQ // 2) - (CMP_LEN - 1))
    ls = lax.broadcasted_iota(jnp.int32, ts_ref.shape[1:], 1)
    own = (ls % N_KV) == h // GROUP_REP
    tok = ls // N_KV
    ts_ref[0] = jnp.where(ls == N_KV * past, rb_ref[0, h],
                          jnp.where(own & (tok < past), _bias_lookup(rb_ref, h, past - tok), NEG_INF))
    lw = lax.broadcasted_iota(jnp.int32, tw_ref.shape[1:], 1)
    own = (lw % N_KV) == h // GROUP_REP
    dw = WINDOW - lw // N_KV
    tw_ref[0] = jnp.where(lw == N_KV * WINDOW, rb_ref[0, h],
                          jnp.where(own & (dw > 0) & (dw < WINDOW) & (past - dw >= 0), _bias_lookup(rb_ref, h, dw), NEG_INF))
    lc = lax.broadcasted_iota(jnp.int32, tcs_ref.shape[1:], 1)
    n_cmp = tcs_ref.shape[2] // N_KV
    dc = past - CMP_STRIDE * (lc % n_cmp) - (CMP_LEN - 1)
    tcs_ref[0] = jnp.where((lc // n_cmp == h // GROUP_REP) & (dc >= 0), _bias_lookup(rb_ref, h, dc), NEG_INF)


def _bias_tables(rel_bias, *, past, n_cmp_s):
    nh = rel_bias.shape[1]
    ls = N_KV * past + LANES
    lw = N_KV * WINDOW + LANES
    shapes = [(nh, TQ, TQ)] * 3 + [(nh, 1, ls), (nh, 1, lw), (nh, 1, N_KV * n_cmp_s)]
    return pl.pallas_call(
        functools.partial(_bias_tables_kernel, past=past),
        grid=(nh,),
        in_specs=[pl.BlockSpec(memory_space=pltpu.SMEM)],
        out_specs=[pl.BlockSpec((1,) + s[1:], lambda h: (h, 0, 0)) for s in shapes],
        out_shape=[jax.ShapeDtypeStruct(s, F32) for s in shapes],
        compiler_params=_params("parallel"),
        name="rel_bias_tables",
    )(rel_bias)


def _compress_fill(page_refs, lhs_scr, first_page):
    sub_per_page = PAGE_SIZE // CMP_STRIDE
    pairs = CMP_STRIDE // 2
    rows_pp = sub_per_page * SUBLANES
    even = (lax.broadcasted_iota(jnp.int32, (rows_pp, 1), 0) & (SUBLANES - 1)) < N_KV
    for p, pref in enumerate(page_refs, first_page):
        x4 = pref[...].reshape(sub_per_page, pairs, SUBLANES, HEAD_DIM)
        rs = slice(p * rows_pp, (p + 1) * rows_pp)
        for j in range(pairs):
            t = x4[:, j].reshape(rows_pp, HEAD_DIM)
            lhs_scr[rs, 2 * j * HEAD_DIM:(2 * j + 1) * HEAD_DIM] = jnp.where(even, t, 0.0).astype(BF16)
            lhs_scr[rs, (2 * j + 1) * HEAD_DIM:(2 * j + 2) * HEAD_DIM] = jnp.where(even, 0.0, t).astype(BF16)


def _compress_finish(new_ref, wcat_ref, w2_ref, pe_ref, t_scr, p_scr, *, has_new):
    n_rows = p_scr.shape[0]
    n_sub = n_rows // SUBLANES
    wcat = wcat_ref[...]
    t_scr[n_rows:, :] = jnp.zeros((2 * SUBLANES, 2 * HEAD_DIM), F32)
    t_scr[0:n_rows, :] = t_scr[0:n_rows, :] + t_scr[N_KV:n_rows + N_KV, :]
    if has_new:
        new8 = jnp.concatenate([new_ref[0, :, g * HEAD_DIM:(g + 1) * HEAD_DIM] for g in range(N_KV)]
                               + [jnp.zeros((SUBLANES - N_KV, HEAD_DIM), F32)], axis=0)
        t_scr[n_rows:n_rows + SUBLANES, HEAD_DIM:] = _dot(new8.astype(BF16), wcat[0:HEAD_DIM, HEAD_DIM:])
    pe_r = _dot(pe_ref[...].astype(BF16), wcat)
    pe_const = pe_r[0:1, :HEAD_DIM] + pe_r[1:2, HEAD_DIM:]
    p_scr[...] = t_scr[0:n_rows, :HEAD_DIM] + t_scr[SUBLANES:n_rows + SUBLANES, HEAD_DIM:] + pe_const
    pre = jnp.concatenate([p_scr[pl.ds(g, n_sub, stride=SUBLANES), :] for g in range(N_KV)], axis=0)
    out = _dot(_silu(pre).astype(BF16), w2_ref[...])
    if not has_new:
        blk = lax.broadcasted_iota(jnp.int32, (N_KV * n_sub, 1), 0) & (n_sub - 1)
        out = jnp.where(blk == n_sub - 1, 0.0, out)
    return out


def _compress_kv(k_pages, v_pages, k_new, v_new, k_w, v_w, scr, *, has_new):
    rows_pp = (PAGE_SIZE // CMP_STRIDE) * SUBLANES
    for p0 in range(0, len(k_pages), COMPRESS_CHUNK_PAGES):
        p1 = min(p0 + COMPRESS_CHUNK_PAGES, len(k_pages))
        rs = slice(p0 * rows_pp, p1 * rows_pp)
        _compress_fill(k_pages[p0:p1], scr[0], p0)
        _compress_fill(v_pages[p0:p1], scr[3], p0)
        scr[1][rs, :] = _dot(scr[0][rs, :], k_w[0][...])
        scr[4][rs, :] = _dot(scr[3][rs, :], v_w[0][...])
    kc = _compress_finish(k_new, *k_w, scr[1], scr[2], has_new=has_new)
    vc = _compress_finish(v_new, *v_w, scr[4], scr[5], has_new=has_new)
    return kc, vc


def _compress_scratch(n_pages):
    n_rows = n_pages * (PAGE_SIZE // CMP_STRIDE) * SUBLANES
    return [pltpu.VMEM((n_rows, CMP_STRIDE * HEAD_DIM), BF16),
            pltpu.VMEM((n_rows + 2 * SUBLANES, 2 * HEAD_DIM), F32),
            pltpu.VMEM((n_rows, HEAD_DIM), F32)]


def _compress_prompt_kernel(pt_ref, *refs, n_pages):
    kp, vp = refs[:n_pages], refs[n_pages:2 * n_pages]
    wck, w2k, pek, wcv, w2v, pev, kc_ref, vc_ref, *scr = refs[2 * n_pages:]
    n_sub = n_pages * (PAGE_SIZE // CMP_STRIDE)
    assert n_sub & (n_sub - 1) == 0
    kc_ref[0], vc_ref[0] = _compress_kv(kp, vp, None, None, (wck, w2k, pek), (wcv, w2v, pev), scr, has_new=False)


def _page_specs(n_pages):
    return [pl.BlockSpec((1, PAGE_SIZE * N_KV, HEAD_DIM), lambda b, pt, p=p: (pt[b, p], 0, 0)) for p in range(n_pages)]


def _const_spec(shape):
    return pl.BlockSpec(shape, lambda b, pt: (0,) * len(shape))


def _compress_prompt(k_rows, v_rows, cw):
    bsz = k_rows.shape[0]
    t = k_rows.shape[1] // N_KV
    n_pages = t // PAGE_SIZE
    n_sub = t // CMP_STRIDE
    pt = jnp.arange(bsz * n_pages, dtype=jnp.int32).reshape(bsz, n_pages)
    kp = k_rows.reshape(bsz * n_pages, PAGE_SIZE * N_KV, HEAD_DIM)
    vp = v_rows.reshape(bsz * n_pages, PAGE_SIZE * N_KV, HEAD_DIM)
    wspecs = [_const_spec(a.shape) for a in cw]
    grid_spec = pltpu.PrefetchScalarGridSpec(
        num_scalar_prefetch=1,
        grid=(bsz,),
        in_specs=_page_specs(n_pages) * 2 + wspecs,
        out_specs=[pl.BlockSpec((1, N_KV * n_sub, HEAD_DIM), lambda b, pt: (b, 0, 0))] * 2,
        scratch_shapes=_compress_scratch(n_pages) * 2,
    )
    return pl.pallas_call(
        functools.partial(_compress_prompt_kernel, n_pages=n_pages),
        grid_spec=grid_spec,
        out_shape=[jax.ShapeDtypeStruct((bsz, N_KV * n_sub, HEAD_DIM), F32)] * 2,
        compiler_params=_params("parallel"),
        name="compress_prompt",
    )(pt, *([kp] * n_pages), *([vp] * n_pages), *cw)


def _head_rows(ref, base):
    return jnp.concatenate([ref[0, :, base + h * HEAD_DIM:base + (h + 1) * HEAD_DIM] for h in range(N_HEADS)], axis=0)


def _split_dot(a, b):
    hi = a.astype(BF16)
    lo = (a - hi.astype(F32)).astype(BF16)
    return _dot(hi, b) + _dot(lo, b)


def _compress_sample_kernel(pt_ref, ck_hbm, cv_hbm, newk, newv, wck, w2k, pek, wcv, w2v, pev, q_ref, tcs_ref, ovl_ref,
                            pow_ref, oc_ref, bits_ref, kbuf, vbuf, sem, *scr, n_pages, cur_blk):
    b = pl.program_id(0)
    slot = lax.rem(b, 2)

    def page_copies(seq, slot):
        out = []
        for p in range(n_pages):
            page = pt_ref[seq, p]
            out.append(pltpu.make_async_copy(ck_hbm.at[page], kbuf.at[slot, p], sem.at[slot, 0]))
            out.append(pltpu.make_async_copy(cv_hbm.at[page], vbuf.at[slot, p], sem.at[slot, 1]))
        return out

    @pl.when(b == 0)
    def _():
        for c in page_copies(0, 0):
            c.start()

    @pl.when(b + 1 < pl.num_programs(0))
    def _():
        for c in page_copies(b + 1, 1 - slot):
            c.start()

    for c in page_copies(b, slot):
        c.wait()
    kp = [kbuf.at[slot, p] for p in range(n_pages)]
    vp = [vbuf.at[slot, p] for p in range(n_pages)]
    kc, vc = _compress_kv(kp, vp, newk, newv, (wck, w2k, pek), (wcv, w2v, pev), scr, has_new=True)
    q16 = (_head_rows(q_ref, 0) * SCALE).astype(BF16)
    s = _dot_nt(q16, kc.astype(BF16)) + tcs_ref[...]
    e = jnp.exp(s - jnp.max(s, axis=-1, keepdims=True))
    p = e / jnp.sum(e, axis=-1, keepdims=True)
    oc_ref[0] = _dot(p.astype(BF16), vc.astype(BF16))
    imp_h = _split_dot(p, ovl_ref[...])
    imp = jnp.concatenate([jnp.sum(imp_h[GROUP_REP * g:GROUP_REP * (g + 1)], axis=0, keepdims=True)
                           for g in range(N_KV)], axis=0)
    lane = lax.broadcasted_iota(jnp.int32, (1, LANES), 1)
    forced = (lane == cur_blk) | (lane == 0)
    imp = jnp.where(forced, FORCE_SCORE, jnp.where(lane <= cur_blk, imp, -FORCE_SCORE))
    rank = jnp.zeros((N_KV, LANES), F32)
    for k in range(cur_blk + 1):
        ck = imp[:, k:k + 1]
        rank = rank + jnp.where(ck > imp, 1.0, jnp.where((ck == imp) & (lane > k), 1.0, 0.0))
    sel = jnp.where(rank < N_SEL, 1.0, 0.0) * pow_ref[...]
    lo = jnp.sum(jnp.where(lane < 16, sel, 0.0), axis=-1, keepdims=True).astype(jnp.int32)
    hi = jnp.sum(jnp.where((lane >= 16) & (lane < 32), sel, 0.0), axis=-1, keepdims=True).astype(jnp.int32)
    bits_ref[0] = jnp.broadcast_to(lo | (hi << 16), (N_KV, LANES))


def _compress_sample(cache_k, cache_v, page_table, new_k, new_v, cw, q, tcs, ovl, *, past):
    nb, n_pages = page_table.shape
    width = new_k.shape[-1]
    n_sub = n_pages * (PAGE_SIZE // CMP_STRIDE)
    pow2 = np.zeros((1, LANES), np.float32)
    pow2[0, :32] = 2.0 ** (np.arange(32) % 16)
    row_spec = pl.BlockSpec((1, 1, width), lambda b, pt: (b, 0, 0))
    grid_spec = pltpu.PrefetchScalarGridSpec(
        num_scalar_prefetch=1,
        grid=(nb,),
        in_specs=([pl.BlockSpec(memory_space=pl.ANY)] * 2 + [row_spec, row_spec] + [_const_spec(a.shape) for a in cw]
                  + [pl.BlockSpec((1, 1, q.shape[-1]), lambda b, pt: (b, 0, 0)),
                     _const_spec(tcs.shape), _const_spec(ovl.shape), _const_spec(pow2.shape)]),
        out_specs=[pl.BlockSpec((1, N_HEADS, HEAD_DIM), lambda b, pt: (b, 0, 0)),
                   pl.BlockSpec((1, N_KV, LANES), lambda b, pt: (b, 0, 0))],
        scratch_shapes=([pltpu.VMEM((2, n_pages) + cache_k.shape[1:], F32)] * 2
                        + [pltpu.SemaphoreType.DMA((2, 2))]
                        + _compress_scratch(n_pages) * 2),
    )
    return pl.pallas_call(
        functools.partial(_compress_sample_kernel, n_pages=n_pages, cur_blk=past // SLC_BLOCK),
        grid_spec=grid_spec,
        out_shape=[jax.ShapeDtypeStruct((nb, N_HEADS, HEAD_DIM), F32),
                   jax.ShapeDtypeStruct((nb, N_KV, LANES), jnp.int32)],
        compiler_params=_params("arbitrary"),
        name="compress_select_sample",
    )(page_table, cache_k, cache_v, new_k, new_v, *cw, q, tcs, ovl, jnp.asarray(pow2))


def _attn_prompt_kernel(q_ref, z0_ref, z1_ref, z2_ref, gt_ref, kc_ref, vc_ref, ks_ref, vs_ref, kw_ref, vw_ref,
                        t0_ref, t1_ref, tc_ref, ovl_ref, y_ref, m0_scr, m1_scr, acc0_scr, acc1_scr, sel_scr):
    m_scrs, acc_scrs = (m0_scr, m1_scr), (acc0_scr, acc1_scr)
    qt = pl.program_id(2)
    q = q_ref[...]
    qf = jnp.concatenate([q[:, r * HEAD_DIM:(r + 1) * HEAD_DIM] for r in range(GROUP_REP)], axis=0)
    qf = (qf * (SCALE * LOG2E)).astype(BF16)
    lane_q = lax.broadcasted_iota(jnp.int32, (1, TQ), 1)
    qpos = qt * TQ + lane_q
    key_l = lax.broadcasted_iota(jnp.int32, (TQ, 1), 0)
    gate_t = gt_ref[...].T
    z_refs = (z0_ref, z1_ref, z2_ref)

    def emit(br, r, o_t, first):
        cs = slice(r * HEAD_DIM, (r + 1) * HEAD_DIM)
        term = o_t.T * z_refs[br][:, cs]
        if first:
            y_ref[:, cs] = term
        else:
            y_ref[:, cs] += term

    blk_per_tile = TQ // SLC_BLOCK

    def compressed_and_select():
        n_cmp = kc_ref.shape[1]
        s_all = _dot_nt(kc_ref[0].astype(BF16), qf)
        vc_t = vc_ref[0].T.astype(BF16)
        cmp_end = lax.broadcasted_iota(jnp.int32, (n_cmp, 1), 0) * CMP_STRIDE + (CMP_LEN - 1)
        vis = qpos >= cmp_end
        any_vis = qpos >= CMP_LEN - 1
        row0 = pl.multiple_of(TQ // 2 - (TQ // CMP_STRIDE) * qt, CMP_STRIDE)
        psum = jnp.zeros((n_cmp, TQ), F32)
        for r in range(GROUP_REP):
            s = s_all[:, r * TQ:(r + 1) * TQ] + tc_ref[r, pl.ds(row0, n_cmp), :]
            s = jnp.where(vis, s, NEG_INF)
            e = jnp.exp2(s - jnp.max(s, axis=0, keepdims=True))
            p = jnp.where(any_vis, e / jnp.sum(e, axis=0, keepdims=True), 0.0)
            psum = psum + p
            emit(0, r, _dot(vc_t, p.astype(BF16)) * gate_t[r:r + 1, :], True)
        n_slc = ovl_ref.shape[0]
        psum_hi = psum.astype(BF16)
        imp = _dot(ovl_ref[...], psum_hi) + _dot(ovl_ref[...], (psum - psum_hi.astype(F32)).astype(BF16))
        blk = lax.broadcasted_iota(jnp.int32, (n_slc, 1), 0)
        cur = lax.shift_right_logical(qpos, int(math.log2(SLC_BLOCK)))
        forced = (blk == cur) | (blk == 0)
        imp = jnp.where(forced, FORCE_SCORE, jnp.where(blk <= cur, imp, -FORCE_SCORE))
        rank = jnp.zeros((n_slc, TQ), F32)
        for k in range(n_slc):
            rk = imp[k:k + 1, :]
            rank = rank + jnp.where(rk > imp, 1.0, jnp.where((rk == imp) & (blk > k), 1.0, 0.0))
        sel = jnp.where(rank < min(N_SEL, n_slc), 1.0, 0.0)
        for t in range(n_slc // blk_per_tile):
            sel_scr[t, 0:blk_per_tile, :] = sel[t * blk_per_tile:(t + 1) * blk_per_tile, :]

    ones_rows = jnp.ones((ACC_ROWS - HEAD_DIM, TQ), BF16)

    refs = ((ks_ref, vs_ref), (kw_ref, vw_ref))

    def attend(tiles, first):
        k_ts, v_ts, masks = [], [], []
        for st, kt, kind in tiles:
            k_ref, v_ref = refs[st]
            start = pl.multiple_of(kt * TQ, TQ)
            k_ts.append(k_ref[pl.ds(start, TQ), :])
            v_t = v_ref[pl.ds(start, TQ), :].T.astype(BF16)
            v_ts.append(jnp.concatenate([v_t, ones_rows], axis=0))
            mask = None
            if kind == "diag":
                mask = key_l <= lane_q
            elif kind == "winfar":
                mask = key_l > lane_q
            if st == 0:
                sel4 = sel_scr[kt, 0:blk_per_tile, :]
                selm = jnp.concatenate(
                    [jnp.broadcast_to(sel4[j:j + 1, :], (SLC_BLOCK, TQ)) for j in range(blk_per_tile)], axis=0) > 0.5
                mask = selm if mask is None else mask & selm
            masks.append(mask)
        s_all = _dot_nt(k_ts[0] if len(k_ts) == 1 else jnp.concatenate(k_ts, axis=0), qf)
        for r in range(GROUP_REP):
            sl = slice(r * TQ, (r + 1) * TQ)
            for st in sorted(set(t[0] for t in tiles)):
                ss, vs = [], []
                for i, ((st_i, kt, kind), mask) in enumerate(zip(tiles, masks)):
                    if st_i != st:
                        continue
                    s = s_all[i * TQ:(i + 1) * TQ, sl]
                    if kind == "diag":
                        s = s + t0_ref[r]
                    elif kind == "near":
                        s = s + t1_ref[r]
                    if mask is not None:
                        s = jnp.where(mask, s, NEG_INF)
                    ss.append(s)
                    vs.append(v_ts[i])
                mx = jnp.max(ss[0], axis=0, keepdims=True)
                for s in ss[1:]:
                    mx = jnp.maximum(mx, jnp.max(s, axis=0, keepdims=True))
                if first:
                    m_new = mx
                    upd = None
                else:
                    m_old = m_scrs[st][:, sl]
                    m_new = jnp.maximum(m_old, mx)
                    upd = jnp.exp2(m_old - m_new) * acc_scrs[st][:, sl]
                for s, v_t in zip(ss, vs):
                    pv = _dot(v_t, jnp.exp2(s - m_new).astype(BF16))
                    upd = pv if upd is None else upd + pv
                acc_scrs[st][:, sl] = upd
                m_scrs[st][:, sl] = m_new

    compressed_and_select()
    attend([(0, qt, "diag"), (1, qt, "diag")], True)

    @pl.when(qt >= 1)
    def _():
        attend([(0, qt - 1, "near"), (1, qt - 1, "near")], False)

    @pl.when(qt >= 2)
    def _():
        n_far = qt - 1

        def far_pair(i, carry):
            attend([(0, 2 * i, "far"), (0, 2 * i + 1, "far")], False)
            return carry

        lax.fori_loop(0, lax.shift_right_logical(n_far, 1), far_pair, 0)

        @pl.when((n_far & 1) == 1)
        def _():
            attend([(1, qt - 2, "winfar"), (0, n_far - 1, "far")], False)

        @pl.when((n_far & 1) == 0)
        def _():
            attend([(1, qt - 2, "winfar")], False)

    for st in range(2):
        br = st + 1
        for r in range(GROUP_REP):
            sl = slice(r * TQ, (r + 1) * TQ)
            w = gate_t[br * GROUP_REP + r:br * GROUP_REP + r + 1, :] / acc_scrs[st][HEAD_DIM:HEAD_DIM + 1, sl]
            emit(br, r, acc_scrs[st][0:HEAD_DIM, sl] * w, False)


def _attn_prompt(proj, kc, vc, k_slc, v_slc, k_win, v_win, t0, t1, tc, ovl_t, *, bsz, t):
    nq = t // TQ
    gw = GROUP_REP * HEAD_DIM
    assert TQ // 2 - (TQ // CMP_STRIDE) * (nq - 1) >= 0 and kc.shape[1] == N_KV * (t // CMP_STRIDE)
    n_cmp = t // CMP_STRIDE
    d_att = N_HEADS * HEAD_DIM
    zoff = d_att // gw
    row = lambda b, g, i: b * nq + i
    in_specs = [
        pl.BlockSpec((TQ, gw), lambda b, g, i: (row(b, g, i), g)),
    ] + [
        pl.BlockSpec((TQ, gw), lambda b, g, i, br=br: (row(b, g, i), zoff * (1 + br) + g)) for br in range(N_BRANCH)
    ] + [
        pl.BlockSpec((TQ, LANES), lambda b, g, i: (row(b, g, i), (1 + N_BRANCH) * d_att // LANES + g)),
        pl.BlockSpec((1, n_cmp, HEAD_DIM), lambda b, g, i: (b, g, 0)),
        pl.BlockSpec((1, n_cmp, HEAD_DIM), lambda b, g, i: (b, g, 0)),
    ] + [pl.BlockSpec((None, None, t, HEAD_DIM), lambda b, g, i: (b, g, 0, 0))] * 4 + [
        pl.BlockSpec((GROUP_REP, TQ, TQ), lambda b, g, i: (g, 0, 0))] * 3 + [
        pl.BlockSpec(ovl_t.shape, lambda b, g, i: (0, 0)),
    ]
    return pl.pallas_call(
        _attn_prompt_kernel,
        grid=(bsz, N_KV, nq),
        in_specs=in_specs,
        out_specs=pl.BlockSpec((TQ, gw), lambda b, g, i: (row(b, g, i), g)),
        out_shape=jax.ShapeDtypeStruct((bsz * t, d_att), F32),
        scratch_shapes=[
            pltpu.VMEM((1, GROUP_REP * TQ), F32),
            pltpu.VMEM((1, GROUP_REP * TQ), F32),
            pltpu.VMEM((ACC_ROWS, GROUP_REP * TQ), F32),
            pltpu.VMEM((ACC_ROWS, GROUP_REP * TQ), F32),
            pltpu.VMEM((t // TQ, SUBLANES, TQ), F32),
        ],
        compiler_params=_params("parallel", "parallel", "arbitrary"),
        name="nsa_attention_prompt",
    )(proj, proj, proj, proj, proj, kc, vc, k_slc, v_slc, k_win, v_win, t0, t1, tc, ovl_t)


def _attn_sample_kernel(bits_ref, pt_ref, *refs, n_pages):
    kp, vp = refs[:n_pages], refs[n_pages:2 * n_pages]
    (kwin_ref, vwin_ref, nks_ref, nvs_ref, nkw_ref, nvw_ref, q_ref, oc_ref, ts_ref, tw_ref,
     y_ref, kwo_ref, vwo_ref) = refs[2 * n_pages:]
    b = pl.program_id(0)
    rows_pp = PAGE_SIZE * N_KV
    blk_rows = SLC_BLOCK * N_KV
    d_att = N_HEADS * HEAD_DIM
    n_win_rows = kwin_ref.shape[1]
    head_rows = functools.partial(_head_rows, q_ref)

    for src, new, dst in ((kwin_ref, nkw_ref, kwo_ref), (vwin_ref, nvw_ref, vwo_ref)):
        dst[0, 0:n_win_rows - N_KV, :] = src[0, N_KV:n_win_rows, :]
        dst[0, n_win_rows - N_KV:n_win_rows, :] = jnp.concatenate(
            [new[0, :, g * HEAD_DIM:(g + 1) * HEAD_DIM] for g in range(N_KV)], axis=0)

    def group_rows(ref):
        return jnp.concatenate([ref[0, :, (h // GROUP_REP) * HEAD_DIM:(h // GROUP_REP + 1) * HEAD_DIM]
                                for h in range(N_HEADS)], axis=0)

    def softmax_pv(s_tiles, v_tiles, s_new, v_new):
        m = s_new
        for s in s_tiles:
            m = jnp.maximum(m, jnp.max(s, axis=-1, keepdims=True))
        p_new = jnp.exp(s_new - m)
        l = p_new
        acc = p_new * v_new
        for s, v in zip(s_tiles, v_tiles):
            p = jnp.exp(s - m)
            l = l + jnp.sum(p, axis=-1, keepdims=True)
            acc = acc + _dot(p.astype(BF16), v.astype(BF16))
        return acc / l

    q32 = head_rows(0) * SCALE
    q16 = q32.astype(BF16)
    head = lax.broadcasted_iota(jnp.int32, (N_HEADS, 1), 0)
    bits = jnp.zeros((N_HEADS, 1), jnp.int32)
    for g in range(N_KV):
        bits = jnp.where((head >= g * GROUP_REP) & (head < (g + 1) * GROUP_REP), bits_ref[b, g], bits)
    lane_blk = lax.broadcasted_iota(jnp.int32, (1, rows_pp), 1) // blk_rows
    s_tiles, v_tiles = [], []
    for p in range(n_pages):
        s = _dot_nt(q16, kp[p][0].astype(BF16)) + ts_ref[:, p * rows_pp:(p + 1) * rows_pp]
        shift = jnp.broadcast_to(lane_blk + p * (rows_pp // blk_rows), s.shape)
        sel = lax.shift_right_logical(jnp.broadcast_to(bits, s.shape), shift) & 1
        s_tiles.append(jnp.where(sel == 1, s, NEG_INF))
        v_tiles.append(vp[p][0])
    tail = n_pages * rows_pp
    s_new = jnp.sum(q32 * group_rows(nks_ref), axis=-1, keepdims=True) + ts_ref[:, tail:tail + 1]
    o_slc = softmax_pv(s_tiles, v_tiles, s_new, group_rows(nvs_ref))
    s_tiles, v_tiles = [], []
    for p in range(n_win_rows // rows_pp):
        rows = slice(p * rows_pp, (p + 1) * rows_pp)
        s_tiles.append(_dot_nt(q16, kwin_ref[0, rows, :].astype(BF16)) + tw_ref[:, rows])
        v_tiles.append(vwin_ref[0, rows, :])
    s_new = jnp.sum(q32 * group_rows(nkw_ref), axis=-1, keepdims=True) + tw_ref[:, n_win_rows:n_win_rows + 1]
    o_win = softmax_pv(s_tiles, v_tiles, s_new, group_rows(nvw_ref))
    y = jnp.zeros((N_HEADS, HEAD_DIM), F32)
    gate0 = (1 + N_BRANCH) * d_att
    for br, o in enumerate((oc_ref[0], o_slc, o_win)):
        cols = [gate0 + (h // GROUP_REP) * LANES + br * GROUP_REP + h % GROUP_REP for h in range(N_HEADS)]
        gate = jnp.concatenate([q_ref[0, :, c:c + 1] for c in cols], axis=0)
        y = y + gate * o * head_rows((1 + br) * d_att)
    y_ref[0] = y


def _attn_sample(cache_k, cache_v, page_table, bits, k_win, v_win, new_rows, proj, o_cmp, ts, tw):
    nb, n_pages = page_table.shape
    rows_pp = PAGE_SIZE * N_KV
    win_spec = pl.BlockSpec((1, k_win.shape[1], HEAD_DIM), lambda b, bits, pt: (b, 0, 0))
    row_spec = lambda w: pl.BlockSpec((1, 1, w), lambda b, bits, pt: (b, 0, 0))
    const = lambda a: pl.BlockSpec(a.shape, lambda b, bits, pt: (0,) * a.ndim)
    page_specs = [pl.BlockSpec((1, rows_pp, HEAD_DIM), lambda b, bits, pt, p=p: (pt[b, p], 0, 0)) for p in range(n_pages)]
    grid_spec = pltpu.PrefetchScalarGridSpec(
        num_scalar_prefetch=2,
        grid=(nb,),
        in_specs=(page_specs * 2 + [win_spec] * 2
                  + [row_spec(N_KV * HEAD_DIM)] * 4
                  + [row_spec(proj.shape[-1]),
                     pl.BlockSpec((1, N_HEADS, HEAD_DIM), lambda b, bits, pt: (b, 0, 0)),
                     const(ts), const(tw)]),
        out_specs=[pl.BlockSpec((1, N_HEADS, HEAD_DIM), lambda b, bits, pt: (b, 0, 0)), win_spec, win_spec],
    )
    return pl.pallas_call(
        functools.partial(_attn_sample_kernel, n_pages=n_pages),
        grid_spec=grid_spec,
        out_shape=[jax.ShapeDtypeStruct((nb, N_HEADS, HEAD_DIM), F32),
                   jax.ShapeDtypeStruct(k_win.shape, F32), jax.ShapeDtypeStruct(v_win.shape, F32)],
        compiler_params=_params("parallel"),
        name="nsa_attention_sample",
    )(bits, page_table, *([cache_k] * n_pages), *([cache_v] * n_pages), k_win, v_win, *new_rows,
      proj, o_cmp, ts, tw)


def _overlap(n_cmp, n_slc):
    cs = np.arange(n_cmp) * CMP_STRIDE
    bs = np.arange(n_slc) * SLC_BLOCK
    ov = np.clip(np.minimum(cs[:, None] + CMP_LEN, bs[None, :] + SLC_BLOCK) - np.maximum(cs[:, None], bs[None, :]), 0, None)
    return (ov / CMP_LEN).astype(np.float32)


def _skip_unselected_pages(page_table, bits):
    nb, n_pages = page_table.shape
    blk_per_page = PAGE_SIZE // SLC_BLOCK
    assert n_pages * blk_per_page <= 32
    any_bits = functools.reduce(jnp.bitwise_or, [bits[:, g] for g in range(bits.shape[1])])
    shifts = blk_per_page * jnp.arange(n_pages, dtype=jnp.int32)
    needed = ((any_bits[:, None] >> shifts[None, :]) & ((1 << blk_per_page) - 1)) != 0
    seq = jnp.arange(nb, dtype=jnp.int32)[:, None]
    last_needed = lax.cummax(jnp.where(needed | (seq == 0), seq, 0), axis=0)
    return jnp.take_along_axis(page_table, last_needed, axis=0)


def _compress_weights(w1, w2, pe):
    half = CMP_STRIDE * HEAD_DIM
    wcat = jnp.concatenate([w1[:half], w1[half:]], axis=1).astype(BF16)
    pe8 = jnp.pad(pe.reshape(2, half), ((0, SUBLANES - 2), (0, 0)))
    return wcat, w2.astype(BF16), pe8


def kernel(x_prompt, x_sample, state_pool, cache_k_cmp, cache_v_cmp, cache_k_slc, cache_v_slc, state_k_win, state_v_win, page_table, c_prompt, c_sample, g_norm, w_ada, b_ada, w_in_a, w_grp, pool_scale, w_out_a, g_kv, w_kv, pe_k, w_ck1, w_ck2, pe_v, w_cv1, w_cv2, rel_bias, w_in_b, w_out_b, g_final):
    bsz, t, d = x_prompt.shape
    nb = x_sample.shape[0]
    n_pages = page_table.shape[1]
    past = n_pages * PAGE_SIZE
    d_att = N_HEADS * HEAD_DIM
    d_kv = N_KV * HEAD_DIM
    n_a = w_in_a.shape[0]
    assert n_a == 1 and w_in_b.shape[0] == 1 and x_sample.shape[1] == 1
    assert t % TQ == 0 and t >= WINDOW and past % PAGE_SIZE == 0 and state_k_win.shape[1] == WINDOW

    w_in_a16 = w_in_a[0].astype(BF16)
    w_grp16 = w_grp[0].astype(BF16)
    w_out_a16 = w_out_a[0].astype(BF16)
    w_kv16 = w_kv.astype(BF16)
    w_out_b16 = w_out_b[0].astype(BF16)
    n_qz = (1 + N_BRANCH) * d_att
    wg = w_in_b[0][:, n_qz:].reshape(d, N_BRANCH, N_KV, GROUP_REP).transpose(0, 2, 1, 3)
    wg = jnp.pad(wg.reshape(d, N_KV, N_BRANCH * GROUP_REP), ((0, 0), (0, 0), (0, LANES - N_BRANCH * GROUP_REP)))
    n_proj = n_qz + TN_PROJ
    w_gate16 = jnp.pad(wg.reshape(d, N_KV * LANES), ((0, 0), (0, TN_PROJ - N_KV * LANES))).astype(BF16)
    nsa_proj = functools.partial(_nm_matmul, w=w_in_b[0].astype(BF16), w_tail=w_gate16, n_cols=n_proj, tn=TN_PROJ,
                                 silu_cols=d_att, sigmoid_cols=n_qz)
    cw_k = _compress_weights(w_ck1, w_ck2, pe_k)
    cw_v = _compress_weights(w_cv1, w_cv2, pe_v)
    cw = cw_k + cw_v

    mod = _ada(jnp.concatenate([c_prompt, c_sample], axis=0), w_ada, b_ada)

    def modulation(l, lo, hi, per_row):
        parts = [mod[l, lo:hi, k * d:(k + 1) * d] for k in range(3)]
        return [p[None] if per_row else p[:, None] for p in parts]

    t0, t1, tc, ts, tw, tcs = _bias_tables(rel_bias, past=past, n_cmp_s=past // CMP_STRIDE)
    ts, tw, tcs = ts.reshape(N_HEADS, -1), tw.reshape(N_HEADS, -1), tcs.reshape(N_HEADS, -1)

    shift, scale, gate = modulation(0, 0, bsz, False)
    uz = _nm_matmul(x_prompt, g_norm[0], scale, shift, w_in_a16, n_cols=2 * d, tm=TM_PROJ, tn=TN_PROJ, name="in_proj_pool_prompt")
    pool_p = uz[:, t - POOL_BUF:, :d][None]
    x1 = _pool_prompt(uz, w_grp16, pool_scale[0], w_out_a16, x_prompt, gate, tm=256)
    kv_p = _kv_proj(x1, g_kv, w_kv16, tm=256, name="kv_proj_prompt")
    heads = lambda a: a.reshape(a.shape[0], -1, N_KV, HEAD_DIM)
    kv_state_p = [heads(a) for a in kv_p[:4]] + [heads(a[:, (t - WINDOW) * N_KV:]) for a in kv_p[4:6]]
    kc_p, vc_p = _compress_prompt(kv_p[0], kv_p[1], cw)
    shift, scale, gate = modulation(1, 0, bsz, False)
    proj = nsa_proj(x1, g_norm[1], scale, shift, tm=TM_PROJ, name="in_proj_nsa_prompt")
    ovl_t = jnp.asarray(np.pad(_overlap(t // CMP_STRIDE - 1, t // SLC_BLOCK), ((0, 1), (0, 0))).T).astype(BF16)
    flat = lambda a: a.reshape(bsz * t, a.shape[-1])
    y1 = _attn_prompt(flat(proj), kc_p, vc_p, kv_p[6], kv_p[7], kv_p[8], kv_p[9],
                      t0, t1, tc, ovl_t, bsz=bsz, t=t)
    y_prompt = _proj_residual(y1.reshape(bsz, t, d_att), w_out_b16, x1, gate, g_final, tm=512, final_norm=True,
                              name="out_proj_nsa_prompt")

    xs = x_sample.reshape(1, nb, d)
    shift, scale, gate = modulation(0, bsz, bsz + nb, True)
    uz_s = _nm_matmul(xs, g_norm[0], scale, shift, w_in_a16, n_cols=2 * d, tm=nb, tn=512, name="in_proj_pool_sample")
    pool_s = jnp.concatenate([state_pool[:, :, 1:], uz_s[0, :, None, :d][None]], axis=2)
    y0_s = _pool_sample(uz_s, state_pool[0], w_grp16, pool_scale[0], qpos=past, tb=min(32, nb))
    x1_s = _proj_residual(y0_s, w_out_a16, xs, gate, g_final, tm=nb, final_norm=False, name="out_proj_pool_sample")
    kv_s = _kv_proj(x1_s, g_kv, w_kv16, tm=nb, name="kv_proj_sample")
    new_rows = [a.reshape(nb, 1, d_kv) for a in kv_s[:2 * N_BRANCH]]
    shift, scale, gate = modulation(1, bsz, bsz + nb, True)
    proj_s = nsa_proj(x1_s, g_norm[1], scale, shift, tm=nb, name="in_proj_nsa_sample").reshape(nb, 1, n_proj)
    n_cmp_s = past // CMP_STRIDE
    n_slc_s = past // SLC_BLOCK + 1
    ovl_s = jnp.asarray(np.tile(np.pad(_overlap(n_cmp_s, n_slc_s), ((0, 0), (0, LANES - n_slc_s))), (N_KV, 1))).astype(BF16)
    rows_of = lambda c: c.reshape(c.shape[0], PAGE_SIZE * N_KV, HEAD_DIM)
    o_cmp, bits = _compress_sample(rows_of(cache_k_cmp), rows_of(cache_v_cmp), page_table, new_rows[0], new_rows[1], cw,
                                   proj_s, tcs, ovl_s, past=past)
    bits = bits[:, :, 0]
    y1_s, k_win_s, v_win_s = _attn_sample(
        rows_of(cache_k_slc), rows_of(cache_v_slc), _skip_unselected_pages(page_table, bits), bits,
        state_k_win.reshape(nb, WINDOW * N_KV, HEAD_DIM), state_v_win.reshape(nb, WINDOW * N_KV, HEAD_DIM),
        new_rows[2:], proj_s, o_cmp, ts, tw)
    y_sample = _proj_residual(y1_s.reshape(1, nb, d_att), w_out_b16, x1_s, gate, g_final, tm=nb, final_norm=True,
                              name="out_proj_nsa_sample").reshape(nb, 1, d)
    new4 = [a.reshape(nb, 1, N_KV, HEAD_DIM) for a in kv_s[:2 * N_BRANCH]]
    kv_state_s = new4[:4] + [k_win_s.reshape(state_k_win.shape), v_win_s.reshape(state_v_win.shape)]

    return (y_prompt, y_sample, pool_p, *kv_state_p, pool_s, *kv_state_s)
```

```python
import functools
import math

import numpy as np
import jax
import jax.numpy as jnp
from jax import lax
from jax.experimental import pallas as pl
from jax.experimental.pallas import tpu as pltpu

F32 = jnp.float32
BF16 = jnp.bfloat16

PAGE_SIZE = 128
POOL_WINDOWS = (2, 4, 8, 16)
POOL_BUF = max(POOL_WINDOWS) - 1
HEAD_DIM = 128
N_KV = 4
GROUP_REP = 4
N_HEADS = N_KV * GROUP_REP
N_BRANCH = 3
CMP_LEN = 32
CMP_STRIDE = 16
SLC_BLOCK = 64
N_SEL = 8
WINDOW = 512
N_BUCKETS = 32
MAX_DISTANCE = 128
RMS_EPS = 1e-6
SCALE = HEAD_DIM ** -0.5
LOG2E = math.log2(math.e)
NEG_INF = -1e30
FORCE_SCORE = 1e9

LANES = 128
SUBLANES = 8
VMEM_LIMIT_BYTES = 56 * 1024 * 1024
TQ = 256
COMPRESS_CHUNK_PAGES = 4
ACC_ROWS = HEAD_DIM + 16
TN_PROJ = 1024
TM_PROJ = 1024
HALO = 16


def _bucket_starts():
    max_exact = N_BUCKETS // 2
    d = np.arange(0, MAX_DISTANCE + 1)
    large = max_exact + np.floor(
        np.log(np.maximum(d, max_exact) / max_exact) / math.log(MAX_DISTANCE / max_exact) * (N_BUCKETS - max_exact)
    ).astype(np.int64)
    bucket = np.where(d < max_exact, d, np.minimum(large, N_BUCKETS - 1))
    return [int(np.argmax(bucket >= k)) for k in range(N_BUCKETS)]


BUCKET_STARTS = _bucket_starts()
FAR_DIST = BUCKET_STARTS[-1]


def _params(*sem):
    return pltpu.CompilerParams(dimension_semantics=sem, vmem_limit_bytes=VMEM_LIMIT_BYTES)


def _silu(x):
    return x * jax.nn.sigmoid(x)


def _rms(x, g):
    return x * lax.rsqrt(jnp.mean(x * x, axis=-1, keepdims=True) + RMS_EPS) * g


def _dot(a, b):
    return jnp.dot(a, b, preferred_element_type=F32)


def _dot_nt(a, b):
    return lax.dot_general(a, b, (((1,), (1,)), ((), ())), preferred_element_type=F32)


def _ada_kernel(c_ref, w_ref, b_ref, o_ref):
    a = _silu(c_ref[...]).astype(BF16)
    o_ref[0] = _dot(a, w_ref[0].astype(BF16)) + b_ref[0]


def _ada(c_all, w_ada, b_ada):
    depth, d, n = w_ada.shape
    m = c_all.shape[0]
    tn = 512
    return pl.pallas_call(
        _ada_kernel,
        grid=(depth, n // tn),
        in_specs=[
            pl.BlockSpec((m, d), lambda l, j: (0, 0)),
            pl.BlockSpec((1, d, tn), lambda l, j: (l, 0, j)),
            pl.BlockSpec((1, 1, tn), lambda l, j: (l, 0, j)),
        ],
        out_specs=pl.BlockSpec((1, m, tn), lambda l, j: (l, 0, j)),
        out_shape=jax.ShapeDtypeStruct((depth, m, n), F32),
        compiler_params=_params("parallel", "parallel"),
        name="ada_modulation",
    )(c_all, w_ada, b_ada.reshape(depth, 1, n))


def _nm_matmul_kernel(x_ref, g_ref, sc_ref, sh_ref, w_ref, *rest, silu_from, sigmoid_from):
    if len(rest) == 2:
        tail, (o_ref, h_scr) = None, rest
    else:
        tail, o_ref, ot_ref, h_scr = rest
    j = pl.program_id(2)

    @pl.when(j == 0)
    def _():
        y = _rms(x_ref[0], g_ref[...])
        h_scr[...] = (y * (1.0 + sc_ref[0]) + sh_ref[0]).astype(BF16)

    def column_tile():
        r = _dot(h_scr[...], w_ref[...])
        if silu_from is None:
            o_ref[0] = r
        else:
            sig = jax.nn.sigmoid(r)
            o_ref[0] = jnp.where(j >= sigmoid_from, sig, jnp.where(j >= silu_from, r * sig, r))

    def tail_tile():
        ot_ref[0] = jax.nn.sigmoid(_dot(h_scr[...], tail[...]))

    if tail is None:
        column_tile()
    else:
        last = pl.num_programs(2) - 1
        pl.when(j < last)(column_tile)
        pl.when(j == last)(tail_tile)


def _mod_spec(arr, tm):
    d = arr.shape[-1]
    if arr.shape[1] == 1:
        return pl.BlockSpec((1, 1, d), lambda b, i, *_: (b, 0, 0))
    return pl.BlockSpec((1, tm, d), lambda b, i, *_: (b, i, 0))


def _nm_matmul(x, g, scale, shift, w, *, n_cols, tm, tn, silu_cols=None, sigmoid_cols=None, w_tail=None, name):
    bsz, rows, d = x.shape
    n_j = n_cols // tn
    out_specs = pl.BlockSpec((1, tm, tn), lambda b, i, j: (b, i, j))
    out_shape = jax.ShapeDtypeStruct((bsz, rows, n_cols), F32)
    if w_tail is None:
        w_specs, w_args = [pl.BlockSpec((d, tn), lambda b, i, j: (0, j))], [w]
    else:
        tw = w_tail.shape[1]
        last_main = n_j - 1
        w_specs = [pl.BlockSpec((d, tn), lambda b, i, j: (0, jnp.minimum(j, last_main))),
                   pl.BlockSpec((d, tw), lambda b, i, j: (0, 0))]
        w_args = [w, w_tail]
        out_specs = [pl.BlockSpec((1, tm, tn), lambda b, i, j: (b, i, jnp.minimum(j, last_main))),
                     pl.BlockSpec((1, tm, tw), lambda b, i, j: (b, i, 0))]
        out_shape = [out_shape, jax.ShapeDtypeStruct((bsz, rows, tw), F32)]
        n_j += 1
    if silu_cols is None and sigmoid_cols is None:
        silu_from = sigmoid_from = None
    else:
        sigmoid_from = n_cols // tn if sigmoid_cols is None else sigmoid_cols // tn
        silu_from = sigmoid_from if silu_cols is None else silu_cols // tn
    return pl.pallas_call(
        functools.partial(_nm_matmul_kernel, silu_from=silu_from, sigmoid_from=sigmoid_from),
        grid=(bsz, rows // tm, n_j),
        in_specs=[
            pl.BlockSpec((1, tm, d), lambda b, i, j: (b, i, 0)),
            pl.BlockSpec((1, d), lambda b, i, j: (0, 0)),
            _mod_spec(scale, tm),
            _mod_spec(shift, tm),
        ] + w_specs,
        out_specs=out_specs,
        out_shape=out_shape,
        scratch_shapes=[pltpu.VMEM((tm, d), BF16)],
        compiler_params=_params("parallel", "parallel", "arbitrary"),
        name=name,
    )(x, g.reshape(1, d), scale, shift, *w_args)


def _pool_mix(pooled_fn, z, wg_ref, ps_ref, store):
    grp = z.shape[-1] // len(POOL_WINDOWS)
    for gi, w in enumerate(POOL_WINDOWS):
        cs = slice(gi * grp, (gi + 1) * grp)
        mixed = _dot(pooled_fn(gi, w, cs).astype(BF16), wg_ref[gi])
        store(cs, mixed * ps_ref[:, cs] * _silu(z[:, cs]))


def _pool_prompt_kernel(u_ref, z_ref, halo_ref, wg_ref, ps_ref, wo_ref, x_ref, gate_ref, o_ref,
                        ue_scr, a_scr, b_scr, y_scr):
    i = pl.program_id(1)
    tm = u_ref.shape[1]
    top = SUBLANES + HALO
    n = top + tm
    u = u_ref[0]
    ue_scr[0:SUBLANES, :] = jnp.zeros((SUBLANES, u.shape[1]), F32)
    ue_scr[SUBLANES:top, :] = jnp.where(i > 0, halo_ref[0], 0.0)
    ue_scr[top:n, :] = u
    zero8 = jnp.zeros((SUBLANES, a_scr.shape[1]), F32)
    a_scr[0:SUBLANES, :] = zero8
    b_scr[0:SUBLANES, :] = zero8
    pos = i * tm + lax.broadcasted_iota(jnp.int32, (tm, 1), 0)

    def pooled(gi, w, cs):
        src = lambda lo, hi: ue_scr[lo:hi, cs]
        k = 1
        for dst in (a_scr, b_scr, a_scr):
            if 2 * k >= w:
                break
            dst[SUBLANES:n, :] = src(SUBLANES, n) + src(SUBLANES - k, n - k)
            src = lambda lo, hi, dst=dst: dst[lo:hi, :]
            k *= 2
        acc = src(top, n) + src(top - k, n - k)
        cnt = jnp.minimum(pos + 1, w).astype(F32)
        return acc / cnt - u[:, cs]

    def store(cs, val):
        y_scr[:, cs] = val.astype(BF16)

    _pool_mix(pooled, z_ref[0], wg_ref, ps_ref, store)
    o_ref[0] = x_ref[0] + gate_ref[0] * _dot(y_scr[...], wo_ref[...])


def _pool_prompt(uz, w_grp, pool_scale, w_out, x, gate, *, tm):
    bsz, t, d2 = uz.shape
    d = d2 // 2
    ng, grp, _ = w_grp.shape
    return pl.pallas_call(
        _pool_prompt_kernel,
        grid=(bsz, t // tm),
        in_specs=[
            pl.BlockSpec((1, tm, d), lambda b, i: (b, i, 0)),
            pl.BlockSpec((1, tm, d), lambda b, i: (b, i, 1)),
            pl.BlockSpec((1, HALO, d), lambda b, i: (b, jnp.maximum(i * (tm // HALO) - 1, 0), 0)),
            pl.BlockSpec((ng, grp, grp), lambda b, i: (0, 0, 0)),
            pl.BlockSpec((1, d), lambda b, i: (0, 0)),
            pl.BlockSpec(w_out.shape, lambda b, i: (0, 0)),
            pl.BlockSpec((1, tm, d), lambda b, i: (b, i, 0)),
            _mod_spec(gate, tm),
        ],
        out_specs=pl.BlockSpec((1, tm, d), lambda b, i: (b, i, 0)),
        out_shape=jax.ShapeDtypeStruct((bsz, t, d), F32),
        scratch_shapes=[pltpu.VMEM((SUBLANES + HALO + tm, d), F32),
                        pltpu.VMEM((SUBLANES + HALO + tm, grp), F32),
                        pltpu.VMEM((SUBLANES + HALO + tm, grp), F32),
                        pltpu.VMEM((tm, d), BF16)],
        compiler_params=_params("parallel", "parallel"),
        name="pool_mixer_prompt",
    )(uz, uz, uz, w_grp, pool_scale.reshape(1, d), w_out, x, gate)


def _pool_sample_kernel(u_ref, z_ref, pre_ref, wg_ref, ps_ref, y_ref, *, qpos):
    u = u_ref[0]

    def pooled(gi, w, cs):
        acc = u[:, cs]
        for k in range(1, w):
            acc = acc + pre_ref[:, POOL_BUF - k, cs]
        return acc / float(min(qpos + 1, w)) - u[:, cs]

    def store(cs, val):
        y_ref[0, :, cs] = val

    _pool_mix(pooled, z_ref[0], wg_ref, ps_ref, store)


def _pool_sample(uz, prefix, w_grp, pool_scale, *, qpos, tb):
    _, nb, d2 = uz.shape
    d = d2 // 2
    ng, grp, _ = w_grp.shape
    return pl.pallas_call(
        functools.partial(_pool_sample_kernel, qpos=qpos),
        grid=(nb // tb,),
        in_specs=[
            pl.BlockSpec((1, tb, d), lambda i: (0, i, 0)),
            pl.BlockSpec((1, tb, d), lambda i: (0, i, 1)),
            pl.BlockSpec((tb, POOL_BUF, d), lambda i: (i, 0, 0)),
            pl.BlockSpec((ng, grp, grp), lambda i: (0, 0, 0)),
            pl.BlockSpec((1, d), lambda i: (0, 0)),
        ],
        out_specs=pl.BlockSpec((1, tb, d), lambda i: (0, i, 0)),
        out_shape=jax.ShapeDtypeStruct((1, nb, d), F32),
        compiler_params=_params("parallel"),
        name="pool_mixer_sample",
    )(uz, uz, prefix, w_grp, pool_scale.reshape(1, d))


def _proj_residual_kernel(y_ref, w_ref, x_ref, gate_ref, g_ref, o_ref, *, final_norm):
    xo = x_ref[0] + gate_ref[0] * _dot(y_ref[0].astype(BF16), w_ref[...])
    o_ref[0] = _rms(xo, g_ref[...]) if final_norm else xo


def _proj_residual(y, w, x, gate, g_final, *, tm, final_norm, name):
    bsz, rows, d = x.shape
    dy = y.shape[-1]
    return pl.pallas_call(
        functools.partial(_proj_residual_kernel, final_norm=final_norm),
        grid=(bsz, rows // tm),
        in_specs=[
            pl.BlockSpec((1, tm, dy), lambda b, i: (b, i, 0)),
            pl.BlockSpec((dy, d), lambda b, i: (0, 0)),
            pl.BlockSpec((1, tm, d), lambda b, i: (b, i, 0)),
            _mod_spec(gate, tm),
            pl.BlockSpec((1, d), lambda b, i: (0, 0)),
        ],
        out_specs=pl.BlockSpec((1, tm, d), lambda b, i: (b, i, 0)),
        out_shape=jax.ShapeDtypeStruct((bsz, rows, d), F32),
        compiler_params=_params("parallel", "parallel"),
        name=name,
    )(y, w, x, gate, g_final.reshape(1, d))


KV_GROUP_COPIES = ((2, BF16), (3, F32), (4, BF16), (5, F32))


def _kv_kernel(x_ref, g_ref, w_ref, *o_refs):
    h = _rms(x_ref[0], g_ref[...]).astype(BF16)
    n_out = 2 * N_BRANCH
    tm = x_ref.shape[1]
    n = N_KV * HEAD_DIM
    copies = dict((o, (o_refs[n_out + i], dt)) for i, (o, dt) in enumerate(KV_GROUP_COPIES))
    for o in range(n_out):
        r = _dot(h, w_ref[:, o * n:(o + 1) * n])
        for g in range(N_KV):
            rg = r[:, g * HEAD_DIM:(g + 1) * HEAD_DIM]
            o_refs[o][0, pl.ds(g, tm, stride=N_KV), :] = rg
            if o in copies:
                c_ref, dt = copies[o]
                c_ref[0, g] = rg.astype(dt)


def _kv_proj(x, g_kv, w_kv, *, tm, name):
    bsz, rows, d = x.shape
    n_out = 2 * N_BRANCH
    return pl.pallas_call(
        _kv_kernel,
        grid=(bsz, rows // tm),
        in_specs=[
            pl.BlockSpec((1, tm, d), lambda b, i: (b, i, 0)),
            pl.BlockSpec((1, d), lambda b, i: (0, 0)),
            pl.BlockSpec(w_kv.shape, lambda b, i: (0, 0)),
        ],
        out_specs=([pl.BlockSpec((1, tm * N_KV, HEAD_DIM), lambda b, i: (b, i, 0))] * n_out
                   + [pl.BlockSpec((1, N_KV, tm, HEAD_DIM), lambda b, i: (b, 0, i, 0))] * len(KV_GROUP_COPIES)),
        out_shape=([jax.ShapeDtypeStruct((bsz, rows * N_KV, HEAD_DIM), F32)] * n_out
                   + [jax.ShapeDtypeStruct((bsz, N_KV, rows, HEAD_DIM), dt) for _, dt in KV_GROUP_COPIES]),
        compiler_params=_params("parallel", "parallel"),
        name=name,
    )(x, g_kv.reshape(1, d), w_kv)


def _bias_lookup(rb_ref, h, dist):
    acc = jnp.full(dist.shape, rb_ref[N_BUCKETS - 1, h], F32)
    for k in range(N_BUCKETS - 2, -1, -1):
        acc = jnp.where(dist < BUCKET_STARTS[k + 1], rb_ref[k, h], acc)
    return acc


def _bias_tables_kernel(rb_ref, t0_ref, t1_ref, tc_ref, ts_ref, tw_ref, tcs_ref, *, past):
    h = pl.program_id(0)
    key = lax.broadcasted_iota(jnp.int32, (TQ, TQ), 0)
    qry = lax.broadcasted_iota(jnp.int32, (TQ, TQ), 1)
    far = rb_ref[N_BUCKETS - 1, h]
    rel2 = lambda dist: (_bias_lookup(rb_ref, h, dist) - far) * LOG2E
    t0_ref[0] = rel2(qry - key)
    t1_ref[0] = rel2(TQ + qry - key)
    tc_ref[0] = rel2(qry - CMP_STRIDE * (key - TQ // 2) - (CMP_LEN - 1))
    ls = lax.broadcasted_iota(jnp.int32, ts_ref.shape[1:], 1)
    own = (ls % N_KV) == h // GROUP_REP
    tok = ls // N_KV
    ts_ref[0] = jnp.where(ls == N_KV * past, rb_ref[0, h],
                          jnp.where(own & (tok < past), _bias_lookup(rb_ref, h, past - tok), NEG_INF))
    lw = lax.broadcasted_iota(jnp.int32, tw_ref.shape[1:], 1)
    own = (lw % N_KV) == h // GROUP_REP
    dw = WINDOW - lw // N_KV
    tw_ref[0] = jnp.where(lw == N_KV * WINDOW, rb_ref[0, h],
                          jnp.where(own & (dw > 0) & (dw < WINDOW) & (past - dw >= 0), _bias_lookup(rb_ref, h, dw), NEG_INF))
    lc = lax.broadcasted_iota(jnp.int32, tcs_ref.shape[1:], 1)
    n_cmp = tcs_ref.shape[2] // N_KV
    dc = past - CMP_STRIDE * (lc % n_cmp) - (CMP_LEN - 1)
    tcs_ref[0] = jnp.where((lc // n_cmp == h // GROUP_REP) & (dc >= 0), _bias_lookup(rb_ref, h, dc), NEG_INF)


def _bias_tables(rel_bias, *, past, n_cmp_s):
    nh = rel_bias.shape[1]
    ls = N_KV * past + LANES
    lw = N_KV * WINDOW + LANES
    shapes = [(nh, TQ, TQ)] * 3 + [(nh, 1, ls), (nh, 1, lw), (nh, 1, N_KV * n_cmp_s)]
    return pl.pallas_call(
        functools.partial(_bias_tables_kernel, past=past),
        grid=(nh,),
        in_specs=[pl.BlockSpec(memory_space=pltpu.SMEM)],
        out_specs=[pl.BlockSpec((1,) + s[1:], lambda h: (h, 0, 0)) for s in shapes],
        out_shape=[jax.ShapeDtypeStruct(s, F32) for s in shapes],
        compiler_params=_params("parallel"),
        name="rel_bias_tables",
    )(rel_bias)


def _compress_fill(page_refs, lhs_scr, first_page):
    sub_per_page = PAGE_SIZE // CMP_STRIDE
    pairs = CMP_STRIDE // 2
    rows_pp = sub_per_page * SUBLANES
    even = (lax.broadcasted_iota(jnp.int32, (rows_pp, 1), 0) & (SUBLANES - 1)) < N_KV
    for p, pref in enumerate(page_refs, first_page):
        x4 = pref[...].reshape(sub_per_page, pairs, SUBLANES, HEAD_DIM)
        rs = slice(p * rows_pp, (p + 1) * rows_pp)
        for j in range(pairs):
            t = x4[:, j].reshape(rows_pp, HEAD_DIM)
            lhs_scr[rs, 2 * j * HEAD_DIM:(2 * j + 1) * HEAD_DIM] = jnp.where(even, t, 0.0).astype(BF16)
            lhs_scr[rs, (2 * j + 1) * HEAD_DIM:(2 * j + 2) * HEAD_DIM] = jnp.where(even, 0.0, t).astype(BF16)


def _compress_finish(new_ref, wcat_ref, w2_ref, pe_ref, t_scr, p_scr, *, has_new):
    n_rows = p_scr.shape[0]
    n_sub = n_rows // SUBLANES
    wcat = wcat_ref[...]
    t_scr[n_rows:, :] = jnp.zeros((2 * SUBLANES, 2 * HEAD_DIM), F32)
    t_scr[0:n_rows, :] = t_scr[0:n_rows, :] + t_scr[N_KV:n_rows + N_KV, :]
    if has_new:
        new8 = jnp.concatenate([new_ref[0, :, g * HEAD_DIM:(g + 1) * HEAD_DIM] for g in range(N_KV)]
                               + [jnp.zeros((SUBLANES - N_KV, HEAD_DIM), F32)], axis=0)
        t_scr[n_rows:n_rows + SUBLANES, HEAD_DIM:] = _dot(new8.astype(BF16), wcat[0:HEAD_DIM, HEAD_DIM:])
    pe_r = _dot(pe_ref[...].astype(BF16), wcat)
    pe_const = pe_r[0:1, :HEAD_DIM] + pe_r[1:2, HEAD_DIM:]
    p_scr[...] = t_scr[0:n_rows, :HEAD_DIM] + t_scr[SUBLANES:n_rows + SUBLANES, HEAD_DIM:] + pe_const
    pre = jnp.concatenate([p_scr[pl.ds(g, n_sub, stride=SUBLANES), :] for g in range(N_KV)], axis=0)
    out = _dot(_silu(pre).astype(BF16), w2_ref[...])
    if not has_new:
        blk = lax.broadcasted_iota(jnp.int32, (N_KV * n_sub, 1), 0) & (n_sub - 1)
        out = jnp.where(blk == n_sub - 1, 0.0, out)
    return out


def _compress_kv(k_pages, v_pages, k_new, v_new, k_w, v_w, scr, *, has_new):
    rows_pp = (PAGE_SIZE // CMP_STRIDE) * SUBLANES
    for p0 in range(0, len(k_pages), COMPRESS_CHUNK_PAGES):
        p1 = min(p0 + COMPRESS_CHUNK_PAGES, len(k_pages))
        rs = slice(p0 * rows_pp, p1 * rows_pp)
        _compress_fill(k_pages[p0:p1], scr[0], p0)
        _compress_fill(v_pages[p0:p1], scr[3], p0)
        scr[1][rs, :] = _dot(scr[0][rs, :], k_w[0][...])
        scr[4][rs, :] = _dot(scr[3][rs, :], v_w[0][...])
    kc = _compress_finish(k_new, *k_w, scr[1], scr[2], has_new=has_new)
    vc = _compress_finish(v_new, *v_w, scr[4], scr[5], has_new=has_new)
    return kc, vc


def _compress_scratch(n_pages):
    n_rows = n_pages * (PAGE_SIZE // CMP_STRIDE) * SUBLANES
    return [pltpu.VMEM((n_rows, CMP_STRIDE * HEAD_DIM), BF16),
            pltpu.VMEM((n_rows + 2 * SUBLANES, 2 * HEAD_DIM), F32),
            pltpu.VMEM((n_rows, HEAD_DIM), F32)]


def _compress_prompt_kernel(pt_ref, *refs, n_pages):
    kp, vp = refs[:n_pages], refs[n_pages:2 * n_pages]
    wck, w2k, pek, wcv, w2v, pev, kc_ref, vc_ref, *scr = refs[2 * n_pages:]
    n_sub = n_pages * (PAGE_SIZE // CMP_STRIDE)
    assert n_sub & (n_sub - 1) == 0
    kc_ref[0], vc_ref[0] = _compress_kv(kp, vp, None, None, (wck, w2k, pek), (wcv, w2v, pev), scr, has_new=False)


def _page_specs(n_pages):
    return [pl.BlockSpec((1, PAGE_SIZE * N_KV, HEAD_DIM), lambda b, pt, p=p: (pt[b, p], 0, 0)) for p in range(n_pages)]


def _const_spec(shape):
    return pl.BlockSpec(shape, lambda b, pt: (0,) * len(shape))


def _compress_prompt(k_rows, v_rows, cw):
    bsz = k_rows.shape[0]
    t = k_rows.shape[1] // N_KV
    n_pages = t // PAGE_SIZE
    n_sub = t // CMP_STRIDE
    pt = jnp.arange(bsz * n_pages, dtype=jnp.int32).reshape(bsz, n_pages)
    kp = k_rows.reshape(bsz * n_pages, PAGE_SIZE * N_KV, HEAD_DIM)
    vp = v_rows.reshape(bsz * n_pages, PAGE_SIZE * N_KV, HEAD_DIM)
    wspecs = [_const_spec(a.shape) for a in cw]
    grid_spec = pltpu.PrefetchScalarGridSpec(
        num_scalar_prefetch=1,
        grid=(bsz,),
        in_specs=_page_specs(n_pages) * 2 + wspecs,
        out_specs=[pl.BlockSpec((1, N_KV * n_sub, HEAD_DIM), lambda b, pt: (b, 0, 0))] * 2,
        scratch_shapes=_compress_scratch(n_pages) * 2,
    )
    return pl.pallas_call(
        functools.partial(_compress_prompt_kernel, n_pages=n_pages),
        grid_spec=grid_spec,
        out_shape=[jax.ShapeDtypeStruct((bsz, N_KV * n_sub, HEAD_DIM), F32)] * 2,
        compiler_params=_params("parallel"),
        name="compress_prompt",
    )(pt, *([kp] * n_pages), *([vp] * n_pages), *cw)


def _head_rows(ref, base):
    return jnp.concatenate([ref[0, :, base + h * HEAD_DIM:base + (h + 1) * HEAD_DIM] for h in range(N_HEADS)], axis=0)


def _split_dot(a, b):
    hi = a.astype(BF16)
    lo = (a - hi.astype(F32)).astype(BF16)
    return _dot(hi, b) + _dot(lo, b)


def _compress_sample_kernel(pt_ref, ck_hbm, cv_hbm, newk, newv, wck, w2k, pek, wcv, w2v, pev, q_ref, tcs_ref, ovl_ref,
                            pow_ref, oc_ref, bits_ref, kbuf, vbuf, sem, *scr, n_pages, cur_blk):
    b = pl.program_id(0)
    slot = lax.rem(b, 2)

    def page_copies(seq, slot):
        out = []
        for p in range(n_pages):
            page = pt_ref[seq, p]
            out.append(pltpu.make_async_copy(ck_hbm.at[page], kbuf.at[slot, p], sem.at[slot, 0]))
            out.append(pltpu.make_async_copy(cv_hbm.at[page], vbuf.at[slot, p], sem.at[slot, 1]))
        return out

    @pl.when(b == 0)
    def _():
        for c in page_copies(0, 0):
            c.start()

    @pl.when(b + 1 < pl.num_programs(0))
    def _():
        for c in page_copies(b + 1, 1 - slot):
            c.start()

    for c in page_copies(b, slot):
        c.wait()
    kp = [kbuf.at[slot, p] for p in range(n_pages)]
    vp = [vbuf.at[slot, p] for p in range(n_pages)]
    kc, vc = _compress_kv(kp, vp, newk, newv, (wck, w2k, pek), (wcv, w2v, pev), scr, has_new=True)
    q16 = (_head_rows(q_ref, 0) * SCALE).astype(BF16)
    s = _dot_nt(q16, kc.astype(BF16)) + tcs_ref[...]
    e = jnp.exp(s - jnp.max(s, axis=-1, keepdims=True))
    p = e / jnp.sum(e, axis=-1, keepdims=True)
    oc_ref[0] = _dot(p.astype(BF16), vc.astype(BF16))
    imp_h = _split_dot(p, ovl_ref[...])
    imp = jnp.concatenate([jnp.sum(imp_h[GROUP_REP * g:GROUP_REP * (g + 1)], axis=0, keepdims=True)
                           for g in range(N_KV)], axis=0)
    lane = lax.broadcasted_iota(jnp.int32, (1, LANES), 1)
    forced = (lane == cur_blk) | (lane == 0)
    imp = jnp.where(forced, FORCE_SCORE, jnp.where(lane <= cur_blk, imp, -FORCE_SCORE))
    rank = jnp.zeros((N_KV, LANES), F32)
    for k in range(cur_blk + 1):
        ck = imp[:, k:k + 1]
        rank = rank + jnp.where(ck > imp, 1.0, jnp.where((ck == imp) & (lane > k), 1.0, 0.0))
    sel = jnp.where(rank < N_SEL, 1.0, 0.0) * pow_ref[...]
    lo = jnp.sum(jnp.where(lane < 16, sel, 0.0), axis=-1, keepdims=True).astype(jnp.int32)
    hi = jnp.sum(jnp.where((lane >= 16) & (lane < 32), sel, 0.0), axis=-1, keepdims=True).astype(jnp.int32)
    bits_ref[0] = jnp.broadcast_to(lo | (hi << 16), (N_KV, LANES))


def _compress_sample(cache_k, cache_v, page_table, new_k, new_v, cw, q, tcs, ovl, *, past):
    nb, n_pages = page_table.shape
    width = new_k.shape[-1]
    n_sub = n_pages * (PAGE_SIZE // CMP_STRIDE)
    pow2 = np.zeros((1, LANES), np.float32)
    pow2[0, :32] = 2.0 ** (np.arange(32) % 16)
    row_spec = pl.BlockSpec((1, 1, width), lambda b, pt: (b, 0, 0))
    grid_spec = pltpu.PrefetchScalarGridSpec(
        num_scalar_prefetch=1,
        grid=(nb,),
        in_specs=([pl.BlockSpec(memory_space=pl.ANY)] * 2 + [row_spec, row_spec] + [_const_spec(a.shape) for a in cw]
                  + [pl.BlockSpec((1, 1, q.shape[-1]), lambda b, pt: (b, 0, 0)),
                     _const_spec(tcs.shape), _const_spec(ovl.shape), _const_spec(pow2.shape)]),
        out_specs=[pl.BlockSpec((1, N_HEADS, HEAD_DIM), lambda b, pt: (b, 0, 0)),
                   pl.BlockSpec((1, N_KV, LANES), lambda b, pt: (b, 0, 0))],
        scratch_shapes=([pltpu.VMEM((2, n_pages) + cache_k.shape[1:], F32)] * 2
                        + [pltpu.SemaphoreType.DMA((2, 2))]
                        + _compress_scratch(n_pages) * 2),
    )
    return pl.pallas_call(
        functools.partial(_compress_sample_kernel, n_pages=n_pages, cur_blk=past // SLC_BLOCK),
        grid_spec=grid_spec,
        out_shape=[jax.ShapeDtypeStruct((nb, N_HEADS, HEAD_DIM), F32),
                   jax.ShapeDtypeStruct((nb, N_KV, LANES), jnp.int32)],
        compiler_params=_params("arbitrary"),
        name="compress_select_sample",
    )(page_table, cache_k, cache_v, new_k, new_v, *cw, q, tcs, ovl, jnp.asarray(pow2))


def _attn_prompt_kernel(q_ref, z0_ref, z1_ref, z2_ref, gt_ref, kc_ref, vc_ref, ks_ref, vs_ref, kw_ref, vw_ref,
                        t0_ref, t1_ref, tc_ref, ovl_ref, y_ref, m0_scr, m1_scr, acc0_scr, acc1_scr, sel_scr):
    m_scrs, acc_scrs = (m0_scr, m1_scr), (acc0_scr, acc1_scr)
    qt = pl.program_id(2)
    q = q_ref[...]
    qf = jnp.concatenate([q[:, r * HEAD_DIM:(r + 1) * HEAD_DIM] for r in range(GROUP_REP)], axis=0)
    qf = (qf * (SCALE * LOG2E)).astype(BF16)
    lane_q = lax.broadcasted_iota(jnp.int32, (1, TQ), 1)
    qpos = qt * TQ + lane_q
    key_l = lax.broadcasted_iota(jnp.int32, (TQ, 1), 0)
    gate_t = gt_ref[...].T
    z_refs = (z0_ref, z1_ref, z2_ref)

    def emit(br, r, o_t, first):
        cs = slice(r * HEAD_DIM, (r + 1) * HEAD_DIM)
        term = o_t.T * z_refs[br][:, cs]
        if first:
            y_ref[:, cs] = term
        else:
            y_ref[:, cs] += term

    blk_per_tile = TQ // SLC_BLOCK

    def compressed_and_select():
        n_cmp = kc_ref.shape[1]
        s_all = _dot_nt(kc_ref[0].astype(BF16), qf)
        vc_t = vc_ref[0].T.astype(BF16)
        cmp_end = lax.broadcasted_iota(jnp.int32, (n_cmp, 1), 0) * CMP_STRIDE + (CMP_LEN - 1)
        vis = qpos >= cmp_end
        any_vis = qpos >= CMP_LEN - 1
        row0 = pl.multiple_of(TQ // 2 - (TQ // CMP_STRIDE) * qt, CMP_STRIDE)
        psum = jnp.zeros((n_cmp, TQ), F32)
        for r in range(GROUP_REP):
            s = s_all[:, r * TQ:(r + 1) * TQ] + tc_ref[r, pl.ds(row0, n_cmp), :]
            s = jnp.where(vis, s, NEG_INF)
            e = jnp.exp2(s - jnp.max(s, axis=0, keepdims=True))
            p = jnp.where(any_vis, e / jnp.sum(e, axis=0, keepdims=True), 0.0)
            psum = psum + p
            emit(0, r, _dot(vc_t, p.astype(BF16)) * gate_t[r:r + 1, :], True)
        n_slc = ovl_ref.shape[0]
        psum_hi = psum.astype(BF16)
        imp = _dot(ovl_ref[...], psum_hi) + _dot(ovl_ref[...], (psum - psum_hi.astype(F32)).astype(BF16))
        blk = lax.broadcasted_iota(jnp.int32, (n_slc, 1), 0)
        cur = lax.shift_right_logical(qpos, int(math.log2(SLC_BLOCK)))
        forced = (blk == cur) | (blk == 0)
        imp = jnp.where(forced, FORCE_SCORE, jnp.where(blk <= cur, imp, -FORCE_SCORE))
        chunks = [imp[c:c + SUBLANES, :] for c in range(0, n_slc, SUBLANES)]
        ranks = [jnp.zeros((SUBLANES, TQ), F32) for _ in chunks]
        for k in range(n_slc):
            rk = imp[k:k + 1, :]
            for c, chunk in enumerate(chunks):
                lo = c * SUBLANES
                if lo > k:
                    beats = rk >= chunk
                elif lo + SUBLANES - 1 <= k:
                    beats = rk > chunk
                else:
                    beats = (rk > chunk) | ((rk == chunk) & (blk[lo:lo + SUBLANES] > k))
                ranks[c] = ranks[c] + jnp.where(beats, 1.0, 0.0)
        rank = jnp.concatenate(ranks, axis=0)
        sel = jnp.where(rank < min(N_SEL, n_slc), 1.0, 0.0)
        for t in range(n_slc // blk_per_tile):
            sel_scr[t, 0:blk_per_tile, :] = sel[t * blk_per_tile:(t + 1) * blk_per_tile, :]

    ones_rows = jnp.ones((ACC_ROWS - HEAD_DIM, TQ), BF16)

    refs = ((ks_ref, vs_ref), (kw_ref, vw_ref))

    def attend(tiles, first):
        k_ts, v_ts, masks = [], [], []
        for st, kt, kind in tiles:
            k_ref, v_ref = refs[st]
            start = pl.multiple_of(kt * TQ, TQ)
            k_ts.append(k_ref[pl.ds(start, TQ), :])
            v_t = v_ref[pl.ds(start, TQ), :].T.astype(BF16)
            v_ts.append(jnp.concatenate([v_t, ones_rows], axis=0))
            mask = None
            if kind == "diag":
                mask = key_l <= lane_q
            elif kind == "winfar":
                mask = key_l > lane_q
            if st == 0:
                sel4 = sel_scr[kt, 0:blk_per_tile, :]
                selm = jnp.concatenate(
                    [jnp.broadcast_to(sel4[j:j + 1, :], (SLC_BLOCK, TQ)) for j in range(blk_per_tile)], axis=0) > 0.5
                mask = selm if mask is None else mask & selm
            masks.append(mask)
        s_all = _dot_nt(k_ts[0] if len(k_ts) == 1 else jnp.concatenate(k_ts, axis=0), qf)
        for r in range(GROUP_REP):
            sl = slice(r * TQ, (r + 1) * TQ)
            for st in sorted(set(t[0] for t in tiles)):
                ss, vs = [], []
                for i, ((st_i, kt, kind), mask) in enumerate(zip(tiles, masks)):
                    if st_i != st:
                        continue
                    s = s_all[i * TQ:(i + 1) * TQ, sl]
                    if kind == "diag":
                        s = s + t0_ref[r]
                    elif kind == "near":
                        s = s + t1_ref[r]
                    if mask is not None:
                        s = jnp.where(mask, s, NEG_INF)
                    ss.append(s)
                    vs.append(v_ts[i])
                mx = jnp.max(ss[0], axis=0, keepdims=True)
                for s in ss[1:]:
                    mx = jnp.maximum(mx, jnp.max(s, axis=0, keepdims=True))
                if first:
                    m_new = mx
                    upd = None
                else:
                    m_old = m_scrs[st][:, sl]
                    m_new = jnp.maximum(m_old, mx)
                    upd = jnp.exp2(m_old - m_new) * acc_scrs[st][:, sl]
                for s, v_t in zip(ss, vs):
                    pv = _dot(v_t, jnp.exp2(s - m_new).astype(BF16))
                    upd = pv if upd is None else upd + pv
                acc_scrs[st][:, sl] = upd
                m_scrs[st][:, sl] = m_new

    compressed_and_select()
    attend([(0, qt, "diag"), (1, qt, "diag")], True)

    @pl.when(qt >= 1)
    def _():
        attend([(0, qt - 1, "near"), (1, qt - 1, "near")], False)

    @pl.when(qt >= 2)
    def _():
        n_far = qt - 1

        def far_pair(i, carry):
            attend([(0, 2 * i, "far"), (0, 2 * i + 1, "far")], False)
            return carry

        lax.fori_loop(0, lax.shift_right_logical(n_far, 1), far_pair, 0)

        @pl.when((n_far & 1) == 1)
        def _():
            attend([(1, qt - 2, "winfar"), (0, n_far - 1, "far")], False)

        @pl.when((n_far & 1) == 0)
        def _():
            attend([(1, qt - 2, "winfar")], False)

    for st in range(2):
        br = st + 1
        for r in range(GROUP_REP):
            sl = slice(r * TQ, (r + 1) * TQ)
            w = gate_t[br * GROUP_REP + r:br * GROUP_REP + r + 1, :] / acc_scrs[st][HEAD_DIM:HEAD_DIM + 1, sl]
            emit(br, r, acc_scrs[st][0:HEAD_DIM, sl] * w, False)


def _attn_prompt(proj, gates, kc, vc, k_slc, v_slc, k_win, v_win, t0, t1, tc, ovl_t, *, bsz, t):
    nq = t // TQ
    gw = GROUP_REP * HEAD_DIM
    assert TQ // 2 - (TQ // CMP_STRIDE) * (nq - 1) >= 0 and kc.shape[1] == N_KV * (t // CMP_STRIDE)
    n_cmp = t // CMP_STRIDE
    d_att = N_HEADS * HEAD_DIM
    zoff = d_att // gw
    row = lambda b, g, i: b * nq + i
    in_specs = [
        pl.BlockSpec((TQ, gw), lambda b, g, i: (row(b, g, i), g)),
    ] + [
        pl.BlockSpec((TQ, gw), lambda b, g, i, br=br: (row(b, g, i), zoff * (1 + br) + g)) for br in range(N_BRANCH)
    ] + [
        pl.BlockSpec((TQ, LANES), lambda b, g, i: (row(b, g, i), g)),
        pl.BlockSpec((1, n_cmp, HEAD_DIM), lambda b, g, i: (b, g, 0)),
        pl.BlockSpec((1, n_cmp, HEAD_DIM), lambda b, g, i: (b, g, 0)),
    ] + [pl.BlockSpec((None, None, t, HEAD_DIM), lambda b, g, i: (b, g, 0, 0))] * 4 + [
        pl.BlockSpec((GROUP_REP, TQ, TQ), lambda b, g, i: (g, 0, 0))] * 3 + [
        pl.BlockSpec(ovl_t.shape, lambda b, g, i: (0, 0)),
    ]
    return pl.pallas_call(
        _attn_prompt_kernel,
        grid=(bsz, N_KV, nq),
        in_specs=in_specs,
        out_specs=pl.BlockSpec((TQ, gw), lambda b, g, i: (row(b, g, i), g)),
        out_shape=jax.ShapeDtypeStruct((bsz * t, d_att), F32),
        scratch_shapes=[
            pltpu.VMEM((1, GROUP_REP * TQ), F32),
            pltpu.VMEM((1, GROUP_REP * TQ), F32),
            pltpu.VMEM((ACC_ROWS, GROUP_REP * TQ), F32),
            pltpu.VMEM((ACC_ROWS, GROUP_REP * TQ), F32),
            pltpu.VMEM((t // TQ, SUBLANES, TQ), F32),
        ],
        compiler_params=_params("parallel", "parallel", "arbitrary"),
        name="nsa_attention_prompt",
    )(proj, proj, proj, proj, gates, kc, vc, k_slc, v_slc, k_win, v_win, t0, t1, tc, ovl_t)


def _attn_sample_kernel(bits_ref, pt_ref, *refs, n_pages):
    kp, vp = refs[:n_pages], refs[n_pages:2 * n_pages]
    (kwin_ref, vwin_ref, nks_ref, nvs_ref, nkw_ref, nvw_ref, q_ref, gt_ref, oc_ref, ts_ref, tw_ref,
     y_ref, kwo_ref, vwo_ref) = refs[2 * n_pages:]
    b = pl.program_id(0)
    rows_pp = PAGE_SIZE * N_KV
    blk_rows = SLC_BLOCK * N_KV
    d_att = N_HEADS * HEAD_DIM
    n_win_rows = kwin_ref.shape[1]
    head_rows = functools.partial(_head_rows, q_ref)

    for src, new, dst in ((kwin_ref, nkw_ref, kwo_ref), (vwin_ref, nvw_ref, vwo_ref)):
        dst[0, 0:n_win_rows - N_KV, :] = src[0, N_KV:n_win_rows, :]
        dst[0, n_win_rows - N_KV:n_win_rows, :] = jnp.concatenate(
            [new[0, :, g * HEAD_DIM:(g + 1) * HEAD_DIM] for g in range(N_KV)], axis=0)

    def group_rows(ref):
        return jnp.concatenate([ref[0, :, (h // GROUP_REP) * HEAD_DIM:(h // GROUP_REP + 1) * HEAD_DIM]
                                for h in range(N_HEADS)], axis=0)

    def softmax_pv(s_tiles, v_tiles, s_new, v_new):
        m = s_new
        for s in s_tiles:
            m = jnp.maximum(m, jnp.max(s, axis=-1, keepdims=True))
        p_new = jnp.exp(s_new - m)
        l = p_new
        acc = p_new * v_new
        for s, v in zip(s_tiles, v_tiles):
            p = jnp.exp(s - m)
            l = l + jnp.sum(p, axis=-1, keepdims=True)
            acc = acc + _dot(p.astype(BF16), v.astype(BF16))
        return acc / l

    q32 = head_rows(0) * SCALE
    q16 = q32.astype(BF16)
    head = lax.broadcasted_iota(jnp.int32, (N_HEADS, 1), 0)
    bits = jnp.zeros((N_HEADS, 1), jnp.int32)
    for g in range(N_KV):
        bits = jnp.where((head >= g * GROUP_REP) & (head < (g + 1) * GROUP_REP), bits_ref[b, g], bits)
    lane_blk = lax.broadcasted_iota(jnp.int32, (1, rows_pp), 1) // blk_rows
    s_tiles, v_tiles = [], []
    for p in range(n_pages):
        s = _dot_nt(q16, kp[p][0].astype(BF16)) + ts_ref[:, p * rows_pp:(p + 1) * rows_pp]
        shift = jnp.broadcast_to(lane_blk + p * (rows_pp // blk_rows), s.shape)
        sel = lax.shift_right_logical(jnp.broadcast_to(bits, s.shape), shift) & 1
        s_tiles.append(jnp.where(sel == 1, s, NEG_INF))
        v_tiles.append(vp[p][0])
    tail = n_pages * rows_pp
    s_new = jnp.sum(q32 * group_rows(nks_ref), axis=-1, keepdims=True) + ts_ref[:, tail:tail + 1]
    o_slc = softmax_pv(s_tiles, v_tiles, s_new, group_rows(nvs_ref))
    s_tiles, v_tiles = [], []
    for p in range(n_win_rows // rows_pp):
        rows = slice(p * rows_pp, (p + 1) * rows_pp)
        s_tiles.append(_dot_nt(q16, kwin_ref[0, rows, :].astype(BF16)) + tw_ref[:, rows])
        v_tiles.append(vwin_ref[0, rows, :])
    s_new = jnp.sum(q32 * group_rows(nkw_ref), axis=-1, keepdims=True) + tw_ref[:, n_win_rows:n_win_rows + 1]
    o_win = softmax_pv(s_tiles, v_tiles, s_new, group_rows(nvw_ref))
    y = jnp.zeros((N_HEADS, HEAD_DIM), F32)
    for br, o in enumerate((oc_ref[0], o_slc, o_win)):
        cols = [(h // GROUP_REP) * LANES + br * GROUP_REP + h % GROUP_REP for h in range(N_HEADS)]
        gate = jnp.concatenate([gt_ref[0, :, c:c + 1] for c in cols], axis=0)
        y = y + gate * o * head_rows((1 + br) * d_att)
    y_ref[0] = y


def _attn_sample(cache_k, cache_v, page_table, bits, k_win, v_win, new_rows, proj, gates, o_cmp, ts, tw):
    nb, n_pages = page_table.shape
    rows_pp = PAGE_SIZE * N_KV
    win_spec = pl.BlockSpec((1, k_win.shape[1], HEAD_DIM), lambda b, bits, pt: (b, 0, 0))
    row_spec = lambda w: pl.BlockSpec((1, 1, w), lambda b, bits, pt: (b, 0, 0))
    const = lambda a: pl.BlockSpec(a.shape, lambda b, bits, pt: (0,) * a.ndim)
    page_specs = [pl.BlockSpec((1, rows_pp, HEAD_DIM), lambda b, bits, pt, p=p: (pt[b, p], 0, 0)) for p in range(n_pages)]
    grid_spec = pltpu.PrefetchScalarGridSpec(
        num_scalar_prefetch=2,
        grid=(nb,),
        in_specs=(page_specs * 2 + [win_spec] * 2
                  + [row_spec(N_KV * HEAD_DIM)] * 4
                  + [row_spec(proj.shape[-1]), row_spec(gates.shape[-1]),
                     pl.BlockSpec((1, N_HEADS, HEAD_DIM), lambda b, bits, pt: (b, 0, 0)),
                     const(ts), const(tw)]),
        out_specs=[pl.BlockSpec((1, N_HEADS, HEAD_DIM), lambda b, bits, pt: (b, 0, 0)), win_spec, win_spec],
    )
    return pl.pallas_call(
        functools.partial(_attn_sample_kernel, n_pages=n_pages),
        grid_spec=grid_spec,
        out_shape=[jax.ShapeDtypeStruct((nb, N_HEADS, HEAD_DIM), F32),
                   jax.ShapeDtypeStruct(k_win.shape, F32), jax.ShapeDtypeStruct(v_win.shape, F32)],
        compiler_params=_params("parallel"),
        name="nsa_attention_sample",
    )(bits, page_table, *([cache_k] * n_pages), *([cache_v] * n_pages), k_win, v_win, *new_rows,
      proj, gates, o_cmp, ts, tw)


def _overlap(n_cmp, n_slc):
    cs = np.arange(n_cmp) * CMP_STRIDE
    bs = np.arange(n_slc) * SLC_BLOCK
    ov = np.clip(np.minimum(cs[:, None] + CMP_LEN, bs[None, :] + SLC_BLOCK) - np.maximum(cs[:, None], bs[None, :]), 0, None)
    return (ov / CMP_LEN).astype(np.float32)


def _skip_unselected_pages(page_table, bits):
    nb, n_pages = page_table.shape
    blk_per_page = PAGE_SIZE // SLC_BLOCK
    assert n_pages * blk_per_page <= 32
    any_bits = functools.reduce(jnp.bitwise_or, [bits[:, g] for g in range(bits.shape[1])])
    shifts = blk_per_page * jnp.arange(n_pages, dtype=jnp.int32)
    needed = ((any_bits[:, None] >> shifts[None, :]) & ((1 << blk_per_page) - 1)) != 0
    seq = jnp.arange(nb, dtype=jnp.int32)[:, None]
    last_needed = lax.cummax(jnp.where(needed | (seq == 0), seq, 0), axis=0)
    return jnp.take_along_axis(page_table, last_needed, axis=0)


def _compress_weights(w1, w2, pe):
    half = CMP_STRIDE * HEAD_DIM
    wcat = jnp.concatenate([w1[:half], w1[half:]], axis=1).astype(BF16)
    pe8 = jnp.pad(pe.reshape(2, half), ((0, SUBLANES - 2), (0, 0)))
    return wcat, w2.astype(BF16), pe8


def kernel(x_prompt, x_sample, state_pool, cache_k_cmp, cache_v_cmp, cache_k_slc, cache_v_slc, state_k_win, state_v_win, page_table, c_prompt, c_sample, g_norm, w_ada, b_ada, w_in_a, w_grp, pool_scale, w_out_a, g_kv, w_kv, pe_k, w_ck1, w_ck2, pe_v, w_cv1, w_cv2, rel_bias, w_in_b, w_out_b, g_final):
    bsz, t, d = x_prompt.shape
    nb = x_sample.shape[0]
    n_pages = page_table.shape[1]
    past = n_pages * PAGE_SIZE
    d_att = N_HEADS * HEAD_DIM
    d_kv = N_KV * HEAD_DIM
    n_a = w_in_a.shape[0]
    assert n_a == 1 and w_in_b.shape[0] == 1 and x_sample.shape[1] == 1
    assert t % TQ == 0 and t >= WINDOW and past % PAGE_SIZE == 0 and state_k_win.shape[1] == WINDOW

    w_in_a16 = w_in_a[0].astype(BF16)
    w_grp16 = w_grp[0].astype(BF16)
    w_out_a16 = w_out_a[0].astype(BF16)
    w_kv16 = w_kv.astype(BF16)
    w_out_b16 = w_out_b[0].astype(BF16)
    n_qz = (1 + N_BRANCH) * d_att
    wg = w_in_b[0][:, n_qz:].reshape(d, N_BRANCH, N_KV, GROUP_REP).transpose(0, 2, 1, 3)
    wg = jnp.pad(wg.reshape(d, N_KV, N_BRANCH * GROUP_REP), ((0, 0), (0, 0), (0, LANES - N_BRANCH * GROUP_REP)))
    w_gate16 = wg.reshape(d, N_KV * LANES).astype(BF16)
    nsa_proj = functools.partial(_nm_matmul, w=w_in_b[0].astype(BF16), w_tail=w_gate16, n_cols=n_qz, tn=TN_PROJ,
                                 silu_cols=d_att)
    cw_k = _compress_weights(w_ck1, w_ck2, pe_k)
    cw_v = _compress_weights(w_cv1, w_cv2, pe_v)
    cw = cw_k + cw_v

    mod = _ada(jnp.concatenate([c_prompt, c_sample], axis=0), w_ada, b_ada)

    def modulation(l, lo, hi, per_row):
        parts = [mod[l, lo:hi, k * d:(k + 1) * d] for k in range(3)]
        return [p[None] if per_row else p[:, None] for p in parts]

    t0, t1, tc, ts, tw, tcs = _bias_tables(rel_bias, past=past, n_cmp_s=past // CMP_STRIDE)
    ts, tw, tcs = ts.reshape(N_HEADS, -1), tw.reshape(N_HEADS, -1), tcs.reshape(N_HEADS, -1)

    shift, scale, gate = modulation(0, 0, bsz, False)
    uz = _nm_matmul(x_prompt, g_norm[0], scale, shift, w_in_a16, n_cols=2 * d, tm=TM_PROJ, tn=TN_PROJ, name="in_proj_pool_prompt")
    pool_p = uz[:, t - POOL_BUF:, :d][None]
    x1 = _pool_prompt(uz, w_grp16, pool_scale[0], w_out_a16, x_prompt, gate, tm=256)
    kv_p = _kv_proj(x1, g_kv, w_kv16, tm=256, name="kv_proj_prompt")
    heads = lambda a: a.reshape(a.shape[0], -1, N_KV, HEAD_DIM)
    kv_state_p = [heads(a) for a in kv_p[:4]] + [heads(a[:, (t - WINDOW) * N_KV:]) for a in kv_p[4:6]]
    kc_p, vc_p = _compress_prompt(kv_p[0], kv_p[1], cw)
    shift, scale, gate = modulation(1, 0, bsz, False)
    proj, gates = nsa_proj(x1, g_norm[1], scale, shift, tm=TM_PROJ, name="in_proj_nsa_prompt")
    ovl_t = jnp.asarray(np.pad(_overlap(t // CMP_STRIDE - 1, t // SLC_BLOCK), ((0, 1), (0, 0))).T).astype(BF16)
    flat = lambda a: a.reshape(bsz * t, a.shape[-1])
    y1 = _attn_prompt(flat(proj), flat(gates), kc_p, vc_p, kv_p[6], kv_p[7], kv_p[8], kv_p[9],
                      t0, t1, tc, ovl_t, bsz=bsz, t=t)
    y_prompt = _proj_residual(y1.reshape(bsz, t, d_att), w_out_b16, x1, gate, g_final, tm=512, final_norm=True,
                              name="out_proj_nsa_prompt")

    xs = x_sample.reshape(1, nb, d)
    shift, scale, gate = modulation(0, bsz, bsz + nb, True)
    uz_s = _nm_matmul(xs, g_norm[0], scale, shift, w_in_a16, n_cols=2 * d, tm=nb, tn=512, name="in_proj_pool_sample")
    pool_s = jnp.concatenate([state_pool[:, :, 1:], uz_s[0, :, None, :d][None]], axis=2)
    y0_s = _pool_sample(uz_s, state_pool[0], w_grp16, pool_scale[0], qpos=past, tb=min(32, nb))
    x1_s = _proj_residual(y0_s, w_out_a16, xs, gate, g_final, tm=nb, final_norm=False, name="out_proj_pool_sample")
    kv_s = _kv_proj(x1_s, g_kv, w_kv16, tm=nb, name="kv_proj_sample")
    new_rows = [a.reshape(nb, 1, d_kv) for a in kv_s[:2 * N_BRANCH]]
    shift, scale, gate = modulation(1, bsz, bsz + nb, True)
    proj_s, gates_s = [a.reshape(nb, 1, -1) for a in nsa_proj(x1_s, g_norm[1], scale, shift, tm=nb, name="in_proj_nsa_sample")]
    n_cmp_s = past // CMP_STRIDE
    n_slc_s = past // SLC_BLOCK + 1
    ovl_s = jnp.asarray(np.tile(np.pad(_overlap(n_cmp_s, n_slc_s), ((0, 0), (0, LANES - n_slc_s))), (N_KV, 1))).astype(BF16)
    rows_of = lambda c: c.reshape(c.shape[0], PAGE_SIZE * N_KV, HEAD_DIM)
    o_cmp, bits = _compress_sample(rows_of(cache_k_cmp), rows_of(cache_v_cmp), page_table, new_rows[0], new_rows[1], cw,
                                   proj_s, tcs, ovl_s, past=past)
    bits = bits[:, :, 0]
    y1_s, k_win_s, v_win_s = _attn_sample(
        rows_of(cache_k_slc), rows_of(cache_v_slc), _skip_unselected_pages(page_table, bits), bits,
        state_k_win.reshape(nb, WINDOW * N_KV, HEAD_DIM), state_v_win.reshape(nb, WINDOW * N_KV, HEAD_DIM),
        new_rows[2:], proj_s, gates_s, o_cmp, ts, tw)
    y_sample = _proj_residual(y1_s.reshape(1, nb, d_att), w_out_b16, x1_s, gate, g_final, tm=nb, final_norm=True,
                              name="out_proj_nsa_sample").reshape(nb, 1, d)
    new4 = [a.reshape(nb, 1, N_KV, HEAD_DIM) for a in kv_s[:2 * N_BRANCH]]
    kv_state_s = new4[:4] + [k_win_s.reshape(state_k_win.shape), v_win_s.reshape(state_v_win.shape)]

    return (y_prompt, y_sample, pool_p, *kv_state_p, pool_s, *kv_state_s)
```

```python
import functools
import math

import numpy as np
import jax
import jax.numpy as jnp
from jax import lax
from jax.experimental import pallas as pl
from jax.experimental.pallas import tpu as pltpu

F32 = jnp.float32
BF16 = jnp.bfloat16

PAGE_SIZE = 128
POOL_WINDOWS = (2, 4, 8, 16)
POOL_BUF = max(POOL_WINDOWS) - 1
HEAD_DIM = 128
N_KV = 4
GROUP_REP = 4
N_HEADS = N_KV * GROUP_REP
N_BRANCH = 3
CMP_LEN = 32
CMP_STRIDE = 16
SLC_BLOCK = 64
N_SEL = 8
WINDOW = 512
N_BUCKETS = 32
MAX_DISTANCE = 128
RMS_EPS = 1e-6
SCALE = HEAD_DIM ** -0.5
LOG2E = math.log2(math.e)
NEG_INF = -1e30
FORCE_SCORE = 1e9

LANES = 128
SUBLANES = 8
VMEM_LIMIT_BYTES = 56 * 1024 * 1024
TQ = 256
COMPRESS_CHUNK_PAGES = 4
ACC_ROWS = HEAD_DIM + 16
TN_PROJ = 1024
TM_PROJ = 1024
HALO = 16


def _bucket_starts():
    max_exact = N_BUCKETS // 2
    d = np.arange(0, MAX_DISTANCE + 1)
    large = max_exact + np.floor(
        np.log(np.maximum(d, max_exact) / max_exact) / math.log(MAX_DISTANCE / max_exact) * (N_BUCKETS - max_exact)
    ).astype(np.int64)
    bucket = np.where(d < max_exact, d, np.minimum(large, N_BUCKETS - 1))
    return [int(np.argmax(bucket >= k)) for k in range(N_BUCKETS)]


BUCKET_STARTS = _bucket_starts()
FAR_DIST = BUCKET_STARTS[-1]


def _params(*sem):
    return pltpu.CompilerParams(dimension_semantics=sem, vmem_limit_bytes=VMEM_LIMIT_BYTES)


def _silu(x):
    return x * jax.nn.sigmoid(x)


def _rms(x, g):
    return x * lax.rsqrt(jnp.mean(x * x, axis=-1, keepdims=True) + RMS_EPS) * g


def _dot(a, b):
    return jnp.dot(a, b, preferred_element_type=F32)


def _dot_nt(a, b):
    return lax.dot_general(a, b, (((1,), (1,)), ((), ())), preferred_element_type=F32)


def _ada_kernel(c_ref, w_ref, b_ref, o_ref):
    a = _silu(c_ref[...]).astype(BF16)
    o_ref[0] = _dot(a, w_ref[0].astype(BF16)) + b_ref[0]


def _ada(c_all, w_ada, b_ada):
    depth, d, n = w_ada.shape
    m = c_all.shape[0]
    tn = 512
    return pl.pallas_call(
        _ada_kernel,
        grid=(depth, n // tn),
        in_specs=[
            pl.BlockSpec((m, d), lambda l, j: (0, 0)),
            pl.BlockSpec((1, d, tn), lambda l, j: (l, 0, j)),
            pl.BlockSpec((1, 1, tn), lambda l, j: (l, 0, j)),
        ],
        out_specs=pl.BlockSpec((1, m, tn), lambda l, j: (l, 0, j)),
        out_shape=jax.ShapeDtypeStruct((depth, m, n), F32),
        compiler_params=_params("parallel", "parallel"),
        name="ada_modulation",
    )(c_all, w_ada, b_ada.reshape(depth, 1, n))


def _nm_matmul_kernel(x_ref, g_ref, sc_ref, sh_ref, w_ref, *rest, silu_from, sigmoid_from):
    if len(rest) == 2:
        tail, (o_ref, h_scr) = None, rest
    else:
        tail, o_ref, ot_ref, h_scr = rest
    j = pl.program_id(2)

    @pl.when(j == 0)
    def _():
        y = _rms(x_ref[0], g_ref[...])
        h_scr[...] = (y * (1.0 + sc_ref[0]) + sh_ref[0]).astype(BF16)

    def column_tile():
        r = _dot(h_scr[...], w_ref[...])
        if silu_from is None:
            o_ref[0] = r
        else:
            sig = jax.nn.sigmoid(r)
            o_ref[0] = jnp.where(j >= sigmoid_from, sig, jnp.where(j >= silu_from, r * sig, r))

    def tail_tile():
        ot_ref[0] = jax.nn.sigmoid(_dot(h_scr[...], tail[...]))

    if tail is None:
        column_tile()
    else:
        last = pl.num_programs(2) - 1
        pl.when(j < last)(column_tile)
        pl.when(j == last)(tail_tile)


def _mod_spec(arr, tm):
    d = arr.shape[-1]
    if arr.shape[1] == 1:
        return pl.BlockSpec((1, 1, d), lambda b, i, *_: (b, 0, 0))
    return pl.BlockSpec((1, tm, d), lambda b, i, *_: (b, i, 0))


def _nm_matmul(x, g, scale, shift, w, *, n_cols, tm, tn, silu_cols=None, sigmoid_cols=None, w_tail=None, name):
    bsz, rows, d = x.shape
    n_j = n_cols // tn
    out_specs = pl.BlockSpec((1, tm, tn), lambda b, i, j: (b, i, j))
    out_shape = jax.ShapeDtypeStruct((bsz, rows, n_cols), F32)
    if w_tail is None:
        w_specs, w_args = [pl.BlockSpec((d, tn), lambda b, i, j: (0, j))], [w]
    else:
        tw = w_tail.shape[1]
        last_main = n_j - 1
        w_specs = [pl.BlockSpec((d, tn), lambda b, i, j: (0, jnp.minimum(j, last_main))),
                   pl.BlockSpec((d, tw), lambda b, i, j: (0, 0))]
        w_args = [w, w_tail]
        out_specs = [pl.BlockSpec((1, tm, tn), lambda b, i, j: (b, i, jnp.minimum(j, last_main))),
                     pl.BlockSpec((1, tm, tw), lambda b, i, j: (b, i, 0))]
        out_shape = [out_shape, jax.ShapeDtypeStruct((bsz, rows, tw), F32)]
        n_j += 1
    if silu_cols is None and sigmoid_cols is None:
        silu_from = sigmoid_from = None
    else:
        sigmoid_from = n_cols // tn if sigmoid_cols is None else sigmoid_cols // tn
        silu_from = sigmoid_from if silu_cols is None else silu_cols // tn
    return pl.pallas_call(
        functools.partial(_nm_matmul_kernel, silu_from=silu_from, sigmoid_from=sigmoid_from),
        grid=(bsz, rows // tm, n_j),
        in_specs=[
            pl.BlockSpec((1, tm, d), lambda b, i, j: (b, i, 0)),
            pl.BlockSpec((1, d), lambda b, i, j: (0, 0)),
            _mod_spec(scale, tm),
            _mod_spec(shift, tm),
        ] + w_specs,
        out_specs=out_specs,
        out_shape=out_shape,
        scratch_shapes=[pltpu.VMEM((tm, d), BF16)],
        compiler_params=_params("parallel", "parallel", "arbitrary"),
        name=name,
    )(x, g.reshape(1, d), scale, shift, *w_args)


def _pool_mix(pooled_fn, z, wg_ref, ps_ref, store):
    grp = z.shape[-1] // len(POOL_WINDOWS)
    for gi, w in enumerate(POOL_WINDOWS):
        cs = slice(gi * grp, (gi + 1) * grp)
        mixed = _dot(pooled_fn(gi, w, cs).astype(BF16), wg_ref[gi])
        store(cs, mixed * ps_ref[:, cs] * _silu(z[:, cs]))


def _pool_prompt_kernel(u_ref, z_ref, halo_ref, wg_ref, ps_ref, wo_ref, x_ref, gate_ref, o_ref,
                        ue_scr, a_scr, b_scr, y_scr):
    i = pl.program_id(1)
    tm = u_ref.shape[1]
    top = SUBLANES + HALO
    n = top + tm
    u = u_ref[0]
    ue_scr[0:SUBLANES, :] = jnp.zeros((SUBLANES, u.shape[1]), F32)
    ue_scr[SUBLANES:top, :] = jnp.where(i > 0, halo_ref[0], 0.0)
    ue_scr[top:n, :] = u
    zero8 = jnp.zeros((SUBLANES, a_scr.shape[1]), F32)
    a_scr[0:SUBLANES, :] = zero8
    b_scr[0:SUBLANES, :] = zero8
    pos = i * tm + lax.broadcasted_iota(jnp.int32, (tm, 1), 0)

    def pooled(gi, w, cs):
        src = lambda lo, hi: ue_scr[lo:hi, cs]
        k = 1
        for dst in (a_scr, b_scr, a_scr):
            if 2 * k >= w:
                break
            dst[SUBLANES:n, :] = src(SUBLANES, n) + src(SUBLANES - k, n - k)
            src = lambda lo, hi, dst=dst: dst[lo:hi, :]
            k *= 2
        acc = src(top, n) + src(top - k, n - k)
        cnt = jnp.minimum(pos + 1, w).astype(F32)
        return acc / cnt - u[:, cs]

    def store(cs, val):
        y_scr[:, cs] = val.astype(BF16)

    _pool_mix(pooled, z_ref[0], wg_ref, ps_ref, store)
    o_ref[0] = x_ref[0] + gate_ref[0] * _dot(y_scr[...], wo_ref[...])


def _pool_prompt(uz, w_grp, pool_scale, w_out, x, gate, *, tm):
    bsz, t, d2 = uz.shape
    d = d2 // 2
    ng, grp, _ = w_grp.shape
    return pl.pallas_call(
        _pool_prompt_kernel,
        grid=(bsz, t // tm),
        in_specs=[
            pl.BlockSpec((1, tm, d), lambda b, i: (b, i, 0)),
            pl.BlockSpec((1, tm, d), lambda b, i: (b, i, 1)),
            pl.BlockSpec((1, HALO, d), lambda b, i: (b, jnp.maximum(i * (tm // HALO) - 1, 0), 0)),
            pl.BlockSpec((ng, grp, grp), lambda b, i: (0, 0, 0)),
            pl.BlockSpec((1, d), lambda b, i: (0, 0)),
            pl.BlockSpec(w_out.shape, lambda b, i: (0, 0)),
            pl.BlockSpec((1, tm, d), lambda b, i: (b, i, 0)),
            _mod_spec(gate, tm),
        ],
        out_specs=pl.BlockSpec((1, tm, d), lambda b, i: (b, i, 0)),
        out_shape=jax.ShapeDtypeStruct((bsz, t, d), F32),
        scratch_shapes=[pltpu.VMEM((SUBLANES + HALO + tm, d), F32),
                        pltpu.VMEM((SUBLANES + HALO + tm, grp), F32),
                        pltpu.VMEM((SUBLANES + HALO + tm, grp), F32),
                        pltpu.VMEM((tm, d), BF16)],
        compiler_params=_params("parallel", "parallel"),
        name="pool_mixer_prompt",
    )(uz, uz, uz, w_grp, pool_scale.reshape(1, d), w_out, x, gate)


def _pool_sample_kernel(u_ref, z_ref, pre_ref, wg_ref, ps_ref, y_ref, *, qpos):
    u = u_ref[0]

    def pooled(gi, w, cs):
        acc = u[:, cs]
        for k in range(1, w):
            acc = acc + pre_ref[:, POOL_BUF - k, cs]
        return acc / float(min(qpos + 1, w)) - u[:, cs]

    def store(cs, val):
        y_ref[0, :, cs] = val

    _pool_mix(pooled, z_ref[0], wg_ref, ps_ref, store)


def _pool_sample(uz, prefix, w_grp, pool_scale, *, qpos, tb):
    _, nb, d2 = uz.shape
    d = d2 // 2
    ng, grp, _ = w_grp.shape
    return pl.pallas_call(
        functools.partial(_pool_sample_kernel, qpos=qpos),
        grid=(nb // tb,),
        in_specs=[
            pl.BlockSpec((1, tb, d), lambda i: (0, i, 0)),
            pl.BlockSpec((1, tb, d), lambda i: (0, i, 1)),
            pl.BlockSpec((tb, POOL_BUF, d), lambda i: (i, 0, 0)),
            pl.BlockSpec((ng, grp, grp), lambda i: (0, 0, 0)),
            pl.BlockSpec((1, d), lambda i: (0, 0)),
        ],
        out_specs=pl.BlockSpec((1, tb, d), lambda i: (0, i, 0)),
        out_shape=jax.ShapeDtypeStruct((1, nb, d), F32),
        compiler_params=_params("parallel"),
        name="pool_mixer_sample",
    )(uz, uz, prefix, w_grp, pool_scale.reshape(1, d))


def _proj_residual_kernel(y_ref, w_ref, x_ref, gate_ref, g_ref, o_ref, *, final_norm):
    xo = x_ref[0] + gate_ref[0] * _dot(y_ref[0].astype(BF16), w_ref[...])
    o_ref[0] = _rms(xo, g_ref[...]) if final_norm else xo


def _proj_residual(y, w, x, gate, g_final, *, tm, final_norm, name):
    bsz, rows, d = x.shape
    dy = y.shape[-1]
    return pl.pallas_call(
        functools.partial(_proj_residual_kernel, final_norm=final_norm),
        grid=(bsz, rows // tm),
        in_specs=[
            pl.BlockSpec((1, tm, dy), lambda b, i: (b, i, 0)),
            pl.BlockSpec((dy, d), lambda b, i: (0, 0)),
            pl.BlockSpec((1, tm, d), lambda b, i: (b, i, 0)),
            _mod_spec(gate, tm),
            pl.BlockSpec((1, d), lambda b, i: (0, 0)),
        ],
        out_specs=pl.BlockSpec((1, tm, d), lambda b, i: (b, i, 0)),
        out_shape=jax.ShapeDtypeStruct((bsz, rows, d), F32),
        compiler_params=_params("parallel", "parallel"),
        name=name,
    )(y, w, x, gate, g_final.reshape(1, d))


KV_GROUP_COPIES = ((2, BF16), (3, F32), (4, BF16), (5, F32))


def _kv_kernel(x_ref, g_ref, w_ref, *o_refs):
    h = _rms(x_ref[0], g_ref[...]).astype(BF16)
    n_out = 2 * N_BRANCH
    tm = x_ref.shape[1]
    n = N_KV * HEAD_DIM
    copies = dict((o, (o_refs[n_out + i], dt)) for i, (o, dt) in enumerate(KV_GROUP_COPIES))
    for o in range(n_out):
        r = _dot(h, w_ref[:, o * n:(o + 1) * n])
        for g in range(N_KV):
            rg = r[:, g * HEAD_DIM:(g + 1) * HEAD_DIM]
            o_refs[o][0, pl.ds(g, tm, stride=N_KV), :] = rg
            if o in copies:
                c_ref, dt = copies[o]
                c_ref[0, g] = rg.astype(dt)


def _kv_proj(x, g_kv, w_kv, *, tm, name):
    bsz, rows, d = x.shape
    n_out = 2 * N_BRANCH
    return pl.pallas_call(
        _kv_kernel,
        grid=(bsz, rows // tm),
        in_specs=[
            pl.BlockSpec((1, tm, d), lambda b, i: (b, i, 0)),
            pl.BlockSpec((1, d), lambda b, i: (0, 0)),
            pl.BlockSpec(w_kv.shape, lambda b, i: (0, 0)),
        ],
        out_specs=([pl.BlockSpec((1, tm * N_KV, HEAD_DIM), lambda b, i: (b, i, 0))] * n_out
                   + [pl.BlockSpec((1, N_KV, tm, HEAD_DIM), lambda b, i: (b, 0, i, 0))] * len(KV_GROUP_COPIES)),
        out_shape=([jax.ShapeDtypeStruct((bsz, rows * N_KV, HEAD_DIM), F32)] * n_out
                   + [jax.ShapeDtypeStruct((bsz, N_KV, rows, HEAD_DIM), dt) for _, dt in KV_GROUP_COPIES]),
        compiler_params=_params("parallel", "parallel"),
        name=name,
    )(x, g_kv.reshape(1, d), w_kv)


def _bias_lookup(rb_ref, h, dist):
    acc = jnp.full(dist.shape, rb_ref[N_BUCKETS - 1, h], F32)
    for k in range(N_BUCKETS - 2, -1, -1):
        acc = jnp.where(dist < BUCKET_STARTS[k + 1], rb_ref[k, h], acc)
    return acc


def _bias_tables_kernel(rb_ref, t0_ref, t1_ref, tc_ref, ts_ref, tw_ref, tcs_ref, *, past):
    h = pl.program_id(0)
    key = lax.broadcasted_iota(jnp.int32, (TQ, TQ), 0)
    qry = lax.broadcasted_iota(jnp.int32, (TQ, TQ), 1)
    far = rb_ref[N_BUCKETS - 1, h]
    rel2 = lambda dist: (_bias_lookup(rb_ref, h, dist) - far) * LOG2E
    t0_ref[0] = rel2(qry - key)
    t1_ref[0] = rel2(TQ + qry - key)
    tc_ref[0] = rel2(qry - CMP_STRIDE * (key - TQ // 2) - (CMP_LEN - 1))
    ls = lax.broadcasted_iota(jnp.int32, ts_ref.shape[1:], 1)
    own = (ls % N_KV) == h // GROUP_REP
    tok = ls // N_KV
    ts_ref[0] = jnp.where(ls == N_KV * past, rb_ref[0, h],
                          jnp.where(own & (tok < past), _bias_lookup(rb_ref, h, past - tok), NEG_INF))
    lw = lax.broadcasted_iota(jnp.int32, tw_ref.shape[1:], 1)
    own = (lw % N_KV) == h // GROUP_REP
    dw = WINDOW - lw // N_KV
    tw_ref[0] = jnp.where(lw == N_KV * WINDOW, rb_ref[0, h],
                          jnp.where(own & (dw > 0) & (dw < WINDOW) & (past - dw >= 0), _bias_lookup(rb_ref, h, dw), NEG_INF))
    lc = lax.broadcasted_iota(jnp.int32, tcs_ref.shape[1:], 1)
    n_cmp = tcs_ref.shape[2] // N_KV
    dc = past - CMP_STRIDE * (lc % n_cmp) - (CMP_LEN - 1)
    tcs_ref[0] = jnp.where((lc // n_cmp == h // GROUP_REP) & (dc >= 0), _bias_lookup(rb_ref, h, dc), NEG_INF)


def _bias_tables(rel_bias, *, past, n_cmp_s):
    nh = rel_bias.shape[1]
    ls = N_KV * past + LANES
    lw = N_KV * WINDOW + LANES
    shapes = [(nh, TQ, TQ)] * 3 + [(nh, 1, ls), (nh, 1, lw), (nh, 1, N_KV * n_cmp_s)]
    return pl.pallas_call(
        functools.partial(_bias_tables_kernel, past=past),
        grid=(nh,),
        in_specs=[pl.BlockSpec(memory_space=pltpu.SMEM)],
        out_specs=[pl.BlockSpec((1,) + s[1:], lambda h: (h, 0, 0)) for s in shapes],
        out_shape=[jax.ShapeDtypeStruct(s, F32) for s in shapes],
        compiler_params=_params("parallel"),
        name="rel_bias_tables",
    )(rel_bias)


def _compress_fill(page_refs, lhs_scr, first_page):
    sub_per_page = PAGE_SIZE // CMP_STRIDE
    pairs = CMP_STRIDE // 2
    rows_pp = sub_per_page * SUBLANES
    even = (lax.broadcasted_iota(jnp.int32, (rows_pp, 1), 0) & (SUBLANES - 1)) < N_KV
    for p, pref in enumerate(page_refs, first_page):
        x4 = pref[...].reshape(sub_per_page, pairs, SUBLANES, HEAD_DIM)
        rs = slice(p * rows_pp, (p + 1) * rows_pp)
        for j in range(pairs):
            t = x4[:, j].reshape(rows_pp, HEAD_DIM)
            lhs_scr[rs, 2 * j * HEAD_DIM:(2 * j + 1) * HEAD_DIM] = jnp.where(even, t, 0.0).astype(BF16)
            lhs_scr[rs, (2 * j + 1) * HEAD_DIM:(2 * j + 2) * HEAD_DIM] = jnp.where(even, 0.0, t).astype(BF16)


def _compress_finish(new_ref, wcat_ref, w2_ref, pe_ref, t_scr, p_scr, *, has_new):
    n_rows = p_scr.shape[0]
    n_sub = n_rows // SUBLANES
    wcat = wcat_ref[...]
    t_scr[n_rows:, :] = jnp.zeros((2 * SUBLANES, 2 * HEAD_DIM), F32)
    t_scr[0:n_rows, :] = t_scr[0:n_rows, :] + t_scr[N_KV:n_rows + N_KV, :]
    if has_new:
        new8 = jnp.concatenate([new_ref[0, :, g * HEAD_DIM:(g + 1) * HEAD_DIM] for g in range(N_KV)]
                               + [jnp.zeros((SUBLANES - N_KV, HEAD_DIM), F32)], axis=0)
        t_scr[n_rows:n_rows + SUBLANES, HEAD_DIM:] = _dot(new8.astype(BF16), wcat[0:HEAD_DIM, HEAD_DIM:])
    pe_r = _dot(pe_ref[...].astype(BF16), wcat)
    pe_const = pe_r[0:1, :HEAD_DIM] + pe_r[1:2, HEAD_DIM:]
    p_scr[...] = t_scr[0:n_rows, :HEAD_DIM] + t_scr[SUBLANES:n_rows + SUBLANES, HEAD_DIM:] + pe_const
    pre = jnp.concatenate([p_scr[pl.ds(g, n_sub, stride=SUBLANES), :] for g in range(N_KV)], axis=0)
    out = _dot(_silu(pre).astype(BF16), w2_ref[...])
    if not has_new:
        blk = lax.broadcasted_iota(jnp.int32, (N_KV * n_sub, 1), 0) & (n_sub - 1)
        out = jnp.where(blk == n_sub - 1, 0.0, out)
    return out


def _compress_kv(k_pages, v_pages, k_new, v_new, k_w, v_w, scr, *, has_new):
    rows_pp = (PAGE_SIZE // CMP_STRIDE) * SUBLANES
    for p0 in range(0, len(k_pages), COMPRESS_CHUNK_PAGES):
        p1 = min(p0 + COMPRESS_CHUNK_PAGES, len(k_pages))
        rs = slice(p0 * rows_pp, p1 * rows_pp)
        _compress_fill(k_pages[p0:p1], scr[0], p0)
        _compress_fill(v_pages[p0:p1], scr[3], p0)
        scr[1][rs, :] = _dot(scr[0][rs, :], k_w[0][...])
        scr[4][rs, :] = _dot(scr[3][rs, :], v_w[0][...])
    kc = _compress_finish(k_new, *k_w, scr[1], scr[2], has_new=has_new)
    vc = _compress_finish(v_new, *v_w, scr[4], scr[5], has_new=has_new)
    return kc, vc


def _compress_scratch(n_pages):
    n_rows = n_pages * (PAGE_SIZE // CMP_STRIDE) * SUBLANES
    return [pltpu.VMEM((n_rows, CMP_STRIDE * HEAD_DIM), BF16),
            pltpu.VMEM((n_rows + 2 * SUBLANES, 2 * HEAD_DIM), F32),
            pltpu.VMEM((n_rows, HEAD_DIM), F32)]


SLAB_PITCH = 136


def _regroup_pages(page_refs, x_scr):
    sub_per_page = PAGE_SIZE // CMP_STRIDE
    pairs = CMP_STRIDE // 2
    for p, pref in enumerate(page_refs):
        for s in range(sub_per_page):
            for j in range(pairs):
                rows = slice((s * pairs + j) * SUBLANES, (s * pairs + j + 1) * SUBLANES)
                tile = pref[0, rows, :] if len(pref.shape) == 3 else pref[rows, :]
                x_scr[pl.ds(SUBLANES * j * SLAB_PITCH + p * sub_per_page + s, SUBLANES, stride=SLAB_PITCH), :] = tile


def _regrouped_lhs(x_scr, n_sub):
    cols = [jnp.concatenate([x_scr[(N_KV * c + g) * SLAB_PITCH:(N_KV * c + g) * SLAB_PITCH + n_sub, :]
                             for g in range(N_KV)], axis=0) for c in range(CMP_STRIDE)]
    return jnp.concatenate(cols, axis=1).astype(BF16)


def _compress_finish_rows(acc, new_ref, wcat_ref, w2_ref, pe_ref, s_scr, *, has_new):
    n_rows = acc.shape[0]
    n_sub = n_rows // N_KV
    wcat = wcat_ref[...]
    s_scr[0:n_rows, :] = acc
    s_scr[n_rows:, :] = jnp.zeros((SUBLANES, 2 * HEAD_DIM), F32)
    pe_r = _dot(pe_ref[...].astype(BF16), wcat)
    pe_const = pe_r[0:1, :HEAD_DIM] + pe_r[1:2, HEAD_DIM:]
    if has_new:
        new8 = jnp.concatenate([new_ref[0, :, g * HEAD_DIM:(g + 1) * HEAD_DIM] for g in range(N_KV)]
                               + [jnp.zeros((SUBLANES - N_KV, HEAD_DIM), F32)], axis=0)
        new_term = _dot(new8.astype(BF16), wcat[0:HEAD_DIM, HEAD_DIM:])
    blk = lax.broadcasted_iota(jnp.int32, (n_sub, 1), 0)
    pres = []
    for g in range(N_KV):
        top = s_scr[g * n_sub:(g + 1) * n_sub, :HEAD_DIM]
        bot = s_scr[g * n_sub + 1:(g + 1) * n_sub + 1, HEAD_DIM:]
        if has_new:
            bot = jnp.where(blk == n_sub - 1, new_term[g:g + 1], bot)
        pres.append(top + bot + pe_const)
    out = _dot(_silu(jnp.concatenate(pres, axis=0)).astype(BF16), w2_ref[...])
    if not has_new:
        last = (lax.broadcasted_iota(jnp.int32, (n_rows, 1), 0) & (n_sub - 1)) == n_sub - 1
        out = jnp.where(last, 0.0, out)
    return out


def _compress_kv(k_pages, v_pages, k_new, v_new, k_w, v_w, scr, *, has_new):
    n_sub = len(k_pages) * (PAGE_SIZE // CMP_STRIDE)
    _regroup_pages(k_pages, scr[0])
    _regroup_pages(v_pages, scr[2])
    acc_k = _dot(_regrouped_lhs(scr[0], n_sub), k_w[0][...])
    acc_v = _dot(_regrouped_lhs(scr[2], n_sub), v_w[0][...])
    kc = _compress_finish_rows(acc_k, k_new, *k_w, scr[1], has_new=has_new)
    vc = _compress_finish_rows(acc_v, v_new, *v_w, scr[3], has_new=has_new)
    return kc, vc


def _compress_scratch(n_pages):
    n_rows = N_KV * n_pages * (PAGE_SIZE // CMP_STRIDE)
    return [pltpu.VMEM((CMP_STRIDE * N_KV * SLAB_PITCH, HEAD_DIM), F32),
            pltpu.VMEM((n_rows + SUBLANES, 2 * HEAD_DIM), F32)]


def _compress_prompt_kernel(pt_ref, *refs, n_pages):
    kp, vp = refs[:n_pages], refs[n_pages:2 * n_pages]
    wck, w2k, pek, wcv, w2v, pev, kc_ref, vc_ref, *scr = refs[2 * n_pages:]
    n_sub = n_pages * (PAGE_SIZE // CMP_STRIDE)
    assert n_sub & (n_sub - 1) == 0
    kc_ref[0], vc_ref[0] = _compress_kv(kp, vp, None, None, (wck, w2k, pek), (wcv, w2v, pev), scr, has_new=False)


def _page_specs(n_pages):
    return [pl.BlockSpec((1, PAGE_SIZE * N_KV, HEAD_DIM), lambda b, pt, p=p: (pt[b, p], 0, 0)) for p in range(n_pages)]


def _const_spec(shape):
    return pl.BlockSpec(shape, lambda b, pt: (0,) * len(shape))


def _compress_prompt(k_rows, v_rows, cw):
    bsz = k_rows.shape[0]
    t = k_rows.shape[1] // N_KV
    n_pages = t // PAGE_SIZE
    n_sub = t // CMP_STRIDE
    pt = jnp.arange(bsz * n_pages, dtype=jnp.int32).reshape(bsz, n_pages)
    kp = k_rows.reshape(bsz * n_pages, PAGE_SIZE * N_KV, HEAD_DIM)
    vp = v_rows.reshape(bsz * n_pages, PAGE_SIZE * N_KV, HEAD_DIM)
    wspecs = [_const_spec(a.shape) for a in cw]
    grid_spec = pltpu.PrefetchScalarGridSpec(
        num_scalar_prefetch=1,
        grid=(bsz,),
        in_specs=_page_specs(n_pages) * 2 + wspecs,
        out_specs=[pl.BlockSpec((1, N_KV * n_sub, HEAD_DIM), lambda b, pt: (b, 0, 0))] * 2,
        scratch_shapes=_compress_scratch(n_pages) * 2,
    )
    return pl.pallas_call(
        functools.partial(_compress_prompt_kernel, n_pages=n_pages),
        grid_spec=grid_spec,
        out_shape=[jax.ShapeDtypeStruct((bsz, N_KV * n_sub, HEAD_DIM), F32)] * 2,
        compiler_params=_params("parallel"),
        name="compress_prompt",
    )(pt, *([kp] * n_pages), *([vp] * n_pages), *cw)


def _head_rows(ref, base):
    return jnp.concatenate([ref[0, :, base + h * HEAD_DIM:base + (h + 1) * HEAD_DIM] for h in range(N_HEADS)], axis=0)


def _split_dot(a, b):
    hi = a.astype(BF16)
    lo = (a - hi.astype(F32)).astype(BF16)
    return _dot(hi, b) + _dot(lo, b)


def _compress_sample_kernel(pt_ref, ck_hbm, cv_hbm, newk, newv, wck, w2k, pek, wcv, w2v, pev, q_ref, tcs_ref, ovl_ref,
                            pow_ref, oc_ref, bits_ref, kbuf, vbuf, sem, *scr, n_pages, cur_blk):
    b = pl.program_id(0)
    slot = lax.rem(b, 2)

    def page_copies(seq, slot):
        out = []
        for p in range(n_pages):
            page = pt_ref[seq, p]
            out.append(pltpu.make_async_copy(ck_hbm.at[page], kbuf.at[slot, p], sem.at[slot, 0]))
            out.append(pltpu.make_async_copy(cv_hbm.at[page], vbuf.at[slot, p], sem.at[slot, 1]))
        return out

    @pl.when(b == 0)
    def _():
        for c in page_copies(0, 0):
            c.start()

    @pl.when(b + 1 < pl.num_programs(0))
    def _():
        for c in page_copies(b + 1, 1 - slot):
            c.start()

    for c in page_copies(b, slot):
        c.wait()
    kp = [kbuf.at[slot, p] for p in range(n_pages)]
    vp = [vbuf.at[slot, p] for p in range(n_pages)]
    kc, vc = _compress_kv(kp, vp, newk, newv, (wck, w2k, pek), (wcv, w2v, pev), scr, has_new=True)
    q16 = (_head_rows(q_ref, 0) * SCALE).astype(BF16)
    s = _dot_nt(q16, kc.astype(BF16)) + tcs_ref[...]
    e = jnp.exp(s - jnp.max(s, axis=-1, keepdims=True))
    p = e / jnp.sum(e, axis=-1, keepdims=True)
    oc_ref[0] = _dot(p.astype(BF16), vc.astype(BF16))
    imp_h = _split_dot(p, ovl_ref[...])
    imp = jnp.concatenate([jnp.sum(imp_h[GROUP_REP * g:GROUP_REP * (g + 1)], axis=0, keepdims=True)
                           for g in range(N_KV)], axis=0)
    lane = lax.broadcasted_iota(jnp.int32, (1, LANES), 1)
    forced = (lane == cur_blk) | (lane == 0)
    imp = jnp.where(forced, FORCE_SCORE, jnp.where(lane <= cur_blk, imp, -FORCE_SCORE))
    rank = jnp.zeros((N_KV, LANES), F32)
    for k in range(cur_blk + 1):
        ck = imp[:, k:k + 1]
        rank = rank + jnp.where(ck > imp, 1.0, jnp.where((ck == imp) & (lane > k), 1.0, 0.0))
    sel = jnp.where(rank < N_SEL, 1.0, 0.0) * pow_ref[...]
    lo = jnp.sum(jnp.where(lane < 16, sel, 0.0), axis=-1, keepdims=True).astype(jnp.int32)
    hi = jnp.sum(jnp.where((lane >= 16) & (lane < 32), sel, 0.0), axis=-1, keepdims=True).astype(jnp.int32)
    bits_ref[0] = jnp.broadcast_to(lo | (hi << 16), (N_KV, LANES))


def _compress_sample(cache_k, cache_v, page_table, new_k, new_v, cw, q, tcs, ovl, *, past):
    nb, n_pages = page_table.shape
    width = new_k.shape[-1]
    n_sub = n_pages * (PAGE_SIZE // CMP_STRIDE)
    pow2 = np.zeros((1, LANES), np.float32)
    pow2[0, :32] = 2.0 ** (np.arange(32) % 16)
    row_spec = pl.BlockSpec((1, 1, width), lambda b, pt: (b, 0, 0))
    grid_spec = pltpu.PrefetchScalarGridSpec(
        num_scalar_prefetch=1,
        grid=(nb,),
        in_specs=([pl.BlockSpec(memory_space=pl.ANY)] * 2 + [row_spec, row_spec] + [_const_spec(a.shape) for a in cw]
                  + [pl.BlockSpec((1, 1, q.shape[-1]), lambda b, pt: (b, 0, 0)),
                     _const_spec(tcs.shape), _const_spec(ovl.shape), _const_spec(pow2.shape)]),
        out_specs=[pl.BlockSpec((1, N_HEADS, HEAD_DIM), lambda b, pt: (b, 0, 0)),
                   pl.BlockSpec((1, N_KV, LANES), lambda b, pt: (b, 0, 0))],
        scratch_shapes=([pltpu.VMEM((2, n_pages) + cache_k.shape[1:], F32)] * 2
                        + [pltpu.SemaphoreType.DMA((2, 2))]
                        + _compress_scratch(n_pages) * 2),
    )
    return pl.pallas_call(
        functools.partial(_compress_sample_kernel, n_pages=n_pages, cur_blk=past // SLC_BLOCK),
        grid_spec=grid_spec,
        out_shape=[jax.ShapeDtypeStruct((nb, N_HEADS, HEAD_DIM), F32),
                   jax.ShapeDtypeStruct((nb, N_KV, LANES), jnp.int32)],
        compiler_params=_params("arbitrary"),
        name="compress_select_sample",
    )(page_table, cache_k, cache_v, new_k, new_v, *cw, q, tcs, ovl, jnp.asarray(pow2))


def _attn_prompt_kernel(q_ref, z0_ref, z1_ref, z2_ref, gt_ref, kc_ref, vc_ref, ks_ref, vs_ref, kw_ref, vw_ref,
                        t0_ref, t1_ref, tc_ref, ovl_ref, y_ref, m0_scr, m1_scr, acc0_scr, acc1_scr, sel_scr):
    m_scrs, acc_scrs = (m0_scr, m1_scr), (acc0_scr, acc1_scr)
    qt = pl.program_id(2)
    q = q_ref[...]
    qf = jnp.concatenate([q[:, r * HEAD_DIM:(r + 1) * HEAD_DIM] for r in range(GROUP_REP)], axis=0)
    qf = (qf * (SCALE * LOG2E)).astype(BF16)
    lane_q = lax.broadcasted_iota(jnp.int32, (1, TQ), 1)
    qpos = qt * TQ + lane_q
    key_l = lax.broadcasted_iota(jnp.int32, (TQ, 1), 0)
    gate_t = gt_ref[...].T
    z_refs = (z0_ref, z1_ref, z2_ref)

    def emit(br, r, o_t, first):
        cs = slice(r * HEAD_DIM, (r + 1) * HEAD_DIM)
        term = o_t.T * z_refs[br][:, cs]
        if first:
            y_ref[:, cs] = term
        else:
            y_ref[:, cs] += term

    blk_per_tile = TQ // SLC_BLOCK

    def compressed_and_select():
        n_cmp = kc_ref.shape[1]
        s_all = _dot_nt(kc_ref[0].astype(BF16), qf)
        vc_t = vc_ref[0].T.astype(BF16)
        cmp_end = lax.broadcasted_iota(jnp.int32, (n_cmp, 1), 0) * CMP_STRIDE + (CMP_LEN - 1)
        vis = qpos >= cmp_end
        any_vis = qpos >= CMP_LEN - 1
        row0 = pl.multiple_of(TQ // 2 - (TQ // CMP_STRIDE) * qt, CMP_STRIDE)
        psum = jnp.zeros((n_cmp, TQ), F32)
        for r in range(GROUP_REP):
            s = s_all[:, r * TQ:(r + 1) * TQ] + tc_ref[r, pl.ds(row0, n_cmp), :]
            s = jnp.where(vis, s, NEG_INF)
            e = jnp.exp2(s - jnp.max(s, axis=0, keepdims=True))
            p = jnp.where(any_vis, e / jnp.sum(e, axis=0, keepdims=True), 0.0)
            psum = psum + p
            emit(0, r, _dot(vc_t, p.astype(BF16)) * gate_t[r:r + 1, :], True)
        n_slc = ovl_ref.shape[0]
        psum_hi = psum.astype(BF16)
        imp = _dot(ovl_ref[...], psum_hi) + _dot(ovl_ref[...], (psum - psum_hi.astype(F32)).astype(BF16))
        blk = lax.broadcasted_iota(jnp.int32, (n_slc, 1), 0)
        cur = lax.shift_right_logical(qpos, int(math.log2(SLC_BLOCK)))
        forced = (blk == cur) | (blk == 0)
        imp = jnp.where(forced, FORCE_SCORE, jnp.where(blk <= cur, imp, -FORCE_SCORE))
        chunks = [imp[c:c + SUBLANES, :] for c in range(0, n_slc, SUBLANES)]
        ranks = [jnp.zeros((SUBLANES, TQ), F32) for _ in chunks]
        for k in range(n_slc):
            rk = imp[k:k + 1, :]
            for c, chunk in enumerate(chunks):
                lo = c * SUBLANES
                if lo > k:
                    beats = rk >= chunk
                elif lo + SUBLANES - 1 <= k:
                    beats = rk > chunk
                else:
                    beats = (rk > chunk) | ((rk == chunk) & (blk[lo:lo + SUBLANES] > k))
                ranks[c] = ranks[c] + jnp.where(beats, 1.0, 0.0)
        rank = jnp.concatenate(ranks, axis=0)
        sel = jnp.where(rank < min(N_SEL, n_slc), 1.0, 0.0)
        for t in range(n_slc // blk_per_tile):
            sel_scr[t, 0:blk_per_tile, :] = sel[t * blk_per_tile:(t + 1) * blk_per_tile, :]

    ones_rows = jnp.ones((ACC_ROWS - HEAD_DIM, TQ), BF16)

    refs = ((ks_ref, vs_ref), (kw_ref, vw_ref))

    def attend(tiles, first):
        k_ts, v_ts, masks = [], [], []
        for st, kt, kind in tiles:
            k_ref, v_ref = refs[st]
            start = pl.multiple_of(kt * TQ, TQ)
            k_ts.append(k_ref[pl.ds(start, TQ), :])
            v_t = v_ref[pl.ds(start, TQ), :].T.astype(BF16)
            v_ts.append(jnp.concatenate([v_t, ones_rows], axis=0))
            mask = None
            if kind == "diag":
                mask = key_l <= lane_q
            elif kind == "winfar":
                mask = key_l > lane_q
            if st == 0:
                sel4 = sel_scr[kt, 0:blk_per_tile, :]
                selm = jnp.concatenate(
                    [jnp.broadcast_to(sel4[j:j + 1, :], (SLC_BLOCK, TQ)) for j in range(blk_per_tile)], axis=0) > 0.5
                mask = selm if mask is None else mask & selm
            masks.append(mask)
        s_all = _dot_nt(k_ts[0] if len(k_ts) == 1 else jnp.concatenate(k_ts, axis=0), qf)
        for r in range(GROUP_REP):
            sl = slice(r * TQ, (r + 1) * TQ)
            for st in sorted(set(t[0] for t in tiles)):
                ss, vs = [], []
                for i, ((st_i, kt, kind), mask) in enumerate(zip(tiles, masks)):
                    if st_i != st:
                        continue
                    s = s_all[i * TQ:(i + 1) * TQ, sl]
                    if kind == "diag":
                        s = s + t0_ref[r]
                    elif kind == "near":
                        s = s + t1_ref[r]
                    if mask is not None:
                        s = jnp.where(mask, s, NEG_INF)
                    ss.append(s)
                    vs.append(v_ts[i])
                mx = jnp.max(ss[0], axis=0, keepdims=True)
                for s in ss[1:]:
                    mx = jnp.maximum(mx, jnp.max(s, axis=0, keepdims=True))
                if first:
                    m_new = mx
                    upd = None
                else:
                    m_old = m_scrs[st][:, sl]
                    m_new = jnp.maximum(m_old, mx)
                    upd = jnp.exp2(m_old - m_new) * acc_scrs[st][:, sl]
                for s, v_t in zip(ss, vs):
                    pv = _dot(v_t, jnp.exp2(s - m_new).astype(BF16))
                    upd = pv if upd is None else upd + pv
                acc_scrs[st][:, sl] = upd
                m_scrs[st][:, sl] = m_new

    compressed_and_select()
    attend([(0, qt, "diag"), (1, qt, "diag")], True)

    @pl.when(qt >= 1)
    def _():
        attend([(0, qt - 1, "near"), (1, qt - 1, "near")], False)

    @pl.when(qt >= 2)
    def _():
        n_far = qt - 1

        def far_pair(i, carry):
            attend([(0, 2 * i, "far"), (0, 2 * i + 1, "far")], False)
            return carry

        lax.fori_loop(0, lax.shift_right_logical(n_far, 1), far_pair, 0)

        @pl.when((n_far & 1) == 1)
        def _():
            attend([(1, qt - 2, "winfar"), (0, n_far - 1, "far")], False)

        @pl.when((n_far & 1) == 0)
        def _():
            attend([(1, qt - 2, "winfar")], False)

    for st in range(2):
        br = st + 1
        for r in range(GROUP_REP):
            sl = slice(r * TQ, (r + 1) * TQ)
            w = gate_t[br * GROUP_REP + r:br * GROUP_REP + r + 1, :] / acc_scrs[st][HEAD_DIM:HEAD_DIM + 1, sl]
            emit(br, r, acc_scrs[st][0:HEAD_DIM, sl] * w, False)


def _attn_prompt(proj, gates, kc, vc, k_slc, v_slc, k_win, v_win, t0, t1, tc, ovl_t, *, bsz, t):
    nq = t // TQ
    gw = GROUP_REP * HEAD_DIM
    assert TQ // 2 - (TQ // CMP_STRIDE) * (nq - 1) >= 0 and kc.shape[1] == N_KV * (t // CMP_STRIDE)
    n_cmp = t // CMP_STRIDE
    d_att = N_HEADS * HEAD_DIM
    zoff = d_att // gw
    row = lambda b, g, i: b * nq + i
    in_specs = [
        pl.BlockSpec((TQ, gw), lambda b, g, i: (row(b, g, i), g)),
    ] + [
        pl.BlockSpec((TQ, gw), lambda b, g, i, br=br: (row(b, g, i), zoff * (1 + br) + g)) for br in range(N_BRANCH)
    ] + [
        pl.BlockSpec((TQ, LANES), lambda b, g, i: (row(b, g, i), g)),
        pl.BlockSpec((1, n_cmp, HEAD_DIM), lambda b, g, i: (b, g, 0)),
        pl.BlockSpec((1, n_cmp, HEAD_DIM), lambda b, g, i: (b, g, 0)),
    ] + [pl.BlockSpec((None, None, t, HEAD_DIM), lambda b, g, i: (b, g, 0, 0))] * 4 + [
        pl.BlockSpec((GROUP_REP, TQ, TQ), lambda b, g, i: (g, 0, 0))] * 3 + [
        pl.BlockSpec(ovl_t.shape, lambda b, g, i: (0, 0)),
    ]
    return pl.pallas_call(
        _attn_prompt_kernel,
        grid=(bsz, N_KV, nq),
        in_specs=in_specs,
        out_specs=pl.BlockSpec((TQ, gw), lambda b, g, i: (row(b, g, i), g)),
        out_shape=jax.ShapeDtypeStruct((bsz * t, d_att), F32),
        scratch_shapes=[
            pltpu.VMEM((1, GROUP_REP * TQ), F32),
            pltpu.VMEM((1, GROUP_REP * TQ), F32),
            pltpu.VMEM((ACC_ROWS, GROUP_REP * TQ), F32),
            pltpu.VMEM((ACC_ROWS, GROUP_REP * TQ), F32),
            pltpu.VMEM((t // TQ, SUBLANES, TQ), F32),
        ],
        compiler_params=_params("parallel", "parallel", "arbitrary"),
        name="nsa_attention_prompt",
    )(proj, proj, proj, proj, gates, kc, vc, k_slc, v_slc, k_win, v_win, t0, t1, tc, ovl_t)


def _attn_sample_kernel(bits_ref, pt_ref, *refs, n_pages):
    kp, vp = refs[:n_pages], refs[n_pages:2 * n_pages]
    (kwin_ref, vwin_ref, nks_ref, nvs_ref, nkw_ref, nvw_ref, q_ref, gt_ref, oc_ref, ts_ref, tw_ref,
     y_ref, kwo_ref, vwo_ref) = refs[2 * n_pages:]
    b = pl.program_id(0)
    rows_pp = PAGE_SIZE * N_KV
    blk_rows = SLC_BLOCK * N_KV
    d_att = N_HEADS * HEAD_DIM
    n_win_rows = kwin_ref.shape[1]
    head_rows = functools.partial(_head_rows, q_ref)

    for src, new, dst in ((kwin_ref, nkw_ref, kwo_ref), (vwin_ref, nvw_ref, vwo_ref)):
        dst[0, 0:n_win_rows - N_KV, :] = src[0, N_KV:n_win_rows, :]
        dst[0, n_win_rows - N_KV:n_win_rows, :] = jnp.concatenate(
            [new[0, :, g * HEAD_DIM:(g + 1) * HEAD_DIM] for g in range(N_KV)], axis=0)

    def group_rows(ref):
        return jnp.concatenate([ref[0, :, (h // GROUP_REP) * HEAD_DIM:(h // GROUP_REP + 1) * HEAD_DIM]
                                for h in range(N_HEADS)], axis=0)

    def softmax_pv(s_tiles, v_tiles, s_new, v_new):
        m = s_new
        for s in s_tiles:
            m = jnp.maximum(m, jnp.max(s, axis=-1, keepdims=True))
        p_new = jnp.exp(s_new - m)
        l = p_new
        acc = p_new * v_new
        for s, v in zip(s_tiles, v_tiles):
            p = jnp.exp(s - m)
            l = l + jnp.sum(p, axis=-1, keepdims=True)
            acc = acc + _dot(p.astype(BF16), v.astype(BF16))
        return acc / l

    q32 = head_rows(0) * SCALE
    q16 = q32.astype(BF16)
    head = lax.broadcasted_iota(jnp.int32, (N_HEADS, 1), 0)
    bits = jnp.zeros((N_HEADS, 1), jnp.int32)
    for g in range(N_KV):
        bits = jnp.where((head >= g * GROUP_REP) & (head < (g + 1) * GROUP_REP), bits_ref[b, g], bits)
    lane_blk = lax.broadcasted_iota(jnp.int32, (1, rows_pp), 1) // blk_rows
    s_tiles, v_tiles = [], []
    for p in range(n_pages):
        s = _dot_nt(q16, kp[p][0].astype(BF16)) + ts_ref[:, p * rows_pp:(p + 1) * rows_pp]
        shift = jnp.broadcast_to(lane_blk + p * (rows_pp // blk_rows), s.shape)
        sel = lax.shift_right_logical(jnp.broadcast_to(bits, s.shape), shift) & 1
        s_tiles.append(jnp.where(sel == 1, s, NEG_INF))
        v_tiles.append(vp[p][0])
    tail = n_pages * rows_pp
    s_new = jnp.sum(q32 * group_rows(nks_ref), axis=-1, keepdims=True) + ts_ref[:, tail:tail + 1]
    o_slc = softmax_pv(s_tiles, v_tiles, s_new, group_rows(nvs_ref))
    s_tiles, v_tiles = [], []
    for p in range(n_win_rows // rows_pp):
        rows = slice(p * rows_pp, (p + 1) * rows_pp)
        s_tiles.append(_dot_nt(q16, kwin_ref[0, rows, :].astype(BF16)) + tw_ref[:, rows])
        v_tiles.append(vwin_ref[0, rows, :])
    s_new = jnp.sum(q32 * group_rows(nkw_ref), axis=-1, keepdims=True) + tw_ref[:, n_win_rows:n_win_rows + 1]
    o_win = softmax_pv(s_tiles, v_tiles, s_new, group_rows(nvw_ref))
    y = jnp.zeros((N_HEADS, HEAD_DIM), F32)
    for br, o in enumerate((oc_ref[0], o_slc, o_win)):
        cols = [(h // GROUP_REP) * LANES + br * GROUP_REP + h % GROUP_REP for h in range(N_HEADS)]
        gate = jnp.concatenate([gt_ref[0, :, c:c + 1] for c in cols], axis=0)
        y = y + gate * o * head_rows((1 + br) * d_att)
    y_ref[0] = y


def _attn_sample(cache_k, cache_v, page_table, bits, k_win, v_win, new_rows, proj, gates, o_cmp, ts, tw):
    nb, n_pages = page_table.shape
    rows_pp = PAGE_SIZE * N_KV
    win_spec = pl.BlockSpec((1, k_win.shape[1], HEAD_DIM), lambda b, bits, pt: (b, 0, 0))
    row_spec = lambda w: pl.BlockSpec((1, 1, w), lambda b, bits, pt: (b, 0, 0))
    const = lambda a: pl.BlockSpec(a.shape, lambda b, bits, pt: (0,) * a.ndim)
    page_specs = [pl.BlockSpec((1, rows_pp, HEAD_DIM), lambda b, bits, pt, p=p: (pt[b, p], 0, 0)) for p in range(n_pages)]
    grid_spec = pltpu.PrefetchScalarGridSpec(
        num_scalar_prefetch=2,
        grid=(nb,),
        in_specs=(page_specs * 2 + [win_spec] * 2
                  + [row_spec(N_KV * HEAD_DIM)] * 4
                  + [row_spec(proj.shape[-1]), row_spec(gates.shape[-1]),
                     pl.BlockSpec((1, N_HEADS, HEAD_DIM), lambda b, bits, pt: (b, 0, 0)),
                     const(ts), const(tw)]),
        out_specs=[pl.BlockSpec((1, N_HEADS, HEAD_DIM), lambda b, bits, pt: (b, 0, 0)), win_spec, win_spec],
    )
    return pl.pallas_call(
        functools.partial(_attn_sample_kernel, n_pages=n_pages),
        grid_spec=grid_spec,
        out_shape=[jax.ShapeDtypeStruct((nb, N_HEADS, HEAD_DIM), F32),
                   jax.ShapeDtypeStruct(k_win.shape, F32), jax.ShapeDtypeStruct(v_win.shape, F32)],
        compiler_params=_params("parallel"),
        name="nsa_attention_sample",
    )(bits, page_table, *([cache_k] * n_pages), *([cache_v] * n_pages), k_win, v_win, *new_rows,
      proj, gates, o_cmp, ts, tw)


def _overlap(n_cmp, n_slc):
    cs = np.arange(n_cmp) * CMP_STRIDE
    bs = np.arange(n_slc) * SLC_BLOCK
    ov = np.clip(np.minimum(cs[:, None] + CMP_LEN, bs[None, :] + SLC_BLOCK) - np.maximum(cs[:, None], bs[None, :]), 0, None)
    return (ov / CMP_LEN).astype(np.float32)


def _skip_unselected_pages(page_table, bits):
    nb, n_pages = page_table.shape
    blk_per_page = PAGE_SIZE // SLC_BLOCK
    assert n_pages * blk_per_page <= 32
    any_bits = functools.reduce(jnp.bitwise_or, [bits[:, g] for g in range(bits.shape[1])])
    shifts = blk_per_page * jnp.arange(n_pages, dtype=jnp.int32)
    needed = ((any_bits[:, None] >> shifts[None, :]) & ((1 << blk_per_page) - 1)) != 0
    seq = jnp.arange(nb, dtype=jnp.int32)[:, None]
    last_needed = lax.cummax(jnp.where(needed | (seq == 0), seq, 0), axis=0)
    return jnp.take_along_axis(page_table, last_needed, axis=0)


def _compress_weights(w1, w2, pe):
    half = CMP_STRIDE * HEAD_DIM
    wcat = jnp.concatenate([w1[:half], w1[half:]], axis=1).astype(BF16)
    pe8 = jnp.pad(pe.reshape(2, half), ((0, SUBLANES - 2), (0, 0)))
    return wcat, w2.astype(BF16), pe8


def kernel(x_prompt, x_sample, state_pool, cache_k_cmp, cache_v_cmp, cache_k_slc, cache_v_slc, state_k_win, state_v_win, page_table, c_prompt, c_sample, g_norm, w_ada, b_ada, w_in_a, w_grp, pool_scale, w_out_a, g_kv, w_kv, pe_k, w_ck1, w_ck2, pe_v, w_cv1, w_cv2, rel_bias, w_in_b, w_out_b, g_final):
    bsz, t, d = x_prompt.shape
    nb = x_sample.shape[0]
    n_pages = page_table.shape[1]
    past = n_pages * PAGE_SIZE
    d_att = N_HEADS * HEAD_DIM
    d_kv = N_KV * HEAD_DIM
    n_a = w_in_a.shape[0]
    assert n_a == 1 and w_in_b.shape[0] == 1 and x_sample.shape[1] == 1
    assert t % TQ == 0 and t >= WINDOW and past % PAGE_SIZE == 0 and state_k_win.shape[1] == WINDOW

    w_in_a16 = w_in_a[0].astype(BF16)
    w_grp16 = w_grp[0].astype(BF16)
    w_out_a16 = w_out_a[0].astype(BF16)
    w_kv16 = w_kv.astype(BF16)
    w_out_b16 = w_out_b[0].astype(BF16)
    n_qz = (1 + N_BRANCH) * d_att
    wg = w_in_b[0][:, n_qz:].reshape(d, N_BRANCH, N_KV, GROUP_REP).transpose(0, 2, 1, 3)
    wg = jnp.pad(wg.reshape(d, N_KV, N_BRANCH * GROUP_REP), ((0, 0), (0, 0), (0, LANES - N_BRANCH * GROUP_REP)))
    w_gate16 = wg.reshape(d, N_KV * LANES).astype(BF16)
    nsa_proj = functools.partial(_nm_matmul, w=w_in_b[0].astype(BF16), w_tail=w_gate16, n_cols=n_qz, tn=TN_PROJ,
                                 silu_cols=d_att)
    cw_k = _compress_weights(w_ck1, w_ck2, pe_k)
    cw_v = _compress_weights(w_cv1, w_cv2, pe_v)
    cw = cw_k + cw_v

    mod = _ada(jnp.concatenate([c_prompt, c_sample], axis=0), w_ada, b_ada)

    def modulation(l, lo, hi, per_row):
        parts = [mod[l, lo:hi, k * d:(k + 1) * d] for k in range(3)]
        return [p[None] if per_row else p[:, None] for p in parts]

    t0, t1, tc, ts, tw, tcs = _bias_tables(rel_bias, past=past, n_cmp_s=past // CMP_STRIDE)
    ts, tw, tcs = ts.reshape(N_HEADS, -1), tw.reshape(N_HEADS, -1), tcs.reshape(N_HEADS, -1)

    shift, scale, gate = modulation(0, 0, bsz, False)
    uz = _nm_matmul(x_prompt, g_norm[0], scale, shift, w_in_a16, n_cols=2 * d, tm=TM_PROJ, tn=TN_PROJ, name="in_proj_pool_prompt")
    pool_p = uz[:, t - POOL_BUF:, :d][None]
    x1 = _pool_prompt(uz, w_grp16, pool_scale[0], w_out_a16, x_prompt, gate, tm=256)
    kv_p = _kv_proj(x1, g_kv, w_kv16, tm=256, name="kv_proj_prompt")
    heads = lambda a: a.reshape(a.shape[0], -1, N_KV, HEAD_DIM)
    kv_state_p = [heads(a) for a in kv_p[:4]] + [heads(a[:, (t - WINDOW) * N_KV:]) for a in kv_p[4:6]]
    kc_p, vc_p = _compress_prompt(kv_p[0], kv_p[1], cw)
    shift, scale, gate = modulation(1, 0, bsz, False)
    proj, gates = nsa_proj(x1, g_norm[1], scale, shift, tm=TM_PROJ, name="in_proj_nsa_prompt")
    ovl_t = jnp.asarray(np.pad(_overlap(t // CMP_STRIDE - 1, t // SLC_BLOCK), ((0, 1), (0, 0))).T).astype(BF16)
    flat = lambda a: a.reshape(bsz * t, a.shape[-1])
    y1 = _attn_prompt(flat(proj), flat(gates), kc_p, vc_p, kv_p[6], kv_p[7], kv_p[8], kv_p[9],
                      t0, t1, tc, ovl_t, bsz=bsz, t=t)
    y_prompt = _proj_residual(y1.reshape(bsz, t, d_att), w_out_b16, x1, gate, g_final, tm=512, final_norm=True,
                              name="out_proj_nsa_prompt")

    xs = x_sample.reshape(1, nb, d)
    shift, scale, gate = modulation(0, bsz, bsz + nb, True)
    uz_s = _nm_matmul(xs, g_norm[0], scale, shift, w_in_a16, n_cols=2 * d, tm=nb, tn=512, name="in_proj_pool_sample")
    pool_s = jnp.concatenate([state_pool[:, :, 1:], uz_s[0, :, None, :d][None]], axis=2)
    y0_s = _pool_sample(uz_s, state_pool[0], w_grp16, pool_scale[0], qpos=past, tb=min(32, nb))
    x1_s = _proj_residual(y0_s, w_out_a16, xs, gate, g_final, tm=nb, final_norm=False, name="out_proj_pool_sample")
    kv_s = _kv_proj(x1_s, g_kv, w_kv16, tm=nb, name="kv_proj_sample")
    new_rows = [a.reshape(nb, 1, d_kv) for a in kv_s[:2 * N_BRANCH]]
    shift, scale, gate = modulation(1, bsz, bsz + nb, True)
    proj_s, gates_s = [a.reshape(nb, 1, -1) for a in nsa_proj(x1_s, g_norm[1], scale, shift, tm=nb, name="in_proj_nsa_sample")]
    n_cmp_s = past // CMP_STRIDE
    n_slc_s = past // SLC_BLOCK + 1
    ovl_s = jnp.asarray(np.tile(np.pad(_overlap(n_cmp_s, n_slc_s), ((0, 0), (0, LANES - n_slc_s))), (N_KV, 1))).astype(BF16)
    rows_of = lambda c: c.reshape(c.shape[0], PAGE_SIZE * N_KV, HEAD_DIM)
    o_cmp, bits = _compress_sample(rows_of(cache_k_cmp), rows_of(cache_v_cmp), page_table, new_rows[0], new_rows[1], cw,
                                   proj_s, tcs, ovl_s, past=past)
    bits = bits[:, :, 0]
    y1_s, k_win_s, v_win_s = _attn_sample(
        rows_of(cache_k_slc), rows_of(cache_v_slc), _skip_unselected_pages(page_table, bits), bits,
        state_k_win.reshape(nb, WINDOW * N_KV, HEAD_DIM), state_v_win.reshape(nb, WINDOW * N_KV, HEAD_DIM),
        new_rows[2:], proj_s, gates_s, o_cmp, ts, tw)
    y_sample = _proj_residual(y1_s.reshape(1, nb, d_att), w_out_b16, x1_s, gate, g_final, tm=nb, final_norm=True,
                              name="out_proj_nsa_sample").reshape(nb, 1, d)
    new4 = [a.reshape(nb, 1, N_KV, HEAD_DIM) for a in kv_s[:2 * N_BRANCH]]
    kv_state_s = new4[:4] + [k_win_s.reshape(state_k_win.shape), v_win_s.reshape(state_v_win.shape)]

    return (y_prompt, y_sample, pool_p, *kv_state_p, pool_s, *kv_state_s)
```
